```python
import jax
import jax.numpy as jnp
from jax import lax
import numpy as np


D_MODEL = 4096
BATCH = 1
SEQ = 8192
DEPTH = 4

GRID_W = 64
CTX_LEN = 256
EPS = 1e-6
N_MOD = 6

NA_HEADS = 12
NA_HEAD_DIM = 128
NA_WIN_H = 8
NA_WIN_W = 16
NA_WIDTH = NA_HEADS * NA_HEAD_DIM

FN_GROUPS = 8
FN_GROUP_DIM = 128
FN_WIDTH = FN_GROUPS * FN_GROUP_DIM

GLA_HEADS = 4
GLA_DK = 192
GLA_DV = 384
GLA_KW = GLA_HEADS * GLA_DK
GLA_VW = GLA_HEADS * GLA_DV
GLA_GATE_RANK = 16
GLA_GATE_TEMP = 16.0
GLA_CHUNK = 64
ROPE_THETA = 10000.0

MIX_WIDTH = NA_WIDTH + FN_WIDTH + GLA_VW
PROJ_SIZES = (NA_WIDTH, NA_WIDTH, NA_WIDTH, FN_WIDTH, GLA_KW, GLA_KW, GLA_VW, GLA_VW, GLA_GATE_RANK, GLA_GATE_RANK)
IN_WIDTH = 3 * NA_WIDTH + FN_WIDTH + 2 * GLA_KW + 2 * GLA_VW + 2 * GLA_GATE_RANK

MOE_GROUPS = 4
MOE_EXPERTS_PER_GROUP = 8
MOE_EXPERTS = MOE_GROUPS * MOE_EXPERTS_PER_GROUP
MOE_TOP_K = 2
MOE_HIDDEN = 192

F32 = jnp.float32

kernel_name = 'hybrid_natten_fnet_gla_hmoe_prefix'


def rms_norm(x, g):
    xf = x.astype(F32)
    y = xf * lax.rsqrt(jnp.mean(xf * xf, axis=-1, keepdims=True) + EPS)
    return (y * g.astype(F32)).astype(x.dtype)


def modulate(h, shift, scale):
    return h * (1 + scale) + shift


def heads(t, n):
    return t.reshape(t.shape[0], t.shape[1], n, -1)


def split_proj(u):
    offs = [int(o) for o in np.cumsum(PROJ_SIZES)[:-1]]
    return jnp.split(u, offs, axis=-1)


def axial_rope(t, row, col):
    seg = t.shape[-1] // 2
    half = seg // 2
    inv_freq = ROPE_THETA ** (-jnp.arange(half, dtype=F32) / half)
    tf = t.astype(F32)

    def rot(u, p):
        ang = p.astype(F32)[:, None] * inv_freq
        cos = jnp.cos(ang)[None, :, None, :]
        sin = jnp.sin(ang)[None, :, None, :]
        u1, u2 = u[..., :half], u[..., half:]
        return jnp.concatenate([u1 * cos - u2 * sin, u1 * sin + u2 * cos], axis=-1)

    return jnp.concatenate([rot(tf[..., :seg], row), rot(tf[..., seg:], col)], axis=-1).astype(t.dtype)


def full_attention(q, k, v):
    b, l, h, dh = q.shape
    s = jnp.einsum('bqhd,bkhd->bhqk', q, k, preferred_element_type=F32) * dh ** -0.5
    p = jax.nn.softmax(s, axis=-1).astype(v.dtype)
    return jnp.einsum('bhqk,bkhd->bqhd', p, v).reshape(b, l, h * dh)


def neighbourhood_attention(q, k, v, k_ctx, v_ctx, rpb):
    b, s, h, dh = q.shape
    rows = s // GRID_W
    kh = min(NA_WIN_H, rows)
    n_loc = kh * NA_WIN_W
    qg = q.reshape(b, rows, GRID_W, h, dh)
    kg = k.reshape(b, rows, GRID_W, h, dh)
    vg = v.reshape(b, rows, GRID_W, h, dh)
    row_start = jnp.clip(jnp.arange(rows) - kh // 2, 0, rows - kh)
    cols = jnp.arange(GRID_W)
    col_idx = jnp.clip(cols - NA_WIN_W // 2, 0, GRID_W - NA_WIN_W)[:, None] + jnp.arange(NA_WIN_W)
    dc_idx = col_idx - cols[:, None] + (NA_WIN_W - 1)
    scale = dh ** -0.5

    def row_block(r):
        r0 = row_start[r]
        kb = lax.dynamic_slice_in_dim(kg, r0, kh, axis=1)[:, :, col_idx]
        vb = lax.dynamic_slice_in_dim(vg, r0, kh, axis=1)[:, :, col_idx]
        kb = kb.transpose(0, 2, 1, 3, 4, 5).reshape(b, GRID_W, n_loc, h, dh)
        vb = vb.transpose(0, 2, 1, 3, 4, 5).reshape(b, GRID_W, n_loc, h, dh)
        qb = lax.dynamic_index_in_dim(qg, r, axis=1, keepdims=False)
        dr_idx = r0 + jnp.arange(kh) - r + (NA_WIN_H - 1)
        bias = rpb[:, dr_idx][:, :, dc_idx].transpose(0, 2, 1, 3).reshape(h, GRID_W, n_loc)
        s_loc = jnp.einsum('bqhd,bqkhd->bhqk', qb, kb, preferred_element_type=F32) * scale + bias.astype(F32)
        s_ctx = jnp.einsum('bqhd,bkhd->bhqk', qb, k_ctx, preferred_element_type=F32) * scale
        p = jax.nn.softmax(jnp.concatenate([s_loc, s_ctx], axis=-1), axis=-1).astype(v.dtype)
        return (jnp.einsum('bhqk,bqkhd->bqhd', p[..., :n_loc], vb)
                + jnp.einsum('bhqk,bkhd->bqhd', p[..., n_loc:], v_ctx))

    out = lax.map(row_block, jnp.arange(rows))
    return out.transpose(1, 0, 2, 3, 4).reshape(b, s, h * dh)


def fourier_mix(u, w_fn):
    b, l, _ = u.shape
    ug = u.reshape(b, l, FN_GROUPS, FN_GROUP_DIM).astype(F32)
    f = jnp.fft.fftn(ug, axes=(1, 3), norm='ortho').real.astype(u.dtype)
    return jnp.einsum('blgc,gcd->blgd', f, w_fn).reshape(b, l, FN_WIDTH)


def gla_log_decay(lr_f, lr_b, w_g2, b_g):
    la_f = jax.nn.log_sigmoid((lr_f @ w_g2[0] + b_g[0]).astype(F32)) / GLA_GATE_TEMP
    la_b = jax.nn.log_sigmoid((lr_b @ w_g2[1] + b_g[1]).astype(F32)) / GLA_GATE_TEMP
    return heads(la_f, GLA_HEADS), heads(la_b, GLA_HEADS)


def gla_chunked(q, k, v, log_a, s0):
    b, l, h, _ = q.shape
    dv = v.shape[-1]
    n = l // GLA_CHUNK

    def chunks(t):
        return t.reshape(b, n, GLA_CHUNK, h, t.shape[-1]).transpose(1, 0, 3, 2, 4).astype(F32)

    qc, kc, vc = chunks(q), chunks(k), chunks(v)
    bc = jnp.cumsum(chunks(log_a), axis=3)
    tri = jnp.tril(jnp.ones((GLA_CHUNK, GLA_CHUNK), dtype=bool))

    def step(state, inp):
        qi, ki, vi, bi = inp
        o_inter = jnp.einsum('bhck,bhkv->bhcv', qi * jnp.exp(bi), state)
        diff = jnp.where(tri[:, :, None], bi[:, :, :, None, :] - bi[:, :, None, :, :], -jnp.inf)
        attn = jnp.einsum('bhik,bhjk,bhijk->bhij', qi, ki, jnp.exp(diff))
        o_intra = jnp.einsum('bhij,bhjv->bhiv', attn, vi)
        b_last = bi[:, :, -1:, :]
        new_state = (jnp.exp(b_last[:, :, 0, :, None]) * state
                     + jnp.einsum('bhck,bhcv->bhkv', ki * jnp.exp(b_last - bi), vi))
        return new_state, o_inter + o_intra

    s_fin, o = lax.scan(step, s0, (qc, kc, vc, bc))
    return o.transpose(1, 0, 3, 2, 4).reshape(b, l, h, dv), s_fin


def gla_final_state(k, v, log_a):
    bcum = jnp.cumsum(log_a.astype(F32), axis=1)
    w = jnp.exp(bcum[:, -1:] - bcum)
    return jnp.einsum('blhk,blhv->bhkv', k.astype(F32) * w, v.astype(F32))


def gla_bidirectional(q, k, v, la_f, la_b, s0_f, s0_b):
    o_f, s_f = gla_chunked(q, k, v, la_f, s0_f)
    o_b, s_b = gla_chunked(jnp.flip(q, 1), jnp.flip(k, 1), jnp.flip(v, 1), jnp.flip(la_b, 1), s0_b)
    return o_f + jnp.flip(o_b, 1), s_f, s_b


def gla_output(o, r, g_out):
    b, l = o.shape[:2]
    on = o * lax.rsqrt(jnp.mean(o * o, axis=-1, keepdims=True) + EPS)
    on = on.reshape(b, l, GLA_VW) * g_out.astype(F32)
    return (on * jax.nn.silu(r.astype(F32))).astype(r.dtype)


def hier_moe(h, w_rg, b_rg, w_re, b_re, w_gu, w_dn):
    b, l, d = h.shape
    t = h.reshape(-1, d)
    g_logits = (t @ w_rg + b_rg).astype(F32)
    g_top = jnp.argmax(g_logits, axis=-1)
    g_w = jnp.take_along_axis(jax.nn.softmax(g_logits, axis=-1), g_top[:, None], axis=-1)
    e_logits = (t @ w_re + b_re).astype(F32).reshape(-1, MOE_GROUPS, MOE_EXPERTS_PER_GROUP)
    e_in_group = jnp.take_along_axis(e_logits, g_top[:, None, None], axis=1)[:, 0]
    top_v, top_i = lax.top_k(e_in_group, MOE_TOP_K)
    e_w = jax.nn.softmax(top_v, axis=-1) * g_w
    e_id = g_top[:, None] * MOE_EXPERTS_PER_GROUP + top_i
    comb = jnp.sum(jax.nn.one_hot(e_id, MOE_EXPERTS, dtype=F32) * e_w[..., None], axis=1)
    gu = jnp.einsum('td,edf->etf', t, w_gu)
    gate, up = jnp.split(gu, 2, axis=-1)
    a = jax.nn.silu(gate) * up * comb.T[:, :, None].astype(t.dtype)
    return jnp.einsum('etf,efd->td', a, w_dn).reshape(b, l, d)


def trunk_layer(x, xc, mod, mod_c, g_mix, w_in, rpb, w_fn, w_g2, b_g, g_gla, w_out,
                g_ffn, w_rg, b_rg, w_re, b_re, w_gu, w_dn, row, col, update_ctx):
    sh1, sc1, gt1, sh2, sc2, gt2 = [mod[:, i, None, :] for i in range(N_MOD)]
    csh1, csc1, cgt1, csh2, csc2, cgt2 = [mod_c[:, i, None, :] for i in range(N_MOD)]
    (na_q, na_k, na_v, fn_u, g_q, g_k, g_v, g_r, lr_f, lr_b) = split_proj(
        modulate(rms_norm(x, g_mix), sh1, sc1) @ w_in)
    (na_qc, na_kc, na_vc, fn_uc, g_qc, g_kc, g_vc, g_rc, lr_fc, lr_bc) = split_proj(
        modulate(rms_norm(xc, g_mix), csh1, csc1) @ w_in)

    k_na_c, v_na_c = heads(na_kc, NA_HEADS), heads(na_vc, NA_HEADS)
    k_gc, v_gc = heads(g_kc, GLA_HEADS), heads(g_vc, GLA_HEADS)
    la_fc, la_bc = gla_log_decay(lr_fc, lr_bc, w_g2, b_g)
    if update_ctx:
        zero = jnp.zeros((xc.shape[0], GLA_HEADS, GLA_DK, GLA_DV), F32)
        o_gc, s_f, s_b = gla_bidirectional(heads(g_qc, GLA_HEADS) * GLA_DK ** -0.5, k_gc, v_gc,
                                           la_fc, la_bc, zero, zero)
        mix_c = jnp.concatenate([full_attention(heads(na_qc, NA_HEADS), k_na_c, v_na_c),
                                 fourier_mix(fn_uc, w_fn),
                                 gla_output(o_gc, g_rc, g_gla)], axis=-1)
        xc = xc + cgt1 * (mix_c @ w_out)
        xc = xc + cgt2 * hier_moe(modulate(rms_norm(xc, g_ffn), csh2, csc2),
                                  w_rg, b_rg, w_re, b_re, w_gu, w_dn)
    else:
        s_f = gla_final_state(k_gc, v_gc, la_fc)
        s_b = gla_final_state(jnp.flip(k_gc, 1), jnp.flip(v_gc, 1), jnp.flip(la_bc, 1))
        xc = None

    la_f, la_b = gla_log_decay(lr_f, lr_b, w_g2, b_g)
    q_g = axial_rope(heads(g_q, GLA_HEADS), row, col) * GLA_DK ** -0.5
    k_g = axial_rope(heads(g_k, GLA_HEADS), row, col)
    o_g, _, _ = gla_bidirectional(q_g, k_g, heads(g_v, GLA_HEADS), la_f, la_b, s_f, s_b)
    mix = jnp.concatenate([
        neighbourhood_attention(heads(na_q, NA_HEADS), heads(na_k, NA_HEADS), heads(na_v, NA_HEADS),
                                k_na_c, v_na_c, rpb),
        fourier_mix(fn_u, w_fn),
        gla_output(o_g, g_r, g_gla)], axis=-1)
    x = x + gt1 * (mix @ w_out)
    x = x + gt2 * hier_moe(modulate(rms_norm(x, g_ffn), sh2, sc2), w_rg, b_rg, w_re, b_re, w_gu, w_dn)
    return x, xc


def setup_inputs(seed: int = 0) -> dict:
    key = jax.random.key(seed)
    ks = jax.random.split(key, 22)
    d = D_MODEL
    n = DEPTH

    def nrm(k, shape, scale):
        return jax.random.normal(k, shape, F32) * scale

    return {
        'x': nrm(ks[0], (BATCH, SEQ, d), 1.0),
        'c': nrm(ks[1], (BATCH, d), 1.0),
        'ctx': nrm(ks[2], (BATCH, CTX_LEN, d), 1.0),
        'c_ctx': nrm(ks[3], (d,), 1.0),
        'w_ada': nrm(ks[4], (n, d, N_MOD * d), 0.5 * d ** -0.5),
        'b_ada': nrm(ks[5], (n, N_MOD * d), 0.02),
        'g_mix': 1.0 + nrm(ks[6], (n, d), 0.02),
        'w_in': nrm(ks[7], (n, d, IN_WIDTH), d ** -0.5),
        'rpb': nrm(ks[8], (n, NA_HEADS, 2 * NA_WIN_H - 1, 2 * NA_WIN_W - 1), 0.1),
        'w_fn': nrm(ks[9], (n, FN_GROUPS, FN_GROUP_DIM, FN_GROUP_DIM), FN_GROUP_DIM ** -0.5),
        'w_g2': nrm(ks[10], (n, 2, GLA_GATE_RANK, GLA_KW), GLA_GATE_RANK ** -0.5),
        'b_g': nrm(ks[11], (n, 2, GLA_KW), 0.1),
        'g_gla': 1.0 + nrm(ks[12], (n, GLA_VW), 0.02),
        'w_out': nrm(ks[13], (n, MIX_WIDTH, d), MIX_WIDTH ** -0.5),
        'g_ffn': 1.0 + nrm(ks[14], (n, d), 0.02),
        'w_rg': nrm(ks[15], (n, d, MOE_GROUPS), d ** -0.5),
        'b_rg': nrm(ks[16], (n, MOE_GROUPS), 0.01),
        'w_re': nrm(ks[17], (n, d, MOE_EXPERTS), d ** -0.5),
        'b_re': nrm(ks[18], (n, MOE_EXPERTS), 0.01),
        'w_gu': nrm(ks[19], (n, MOE_EXPERTS, d, 2 * MOE_HIDDEN), d ** -0.5),
        'w_dn': nrm(ks[20], (n, MOE_EXPERTS, MOE_HIDDEN, d), MOE_HIDDEN ** -0.5),
        'g_final': 1.0 + nrm(ks[21], (d,), 0.02),
    }


def reference(x, c, ctx, c_ctx, w_ada, b_ada, g_mix, w_in, rpb, w_fn, w_g2, b_g, g_gla, w_out,
              g_ffn, w_rg, b_rg, w_re, b_re, w_gu, w_dn, g_final):
    pos = jnp.arange(x.shape[1])
    row = pos // GRID_W
    col = pos % GRID_W
    xc = ctx
    for l in range(DEPTH):
        mod = (jax.nn.silu(c) @ w_ada[l] + b_ada[l]).reshape(c.shape[0], N_MOD, D_MODEL)
        mod_c = (jax.nn.silu(c_ctx)[None] @ w_ada[l] + b_ada[l]).reshape(1, N_MOD, D_MODEL)
        x, xc = trunk_layer(x, xc, mod, mod_c, g_mix[l], w_in[l], rpb[l], w_fn[l], w_g2[l], b_g[l],
                            g_gla[l], w_out[l], g_ffn[l], w_rg[l], b_rg[l], w_re[l], b_re[l],
                            w_gu[l], w_dn[l], row, col, l < DEPTH - 1)
    return rms_norm(x, g_final)
```

```python
import functools
import math

import numpy as np
import jax
import jax.numpy as jnp
from jax import lax
from jax.experimental import pallas as pl
from jax.experimental.pallas import tpu as pltpu

F32 = jnp.float32
BF16 = jnp.bfloat16

D_MODEL = 4096
DEPTH = 4
GRID_W = 64
EPS = 1e-6
N_MOD = 6

NA_HEADS = 12
NA_HEAD_DIM = 128
NA_WIN_H = 8
NA_WIN_W = 16
NA_WIDTH = NA_HEADS * NA_HEAD_DIM

FN_GROUPS = 8
FN_GROUP_DIM = 128
FN_WIDTH = FN_GROUPS * FN_GROUP_DIM

GLA_HEADS = 4
GLA_DK = 192
GLA_DKP = 256
GLA_DV = 384
GLA_KW = GLA_HEADS * GLA_DK
GLA_VW = GLA_HEADS * GLA_DV
GLA_GATE_RANK = 16
GLA_GATE_TEMP = 16.0
GLA_CHUNK = 64
GLA_SUB = 16
ROPE_THETA = 10000.0

MOE_GROUPS = 4
MOE_EXPERTS_PER_GROUP = 8
MOE_EXPERTS = MOE_GROUPS * MOE_EXPERTS_PER_GROUP
MOE_HIDDEN = 192

OFF_NA = 0
OFF_FN = 3 * NA_WIDTH
OFF_GQK = OFF_FN + FN_WIDTH
OFF_GVR = OFF_GQK + 2 * GLA_KW
OFF_LR = OFF_GVR + 2 * GLA_VW

LANE = 128
NEG = -1e30
V7X_VMEM_BUDGET = 56 * 1024 * 1024


def _params(sem, vmem_bytes):
    return pltpu.CompilerParams(dimension_semantics=sem,
                                vmem_limit_bytes=int(min(max(vmem_bytes, 16 << 20), V7X_VMEM_BUDGET)))


def _dot(a, b):
    return jnp.dot(a, b, preferred_element_type=F32)


def _dot_nt(a, b):
    return lax.dot_general(a, b, (((1,), (1,)), ((), ())), preferred_element_type=F32)


def _dot_tn(a, b):
    return lax.dot_general(a, b, (((0,), (0,)), ((), ())), preferred_element_type=F32)


def _silu(x):
    return x * (1.0 / (1.0 + jnp.exp(-x)))


def _ada_kernel(c_ref, w_ref, b_ref, o_ref):
    s = _silu(c_ref[...]).astype(BF16)
    o_ref[0] = _dot(s, w_ref[0].astype(BF16)) + b_ref[0]


def ada_mod(cc, w_ada, b_ada, tn=512):
    n, d, nd = w_ada.shape
    return pl.pallas_call(
        _ada_kernel,
        grid=(n, nd // tn),
        in_specs=[pl.BlockSpec((8, d), lambda l, j: (0, 0)),
                  pl.BlockSpec((1, d, tn), lambda l, j: (l, 0, j)),
                  pl.BlockSpec((1, 1, tn), lambda l, j: (l, 0, j))],
        out_specs=pl.BlockSpec((1, 8, tn), lambda l, j: (l, 0, j)),
        out_shape=jax.ShapeDtypeStruct((n, 8, nd), F32),
        compiler_params=_params(("arbitrary", "arbitrary"), 3 * d * tn * 4 + (4 << 20)),
        name="ada_mod",
    )(cc, w_ada, b_ada.reshape(n, 1, nd))


def _normed(x_ref, g_ref, sh_ref, sc_ref):
    x = x_ref[...]
    y = x * lax.rsqrt(jnp.mean(x * x, axis=-1, keepdims=True) + EPS)
    return (y * g_ref[...]) * (1.0 + sc_ref[...]) + sh_ref[...]


def _norm_proj_kernel(x_ref, g_ref, sh_ref, sc_ref, w_ref, xn_ref, p_ref):
    hb = _normed(x_ref, g_ref, sh_ref, sc_ref).astype(BF16)
    xn_ref[...] = hb
    p_ref[...] = _dot(hb, w_ref[...].astype(BF16))


def _lane_pick(v, lane, idx):
    return jnp.sum(jnp.where(lane == idx, v, 0.0), axis=-1, keepdims=True)


def _norm_route_kernel(x_ref, g_ref, sh_ref, sc_ref, w_ref, b_ref, xn_ref, comb_ref):
    hb = _normed(x_ref, g_ref, sh_ref, sc_ref).astype(BF16)
    xn_ref[...] = hb
    logits = _dot(hb, w_ref[...].astype(BF16)) + b_ref[...]
    lane = lax.broadcasted_iota(jnp.int32, logits.shape, 1).astype(F32)
    far = float(4 * LANE)
    is_g = (lane >= MOE_EXPERTS) & (lane < MOE_EXPERTS + MOE_GROUPS)
    gl = jnp.where(is_g, logits, NEG)
    gmax = jnp.max(gl, axis=-1, keepdims=True)
    g_top = jnp.min(jnp.where(gl == gmax, lane, far), axis=-1, keepdims=True) - MOE_EXPERTS
    g_w = 1.0 / jnp.sum(jnp.where(is_g, jnp.exp(gl - gmax), 0.0), axis=-1, keepdims=True)
    in_grp = (lane >= g_top * MOE_EXPERTS_PER_GROUP) & (lane < (g_top + 1) * MOE_EXPERTS_PER_GROUP)
    el = jnp.where(in_grp, logits, NEG)
    v1 = jnp.max(el, axis=-1, keepdims=True)
    i1 = jnp.min(jnp.where(el == v1, lane, far), axis=-1, keepdims=True)
    el2 = jnp.where(lane == i1, NEG, el)
    v2 = jnp.max(el2, axis=-1, keepdims=True)
    i2 = jnp.min(jnp.where(el2 == v2, lane, far), axis=-1, keepdims=True)
    e2 = jnp.exp(v2 - v1)
    w1 = g_w / (1.0 + e2)
    w2 = g_w * e2 / (1.0 + e2)
    comb_ref[...] = jnp.where(lane == i1, w1, 0.0) + jnp.where(lane == i2, w2, 0.0)


def norm_proj(x, g, sh, sc, w_small, b_small=None, tm=256):
    t, d = x.shape
    row = lambda i: (0, 0)
    vec = pl.BlockSpec((1, d), row)
    ins = [pl.BlockSpec((tm, d), lambda i: (i, 0)), vec, vec, vec, pl.BlockSpec((d, LANE), row)]
    args = [x, g.reshape(1, d), sh.reshape(1, d), sc.reshape(1, d), w_small]
    if b_small is None:
        body = _norm_proj_kernel
    else:
        body = _norm_route_kernel
        ins.append(pl.BlockSpec((1, LANE), row))
        args.append(b_small)
    return pl.pallas_call(
        body,
        grid=(t // tm,),
        in_specs=ins,
        out_specs=[pl.BlockSpec((tm, d), lambda i: (i, 0)), pl.BlockSpec((tm, LANE), lambda i: (i, 0))],
        out_shape=[jax.ShapeDtypeStruct((t, d), BF16), jax.ShapeDtypeStruct((t, LANE), F32)],
        compiler_params=_params(("arbitrary",), 6 * tm * d * 4 + 4 * d * LANE * 4),
        name="norm_proj" if b_small is None else "norm_route",
    )(*args)


def _final_norm_kernel(x_ref, g_ref, o_ref):
    x = x_ref[...]
    o_ref[...] = x * lax.rsqrt(jnp.mean(x * x, axis=-1, keepdims=True) + EPS) * g_ref[...]


def final_norm(x, g, tm=256):
    t, d = x.shape
    return pl.pallas_call(
        _final_norm_kernel,
        grid=(t // tm,),
        in_specs=[pl.BlockSpec((tm, d), lambda i: (i, 0)), pl.BlockSpec((1, d), lambda i: (0, 0))],
        out_specs=pl.BlockSpec((tm, d), lambda i: (i, 0)),
        out_shape=jax.ShapeDtypeStruct((t, d), F32),
        compiler_params=_params(("arbitrary",), 6 * tm * d * 4),
        name="final_norm",
    )(x, g.reshape(1, d))


def _mm_kernel(a_ref, w_ref, o_ref):
    o_ref[...] = _dot(a_ref[...], w_ref[...].astype(BF16)).astype(o_ref.dtype)


def _mm_res_kernel(a_ref, w_ref, r_ref, g_ref, o_ref):
    o_ref[...] = r_ref[...] + g_ref[...] * _dot(a_ref[...], w_ref[...].astype(BF16))


def matmul(a, w, w_lead, col0, ncols, tn, tm, out_dtype, res=None, gate=None, name="mm"):
    t, k = a.shape
    tm = min(tm, t)
    nlead = len(w_lead)
    j0 = col0 // tn
    w_spec = pl.BlockSpec((None,) * nlead + (k, tn), lambda i, j: tuple(w_lead) + (0, j + j0))
    ins = [pl.BlockSpec((tm, k), lambda i, j: (i, 0)), w_spec]
    args = [a, w]
    wbytes = jnp.dtype(w.dtype).itemsize
    vmem = 2 * tm * k * 2 + 3 * k * tn * wbytes + 6 * tm * tn * 4
    if res is None:
        body = _mm_kernel
    else:
        body = _mm_res_kernel
        ins += [pl.BlockSpec((tm, tn), lambda i, j: (i, j)), pl.BlockSpec((1, tn), lambda i, j: (0, j))]
        args += [res, gate.reshape(1, ncols)]
    return pl.pallas_call(
        body,
        grid=(t // tm, ncols // tn),
        in_specs=ins,
        out_specs=pl.BlockSpec((tm, tn), lambda i, j: (i, j)),
        out_shape=jax.ShapeDtypeStruct((t, ncols), out_dtype),
        compiler_params=_params(("arbitrary", "arbitrary"), vmem),
        name=name,
    )(*args)


NA_QROWS = 4
NA_SLAB = 12


def na_tables(rpb, rows):
    nblk = rows // NA_QROWS
    a = np.arange(NA_QROWS)[:, None, None, None]
    qc = np.arange(GRID_W)[None, :, None, None]
    b = np.arange(NA_SLAB)[None, None, :, None]
    kc = np.arange(GRID_W)[None, None, None, :]
    c0 = np.clip(qc - NA_WIN_W // 2, 0, GRID_W - NA_WIN_W)
    dr_l, dc_l, ok_l = [], [], []
    for i in (0, 1, nblk - 1):
        base = int(np.clip(i * NA_QROWS - NA_WIN_H // 2, 0, rows - NA_SLAB))
        r = i * NA_QROWS + a
        r0 = np.clip(r - NA_WIN_H // 2, 0, rows - NA_WIN_H)
        krow = base + b
        ok = (krow >= r0) & (krow < r0 + NA_WIN_H) & (kc >= c0) & (kc < c0 + NA_WIN_W)
        dr = np.clip(krow - r + NA_WIN_H - 1, 0, 2 * NA_WIN_H - 2)
        dc = np.clip(kc - qc + NA_WIN_W - 1, 0, 2 * NA_WIN_W - 2)
        shape = (NA_QROWS, GRID_W, NA_SLAB, GRID_W)
        n = NA_QROWS * GRID_W
        m = NA_SLAB * GRID_W
        dr_l.append(np.broadcast_to(dr, shape).reshape(n, m))
        dc_l.append(np.broadcast_to(dc, shape).reshape(n, m))
        ok_l.append(np.broadcast_to(ok, shape).reshape(n, m))
    dr = jnp.asarray(np.stack(dr_l))
    dc = jnp.asarray(np.stack(dc_l))
    ok = jnp.asarray(np.stack(ok_l))
    bias = rpb.astype(F32)[:, dr, dc]
    return jnp.where(ok[None], bias, NEG).transpose(1, 0, 2, 3)


def _na_kernel(q_ref, k_ref, v_ref, kc_ref, vc_ref, t_ref, o_ref, *, rows):
    i = pl.program_id(1)
    base = jnp.clip(i * NA_QROWS - NA_WIN_H // 2, 0, rows - NA_SLAB) * GRID_W
    base = pl.multiple_of(base, NA_QROWS * GRID_W)
    n_keys = NA_SLAB * GRID_W
    scale = NA_HEAD_DIM ** -0.5
    q = q_ref[...]
    k = k_ref[pl.ds(base, n_keys), :]
    v = v_ref[pl.ds(base, n_keys), :]
    s = _dot_nt(q, k) * scale + t_ref[...]
    sc = _dot_nt(q, kc_ref[...]) * scale
    m = jnp.maximum(jnp.max(s, axis=-1, keepdims=True), jnp.max(sc, axis=-1, keepdims=True))
    p = jnp.exp(s - m)
    pc = jnp.exp(sc - m)
    den = jnp.sum(p, axis=-1, keepdims=True) + jnp.sum(pc, axis=-1, keepdims=True)
    o = _dot(p.astype(BF16), v) + _dot(pc.astype(BF16), vc_ref[...])
    o_ref[...] = (o / den).astype(o_ref.dtype)


def neighbourhood_attention(u, uc, tables):
    s = u.shape[0]
    c = uc.shape[0]
    rows = s // GRID_W
    nblk = rows // NA_QROWS
    tq = NA_QROWS * GRID_W
    h_ = NA_HEADS
    hd = NA_HEAD_DIM

    def variant(i):
        return jnp.where(i == 0, 0, jnp.where(i == nblk - 1, 2, 1))

    return pl.pallas_call(
        functools.partial(_na_kernel, rows=rows),
        grid=(h_, nblk),
        in_specs=[pl.BlockSpec((tq, hd), lambda h, i: (i, h)),
                  pl.BlockSpec((s, hd), lambda h, i: (0, h_ + h)),
                  pl.BlockSpec((s, hd), lambda h, i: (0, 2 * h_ + h)),
                  pl.BlockSpec((c, hd), lambda h, i: (0, h_ + h)),
                  pl.BlockSpec((c, hd), lambda h, i: (0, 2 * h_ + h)),
                  pl.BlockSpec((None, None, tq, NA_SLAB * GRID_W), lambda h, i: (variant(i), h, 0, 0))],
        out_specs=pl.BlockSpec((tq, hd), lambda h, i: (i, h)),
        out_shape=jax.ShapeDtypeStruct((s, NA_WIDTH), BF16),
        compiler_params=_params(("arbitrary", "arbitrary"),
                                4 * s * hd * 2 + 2 * tq * NA_SLAB * GRID_W * 4 * 4 + (4 << 20)),
        name="na_latent",
    )(u, u, u, uc, uc, tables)


def _full_attn_kernel(q_ref, k_ref, v_ref, o_ref):
    s = _dot_nt(q_ref[...], k_ref[...]) * NA_HEAD_DIM ** -0.5
    p = jnp.exp(s - jnp.max(s, axis=-1, keepdims=True))
    den = jnp.sum(p, axis=-1, keepdims=True)
    o_ref[...] = (_dot(p.astype(BF16), v_ref[...]) / den).astype(o_ref.dtype)


def full_attention(uc):
    c = uc.shape[0]
    h_ = NA_HEADS
    hd = NA_HEAD_DIM
    return pl.pallas_call(
        _full_attn_kernel,
        grid=(h_,),
        in_specs=[pl.BlockSpec((c, hd), lambda h: (0, h)),
                  pl.BlockSpec((c, hd), lambda h: (0, h_ + h)),
                  pl.BlockSpec((c, hd), lambda h: (0, 2 * h_ + h))],
        out_specs=pl.BlockSpec((c, hd), lambda h: (0, h)),
        out_shape=jax.ShapeDtypeStruct((c, NA_WIDTH), BF16),
        compiler_params=_params(("arbitrary",), 16 << 20),
        name="na_context",
    )(uc, uc, uc)


def _dft_cs(n):
    idx = np.arange(n)
    ang = 2.0 * np.pi * ((idx[:, None] * idx[None, :]) % n) / n
    return np.cos(ang), np.sin(ang)


FN_K1B = 8


def _fn1_kernel(w_ref, x_ref, tr_ref, ti_ref, z_ref, *, r):
    y = _dot(w_ref[...], x_ref[...])
    yr, yi = y[:r], y[r:]
    tr, ti = tr_ref[...], ti_ref[...]
    z_ref[:r, :] = (yr * tr - yi * ti).astype(z_ref.dtype)
    z_ref[r:, :] = (yr * ti + yi * tr).astype(z_ref.dtype)


def _fn2_kernel(zr_ref, zi_ref, kc_ref, ks_ref, c_ref, s_ref, w_ref, o_ref, *, scale):
    zr, zi = zr_ref[...], zi_ref[...]
    kc, ks = kc_ref[...], ks_ref[...]
    xr = (_dot(kc, zr) + _dot(ks, zi)).astype(BF16)
    xi = (_dot(kc, zi) - _dot(ks, zr)).astype(BF16)
    cw = GRID_W
    for g in range(FN_GROUPS):
        lo = g * FN_GROUP_DIM
        f = _dot(xr[:, lo:lo + FN_GROUP_DIM], c_ref[...]) + _dot(xi[:, lo:lo + FN_GROUP_DIM], s_ref[...])
        y = _dot((f * scale).astype(BF16), w_ref[g].astype(BF16))
        for kk in range(FN_K1B):
            o_ref[:, kk * FN_WIDTH + lo:kk * FN_WIDTH + lo + FN_GROUP_DIM] = (
                y[kk * cw:(kk + 1) * cw].astype(o_ref.dtype))


def fourier_latent(u, w_fn):
    l = u.shape[0]
    cw = GRID_W
    r = l // cw
    c_r, s_r = _dft_cs(r)
    w1 = jnp.asarray(np.concatenate([c_r, -s_r], axis=0), BF16)
    ang = 2.0 * np.pi * (np.arange(cw)[:, None] * np.arange(r)[None, :]) / l
    tr = jnp.asarray(np.cos(ang)[:, :, None], F32)
    ti = jnp.asarray(-np.sin(ang)[:, :, None], F32)
    z = pl.pallas_call(
        functools.partial(_fn1_kernel, r=r),
        grid=(cw,),
        in_specs=[pl.BlockSpec((2 * r, r), lambda j: (0, 0)),
                  pl.BlockSpec((r, FN_WIDTH), lambda j: (0, j)),
                  pl.BlockSpec((None, r, 1), lambda j: (j, 0, 0)),
                  pl.BlockSpec((None, r, 1), lambda j: (j, 0, 0))],
        out_specs=pl.BlockSpec((2 * r, FN_WIDTH), lambda j: (0, j)),
        out_shape=jax.ShapeDtypeStruct((2 * r, cw * FN_WIDTH), BF16),
        compiler_params=_params(("arbitrary",), 24 << 20),
        name="fourier_stage1",
    )(w1, u.reshape(r, cw * FN_WIDTH), tr, ti)
    z2 = z.reshape(2 * r * cw, FN_WIDTH)
    c_w, s_w = _dft_cs(cw)
    eye = np.eye(FN_K1B)
    kc = jnp.asarray(np.kron(eye, c_w), BF16)
    ks = jnp.asarray(np.kron(eye, s_w), BF16)
    c_c, s_c = _dft_cs(FN_GROUP_DIM)
    nb = r // FN_K1B
    tb = FN_K1B * cw
    out = pl.pallas_call(
        functools.partial(_fn2_kernel, scale=float((l * FN_GROUP_DIM) ** -0.5)),
        grid=(nb,),
        in_specs=[pl.BlockSpec((tb, FN_WIDTH), lambda b: (b, 0)),
                  pl.BlockSpec((tb, FN_WIDTH), lambda b: (nb + b, 0)),
                  pl.BlockSpec((tb, tb), lambda b: (0, 0)),
                  pl.BlockSpec((tb, tb), lambda b: (0, 0)),
                  pl.BlockSpec((FN_GROUP_DIM, FN_GROUP_DIM), lambda b: (0, 0)),
                  pl.BlockSpec((FN_GROUP_DIM, FN_GROUP_DIM), lambda b: (0, 0)),
                  pl.BlockSpec((FN_GROUPS, FN_GROUP_DIM, FN_GROUP_DIM), lambda b: (0, 0, 0))],
        out_specs=pl.BlockSpec((cw, FN_K1B * FN_WIDTH), lambda b: (0, b)),
        out_shape=jax.ShapeDtypeStruct((cw, r * FN_WIDTH), BF16),
        compiler_params=_params(("arbitrary",), 24 << 20),
        name="fourier_stage2",
    )(z2, z2, kc, ks, jnp.asarray(c_c, BF16), jnp.asarray(s_c, BF16), w_fn)
    return out.reshape(l, FN_WIDTH)


def _fn_ctx_kernel(u_ref, cl_ref, sl_ref, c_ref, s_ref, w_ref, o_ref, *, scale):
    u = u_ref[...]
    gr = _dot(cl_ref[...], u).astype(BF16)
    gi = (-_dot(sl_ref[...], u)).astype(BF16)
    for g in range(FN_GROUPS):
        lo = g * FN_GROUP_DIM
        f = _dot(gr[:, lo:lo + FN_GROUP_DIM], c_ref[...]) + _dot(gi[:, lo:lo + FN_GROUP_DIM], s_ref[...])
        o_ref[:, lo:lo + FN_GROUP_DIM] = _dot((f * scale).astype(BF16), w_ref[g].astype(BF16)).astype(o_ref.dtype)


def fourier_context(u, w_fn):
    c = u.shape[0]
    c_l, s_l = _dft_cs(c)
    c_c, s_c = _dft_cs(FN_GROUP_DIM)
    return pl.pallas_call(
        functools.partial(_fn_ctx_kernel, scale=float((c * FN_GROUP_DIM) ** -0.5)),
        out_shape=jax.ShapeDtypeStruct((c, FN_WIDTH), BF16),
        compiler_params=_params((), 16 << 20),
        name="fourier_context",
    )(u, jnp.asarray(c_l, BF16), jnp.asarray(s_l, BF16), jnp.asarray(c_c, BF16), jnp.asarray(s_c, BF16), w_fn)


def _split_hi_lo(x):
    hi = x.astype(BF16)
    return hi, (x - hi.astype(F32)).astype(BF16)


def _gla_kernel(q_ref, k_ref, v_ref, lr_ref, cos_ref, sin_ref, w2_ref, bg_ref, perm_ref, s0_ref,
                o_ref, sfin_ref, st_ref, *, rev, nchunks):
    step = pl.program_id(0)

    @pl.when(step == 0)
    def _():
        st_ref[...] = s0_ref[...]

    cs = GLA_CHUNK
    sub = GLA_SUB
    nsub = cs // sub
    z = _dot(lr_ref[...].astype(BF16), w2_ref[...].astype(BF16)) + bg_ref[...]
    la = -(jnp.maximum(-z, 0.0) + jnp.log(1.0 + jnp.exp(-jnp.abs(z)))) * (1.0 / GLA_GATE_TEMP)
    ri = lax.broadcasted_iota(jnp.int32, (cs, cs), 0)
    ci = lax.broadcasted_iota(jnp.int32, (cs, cs), 1)
    tri = jnp.where((ci >= ri) if rev else (ci <= ri), 1.0, 0.0).astype(BF16)
    la_hi, la_lo = _split_hi_lo(la)
    bcum = _dot(tri, la_hi) + _dot(tri, la_lo)
    edge = 0 if rev else cs - 1
    row_id = lax.broadcasted_iota(jnp.int32, (cs, GLA_DKP), 0)
    sub_r = lax.broadcasted_iota(jnp.int32, (sub, GLA_DKP), 0)
    lane_c = lax.broadcasted_iota(jnp.int32, (sub, cs), 1)
    cos, sin = cos_ref[...], sin_ref[...]
    perm = perm_ref[...]
    qscale = GLA_DK ** -0.5
    for h in range(GLA_HEADS):
        ks_ = slice(h * GLA_DKP, (h + 1) * GLA_DKP)
        vs_ = slice(h * GLA_DV, (h + 1) * GLA_DV)
        qb, kb = q_ref[:, ks_], k_ref[:, ks_]
        q = (qb.astype(F32) * cos + _dot(qb, perm) * sin) * qscale
        k = kb.astype(F32) * cos + _dot(kb, perm) * sin
        v = v_ref[:, vs_]
        b = bcum[:, ks_]
        b_edge = b[edge:edge + 1, :]
        st = st_ref[h]
        o = _dot_nt((q * jnp.exp(b)).astype(BF16), st.astype(BF16))
        slabs = []
        for blk in range(nsub):
            lo = blk * sub
            q_i = q[lo:lo + sub]
            b_i = b[lo:lo + sub]
            k_i = k[lo:lo + sub]
            acc = jnp.zeros((sub, cs), F32)
            if rev and blk < nsub - 1:
                ref_row = b[lo + sub:lo + sub + 1, :]
                outside = row_id >= lo + sub
            elif (not rev) and blk > 0:
                ref_row = b[lo - 1:lo, :]
                outside = row_id < lo
            else:
                ref_row = None
            if ref_row is not None:
                qe = q_i * jnp.exp(b_i - ref_row)
                ke = jnp.where(outside, k * jnp.exp(jnp.where(outside, ref_row - b, 0.0)), 0.0)
                acc = acc + _dot_nt(qe.astype(BF16), ke.astype(BF16))
            for j in range(sub):
                keep = (sub_r <= j) if rev else (sub_r >= j)
                d = jnp.exp(jnp.where(keep, b_i - b_i[j:j + 1, :], NEG))
                col = jnp.sum(q_i * d * k_i[j:j + 1, :], axis=-1, keepdims=True)
                acc = acc + jnp.where(lane_c == lo + j, col, 0.0)
            slabs.append(acc)
        attn = jnp.concatenate(slabs, axis=0)
        o = o + _dot(attn.astype(BF16), v)
        o_ref[:, vs_] = o
        kend = (k * jnp.exp(b_edge - b)).astype(BF16)
        st_ref[h] = st * jnp.exp(b_edge) + _dot_tn(v, kend)

    @pl.when(step == nchunks - 1)
    def _():
        sfin_ref[...] = st_ref[...]


def gla_scan(u_qk, u_vr, lr, cos, sin, w2p, bgp, perm, s0, rev):
    l = u_qk.shape[0]
    n = l // GLA_CHUNK
    cs = GLA_CHUNK
    kw = GLA_HEADS * GLA_DKP
    ch = (lambda s: n - 1 - s) if rev else (lambda s: s)
    full2 = lambda s: (0, 0)
    o, sfin = pl.pallas_call(
        functools.partial(_gla_kernel, rev=rev, nchunks=n),
        grid=(n,),
        in_specs=[pl.BlockSpec((cs, kw), lambda s: (ch(s), 0)),
                  pl.BlockSpec((cs, kw), lambda s: (ch(s), 1)),
                  pl.BlockSpec((cs, GLA_VW), lambda s: (ch(s), 0)),
                  pl.BlockSpec((cs, LANE), lambda s: (ch(s), 0)),
                  pl.BlockSpec((cs, GLA_DKP), lambda s: (ch(s), 0)),
                  pl.BlockSpec((cs, GLA_DKP), lambda s: (ch(s), 0)),
                  pl.BlockSpec((LANE, kw), full2),
                  pl.BlockSpec((1, kw), full2),
                  pl.BlockSpec((GLA_DKP, GLA_DKP), full2),
                  pl.BlockSpec((GLA_HEADS, GLA_DV, GLA_DKP), lambda s: (0, 0, 0))],
        out_specs=[pl.BlockSpec((cs, GLA_VW), lambda s: (ch(s), 0)),
                   pl.BlockSpec((GLA_HEADS, GLA_DV, GLA_DKP), lambda s: (0, 0, 0))],
        out_shape=[jax.ShapeDtypeStruct((l, GLA_VW), F32),
                   jax.ShapeDtypeStruct((GLA_HEADS, GLA_DV, GLA_DKP), F32)],
        scratch_shapes=[pltpu.VMEM((GLA_HEADS, GLA_DV, GLA_DKP), F32)],
        compiler_params=_params(("arbitrary",), 32 << 20),
        name="gla_bwd" if rev else "gla_fwd",
    )(u_qk, u_qk, u_vr, lr, cos, sin, w2p, bgp, perm, s0)
    return o, sfin


def _gla_out_kernel(of_ref, ob_ref, r_ref, g_ref, o_ref):
    o = of_ref[...] + ob_ref[...]
    g = g_ref[...]
    r = r_ref[...].astype(F32)
    for h in range(GLA_HEADS):
        sl = slice(h * GLA_DV, (h + 1) * GLA_DV)
        oh = o[:, sl]
        on = oh * lax.rsqrt(jnp.mean(oh * oh, axis=-1, keepdims=True) + EPS) * g[:, sl]
        o_ref[:, sl] = (on * _silu(r[:, sl])).astype(o_ref.dtype)


def gla_output(o_f, o_b, u_vr, g_gla, tm=256):
    l = o_f.shape[0]
    tm = min(tm, l)
    return pl.pallas_call(
        _gla_out_kernel,
        grid=(l // tm,),
        in_specs=[pl.BlockSpec((tm, GLA_VW), lambda i: (i, 0)),
                  pl.BlockSpec((tm, GLA_VW), lambda i: (i, 0)),
                  pl.BlockSpec((tm, GLA_VW), lambda i: (i, 1)),
                  pl.BlockSpec((1, GLA_VW), lambda i: (0, 0))],
        out_specs=pl.BlockSpec((tm, GLA_VW), lambda i: (i, 0)),
        out_shape=jax.ShapeDtypeStruct((l, GLA_VW), BF16),
        compiler_params=_params(("arbitrary",), 24 << 20),
        name="gla_output",
    )(o_f, o_b, u_vr, g_gla.reshape(1, GLA_VW))


def rope_tables(n_tokens):
    seg = GLA_DK // 2
    half = seg // 2
    inv = ROPE_THETA ** (-jnp.arange(half, dtype=F32) / half)
    pos = jnp.arange(n_tokens)
    ang_r = (pos // GRID_W).astype(F32)[:, None] * inv
    ang_c = (pos % GRID_W).astype(F32)[:, None] * inv
    pad1 = jnp.ones((n_tokens, GLA_DKP - GLA_DK), F32)
    pad0 = jnp.zeros((n_tokens, GLA_DKP - GLA_DK), F32)
    cos = jnp.concatenate([jnp.cos(ang_r)] * 2 + [jnp.cos(ang_c)] * 2 + [pad1], axis=1)
    sin = jnp.concatenate([jnp.sin(ang_r)] * 2 + [jnp.sin(ang_c)] * 2 + [pad0], axis=1)
    return cos, sin


def rope_perm():
    seg = GLA_DK // 2
    half = seg // 2
    p = np.zeros((GLA_DKP, GLA_DKP), np.float32)
    for s0 in (0, seg):
        for j in range(half):
            p[s0 + half + j, s0 + j] = -1.0
            p[s0 + j, s0 + half + j] = 1.0
    return jnp.asarray(p, BF16)


def _pad_heads(w):
    lead = w.shape[:-1]
    w = w.reshape(lead + (GLA_HEADS, GLA_DK))
    w = jnp.pad(w, [(0, 0)] * len(lead) + [(0, 0), (0, GLA_DKP - GLA_DK)])
    return w.reshape(lead + (GLA_HEADS * GLA_DKP,))


def _moe_up_kernel(a_ref, w_ref, comb_ref, h_ref):
    p = pl.program_id(1)
    gu = _dot(a_ref[...], w_ref[...])
    f2 = 2 * MOE_HIDDEN
    gate, up = gu[:, :f2], gu[:, f2:]
    comb = comb_ref[...]
    lane = lax.broadcasted_iota(jnp.int32, comb.shape, 1)
    c0 = _lane_pick(comb, lane, 2 * p)
    c1 = _lane_pick(comb, lane, 2 * p + 1)
    lane2 = lax.broadcasted_iota(jnp.int32, gate.shape, 1)
    h_ref[...] = (_silu(gate) * up * jnp.where(lane2 < MOE_HIDDEN, c0, c1)).astype(h_ref.dtype)


def moe_up(xn, w_gu2, comb, tm=1024):
    t, d = xn.shape
    tm = min(tm, t)
    npair = w_gu2.shape[0]
    f2 = 2 * MOE_HIDDEN
    return pl.pallas_call(
        _moe_up_kernel,
        grid=(t // tm, npair),
        in_specs=[pl.BlockSpec((tm, d), lambda i, p: (i, 0)),
                  pl.BlockSpec((None, d, 2 * f2), lambda i, p: (p, 0, 0)),
                  pl.BlockSpec((tm, LANE), lambda i, p: (i, 0))],
        out_specs=pl.BlockSpec((tm, f2), lambda i, p: (i, p)),
        out_shape=jax.ShapeDtypeStruct((t, npair * f2), BF16),
        compiler_params=_params(("arbitrary", "arbitrary"),
                                2 * tm * d * 2 + 2 * d * 2 * f2 * 2 + 8 * tm * 2 * f2 * 4),
        name="moe_up",
    )(xn, w_gu2, comb)


def _layer(l, x, xc, mod, last, consts, w):
    (g_mix, w_in, rpb, w_fn, w_g2, b_g, g_gla, w_out, g_ffn, w_rg, b_rg, w_re, b_re, w_gu, w_dn) = w
    cos, sin, cos_c, sin_c, perm = consts
    d = D_MODEL
    s_len = x.shape[0]
    m_x = mod[l, 0].reshape(N_MOD, d)
    m_c = mod[l, 1].reshape(N_MOD, d)

    w_lr = jnp.pad(w_in[l, :, OFF_LR:OFF_LR + 2 * GLA_GATE_RANK], ((0, 0), (0, LANE - 2 * GLA_GATE_RANK)))
    xn, lr = norm_proj(x, g_mix[l], m_x[0], m_x[1], w_lr)
    xcn, lr_c = norm_proj(xc, g_mix[l], m_c[0], m_c[1], w_lr)

    w_qk = jnp.concatenate([_pad_heads(w_in[l, :, OFF_GQK:OFF_GQK + GLA_KW]),
                            _pad_heads(w_in[l, :, OFF_GQK + GLA_KW:OFF_GVR])], axis=1)
    proj = []
    for a in (xn, xcn):
        u_na = matmul(a, w_in, (l,), OFF_NA, 3 * NA_WIDTH, 512, 1024, BF16, name="in_na")
        u_fn = matmul(a, w_in, (l,), OFF_FN, FN_WIDTH, 512, 1024, BF16, name="in_fn")
        u_qk = matmul(a, w_qk, (), 0, 2 * GLA_HEADS * GLA_DKP, 512, 1024, BF16, name="in_gqk")
        u_vr = matmul(a, w_in, (l,), OFF_GVR, 2 * GLA_VW, 512, 1024, BF16, name="in_gvr")
        proj.append((u_na, u_fn, u_qk, u_vr))
    (u_na, u_fn, u_qk, u_vr), (c_na, c_fn, c_qk, c_vr) = proj

    w2p = [jnp.pad(_pad_heads(w_g2[l, dr]), ((dr * GLA_GATE_RANK, LANE - (dr + 1) * GLA_GATE_RANK), (0, 0)))
           for dr in (0, 1)]
    bgp = [_pad_heads(b_g[l, dr]).reshape(1, -1) for dr in (0, 1)]
    zero = jnp.zeros((GLA_HEADS, GLA_DV, GLA_DKP), F32)
    outs_c, outs_x = [], []
    for dr in (0, 1):
        o_c, s_c = gla_scan(c_qk, c_vr, lr_c, cos_c, sin_c, w2p[dr], bgp[dr], perm, zero, rev=bool(dr))
        o_x, _ = gla_scan(u_qk, u_vr, lr, cos, sin, w2p[dr], bgp[dr], perm, s_c, rev=bool(dr))
        outs_c.append(o_c)
        outs_x.append(o_x)

    tables = na_tables(rpb[l], s_len // GRID_W)
    mix = jnp.concatenate([neighbourhood_attention(u_na, c_na, tables),
                           fourier_latent(u_fn, w_fn[l]),
                           gla_output(outs_x[0], outs_x[1], u_vr, g_gla[l])], axis=1)
    x = matmul(mix, w_out, (l,), 0, d, 512, 1024, F32, res=x, gate=m_x[2], name="out_proj")

    w_route = jnp.pad(jnp.concatenate([w_re[l], w_rg[l]], axis=1),
                      ((0, 0), (0, LANE - MOE_EXPERTS - MOE_GROUPS)))
    b_route = jnp.pad(jnp.concatenate([b_re[l], b_rg[l]]), (0, LANE - MOE_EXPERTS - MOE_GROUPS)).reshape(1, LANE)
    f = MOE_HIDDEN
    w_gu2 = (w_gu[l].reshape(MOE_EXPERTS // 2, 2, d, 2, f).transpose(0, 2, 3, 1, 4)
             .reshape(MOE_EXPERTS // 2, d, 4 * f).astype(BF16))
    w_dn2 = w_dn.reshape(DEPTH, MOE_EXPERTS * f, d)

    def ffn(y, m):
        yn, comb = norm_proj(y, g_ffn[l], m[3], m[4], w_route, b_route)
        hid = moe_up(yn, w_gu2, comb)
        return matmul(hid, w_dn2, (l,), 0, d, 256, 1024, F32, res=y, gate=m[5], name="moe_down")

    x = ffn(x, m_x)
    if not last:
        mix_c = jnp.concatenate([full_attention(c_na),
                                 fourier_context(c_fn, w_fn[l]),
                                 gla_output(outs_c[0], outs_c[1], c_vr, g_gla[l])], axis=1)
        xc = matmul(mix_c, w_out, (l,), 0, d, 512, 1024, F32, res=xc, gate=m_c[2], name="out_proj_ctx")
        xc = ffn(xc, m_c)
    return x, xc


def kernel(x, c, ctx, c_ctx, w_ada, b_ada, g_mix, w_in, rpb, w_fn, w_g2, b_g, g_gla, w_out,
           g_ffn, w_rg, b_rg, w_re, b_re, w_gu, w_dn, g_final):
    assert x.shape[0] == 1 and c.shape[0] == 1
    d = D_MODEL
    s_len = x.shape[1]
    c_len = ctx.shape[1]
    cc = jnp.concatenate([c, c_ctx[None], jnp.zeros((6, d), F32)], axis=0)
    mod = ada_mod(cc, w_ada, b_ada)
    cos, sin = rope_tables(s_len)
    cos_c = jnp.ones((c_len, GLA_DKP), F32)
    sin_c = jnp.zeros((c_len, GLA_DKP), F32)
    consts = (cos, sin, cos_c, sin_c, rope_perm())
    w = (g_mix, w_in, rpb, w_fn, w_g2, b_g, g_gla, w_out, g_ffn, w_rg, b_rg, w_re, b_re, w_gu, w_dn)
    xs, xc = x[0], ctx[0]
    for l in range(DEPTH):
        xs, xc = _layer(l, xs, xc, mod, l == DEPTH - 1, consts, w)
    return final_norm(xs, g_final)[None]
```

```python
import functools
import math

import numpy as np
import jax
import jax.numpy as jnp
from jax import lax
from jax.experimental import pallas as pl
from jax.experimental.pallas import tpu as pltpu

F32 = jnp.float32
BF16 = jnp.bfloat16

D_MODEL = 4096
DEPTH = 4
GRID_W = 64
EPS = 1e-6
N_MOD = 6

NA_HEADS = 12
NA_HEAD_DIM = 128
NA_WIN_H = 8
NA_WIN_W = 16
NA_WIDTH = NA_HEADS * NA_HEAD_DIM

FN_GROUPS = 8
FN_GROUP_DIM = 128
FN_WIDTH = FN_GROUPS * FN_GROUP_DIM

GLA_HEADS = 4
GLA_DK = 192
GLA_DKP = 256
GLA_DV = 384
GLA_KW = GLA_HEADS * GLA_DK
GLA_VW = GLA_HEADS * GLA_DV
GLA_GATE_RANK = 16
GLA_GATE_TEMP = 16.0
GLA_CHUNK = 64
GLA_SUB = 16
ROPE_THETA = 10000.0

MOE_GROUPS = 4
MOE_EXPERTS_PER_GROUP = 8
MOE_EXPERTS = MOE_GROUPS * MOE_EXPERTS_PER_GROUP
MOE_HIDDEN = 192

OFF_NA = 0
OFF_FN = 3 * NA_WIDTH
OFF_GQK = OFF_FN + FN_WIDTH
OFF_GVR = OFF_GQK + 2 * GLA_KW
OFF_LR = OFF_GVR + 2 * GLA_VW

LANE = 128
NEG = -1e30
V7X_VMEM_BUDGET = 56 * 1024 * 1024


def _params(sem, vmem_bytes):
    return pltpu.CompilerParams(dimension_semantics=sem,
                                vmem_limit_bytes=int(min(max(vmem_bytes, 16 << 20), V7X_VMEM_BUDGET)))


def _dot(a, b):
    return jnp.dot(a, b, preferred_element_type=F32)


def _dot_nt(a, b):
    return lax.dot_general(a, b, (((1,), (1,)), ((), ())), preferred_element_type=F32)


def _dot_tn(a, b):
    return lax.dot_general(a, b, (((0,), (0,)), ((), ())), preferred_element_type=F32)


def _silu(x):
    return x * (1.0 / (1.0 + jnp.exp(-x)))


def _ada_kernel(c_ref, w_ref, b_ref, o_ref):
    s = _silu(c_ref[...]).astype(BF16)
    o_ref[0] = _dot(s, w_ref[0].astype(BF16)) + b_ref[0]


def ada_mod(cc, w_ada, b_ada, tn=512):
    n, d, nd = w_ada.shape
    return pl.pallas_call(
        _ada_kernel,
        grid=(n, nd // tn),
        in_specs=[pl.BlockSpec((8, d), lambda l, j: (0, 0)),
                  pl.BlockSpec((1, d, tn), lambda l, j: (l, 0, j)),
                  pl.BlockSpec((1, 1, tn), lambda l, j: (l, 0, j))],
        out_specs=pl.BlockSpec((1, 8, tn), lambda l, j: (l, 0, j)),
        out_shape=jax.ShapeDtypeStruct((n, 8, nd), F32),
        compiler_params=_params(("arbitrary", "arbitrary"), 3 * d * tn * 4 + (4 << 20)),
        name="ada_mod",
    )(cc, w_ada, b_ada.reshape(n, 1, nd))


def _normed(x_ref, g_ref, sh_ref, sc_ref):
    x = x_ref[...]
    y = x * lax.rsqrt(jnp.mean(x * x, axis=-1, keepdims=True) + EPS)
    return (y * g_ref[...]) * (1.0 + sc_ref[...]) + sh_ref[...]


def _norm_proj_kernel(x_ref, g_ref, sh_ref, sc_ref, w_ref, xn_ref, p_ref):
    hb = _normed(x_ref, g_ref, sh_ref, sc_ref).astype(BF16)
    xn_ref[...] = hb
    p_ref[...] = _dot(hb, w_ref[...].astype(BF16))


def _lane_pick(v, lane, idx):
    return jnp.sum(jnp.where(lane == idx, v, 0.0), axis=-1, keepdims=True)


def _norm_route_kernel(x_ref, g_ref, sh_ref, sc_ref, w_ref, b_ref, xn_ref, comb_ref):
    hb = _normed(x_ref, g_ref, sh_ref, sc_ref).astype(BF16)
    xn_ref[...] = hb
    logits = _dot(hb, w_ref[...].astype(BF16)) + b_ref[...]
    lane = lax.broadcasted_iota(jnp.int32, logits.shape, 1).astype(F32)
    far = float(4 * LANE)
    is_g = (lane >= MOE_EXPERTS) & (lane < MOE_EXPERTS + MOE_GROUPS)
    gl = jnp.where(is_g, logits, NEG)
    gmax = jnp.max(gl, axis=-1, keepdims=True)
    g_top = jnp.min(jnp.where(gl == gmax, lane, far), axis=-1, keepdims=True) - MOE_EXPERTS
    g_w = 1.0 / jnp.sum(jnp.where(is_g, jnp.exp(gl - gmax), 0.0), axis=-1, keepdims=True)
    in_grp = (lane >= g_top * MOE_EXPERTS_PER_GROUP) & (lane < (g_top + 1) * MOE_EXPERTS_PER_GROUP)
    el = jnp.where(in_grp, logits, NEG)
    v1 = jnp.max(el, axis=-1, keepdims=True)
    i1 = jnp.min(jnp.where(el == v1, lane, far), axis=-1, keepdims=True)
    el2 = jnp.where(lane == i1, NEG, el)
    v2 = jnp.max(el2, axis=-1, keepdims=True)
    i2 = jnp.min(jnp.where(el2 == v2, lane, far), axis=-1, keepdims=True)
    e2 = jnp.exp(v2 - v1)
    w1 = g_w / (1.0 + e2)
    w2 = g_w * e2 / (1.0 + e2)
    comb_ref[...] = jnp.where(lane == i1, w1, 0.0) + jnp.where(lane == i2, w2, 0.0)


def norm_proj(x, g, sh, sc, w_small, b_small=None, tm=256):
    t, d = x.shape
    row = lambda i: (0, 0)
    vec = pl.BlockSpec((1, d), row)
    ins = [pl.BlockSpec((tm, d), lambda i: (i, 0)), vec, vec, vec, pl.BlockSpec((d, LANE), row)]
    args = [x, g.reshape(1, d), sh.reshape(1, d), sc.reshape(1, d), w_small]
    if b_small is None:
        body = _norm_proj_kernel
    else:
        body = _norm_route_kernel
        ins.append(pl.BlockSpec((1, LANE), row))
        args.append(b_small)
    return pl.pallas_call(
        body,
        grid=(t // tm,),
        in_specs=ins,
        out_specs=[pl.BlockSpec((tm, d), lambda i: (i, 0)), pl.BlockSpec((tm, LANE), lambda i: (i, 0))],
        out_shape=[jax.ShapeDtypeStruct((t, d), BF16), jax.ShapeDtypeStruct((t, LANE), F32)],
        compiler_params=_params(("arbitrary",), 6 * tm * d * 4 + 4 * d * LANE * 4),
        name="norm_proj" if b_small is None else "norm_route",
    )(*args)


def _final_norm_kernel(x_ref, g_ref, o_ref):
    x = x_ref[...]
    o_ref[...] = x * lax.rsqrt(jnp.mean(x * x, axis=-1, keepdims=True) + EPS) * g_ref[...]


def final_norm(x, g, tm=256):
    t, d = x.shape
    return pl.pallas_call(
        _final_norm_kernel,
        grid=(t // tm,),
        in_specs=[pl.BlockSpec((tm, d), lambda i: (i, 0)), pl.BlockSpec((1, d), lambda i: (0, 0))],
        out_specs=pl.BlockSpec((tm, d), lambda i: (i, 0)),
        out_shape=jax.ShapeDtypeStruct((t, d), F32),
        compiler_params=_params(("arbitrary",), 6 * tm * d * 4),
        name="final_norm",
    )(x, g.reshape(1, d))


def _mm_kernel(a_ref, w_ref, o_ref):
    o_ref[...] = _dot(a_ref[...], w_ref[...].astype(BF16)).astype(o_ref.dtype)


def _mm_res_kernel(a_ref, w_ref, r_ref, g_ref, o_ref):
    o_ref[...] = r_ref[...] + g_ref[...] * _dot(a_ref[...], w_ref[...].astype(BF16))


def matmul(a, w, w_lead, col0, ncols, tn, tm, out_dtype, res=None, gate=None, name="mm"):
    t, k = a.shape
    tm = min(tm, t)
    nlead = len(w_lead)
    j0 = col0 // tn
    w_spec = pl.BlockSpec((None,) * nlead + (k, tn), lambda i, j: tuple(w_lead) + (0, j + j0))
    ins = [pl.BlockSpec((tm, k), lambda i, j: (i, 0)), w_spec]
    args = [a, w]
    wbytes = jnp.dtype(w.dtype).itemsize
    vmem = 2 * tm * k * 2 + 3 * k * tn * wbytes + 6 * tm * tn * 4
    if res is None:
        body = _mm_kernel
    else:
        body = _mm_res_kernel
        ins += [pl.BlockSpec((tm, tn), lambda i, j: (i, j)), pl.BlockSpec((1, tn), lambda i, j: (0, j))]
        args += [res, gate.reshape(1, ncols)]
    return pl.pallas_call(
        body,
        grid=(t // tm, ncols // tn),
        in_specs=ins,
        out_specs=pl.BlockSpec((tm, tn), lambda i, j: (i, j)),
        out_shape=jax.ShapeDtypeStruct((t, ncols), out_dtype),
        compiler_params=_params(("arbitrary", "arbitrary"), vmem),
        name=name,
    )(*args)


NA_QROWS = 4
NA_SLAB = 12


def na_tables(rpb, rows):
    nblk = rows // NA_QROWS
    n_, h_ = rpb.shape[:2]
    qc = np.arange(GRID_W)[:, None]
    kc = np.arange(GRID_W)[None, :]
    c0 = np.clip(qc - NA_WIN_W // 2, 0, GRID_W - NA_WIN_W)
    ok_c = (kc >= c0) & (kc < c0 + NA_WIN_W)
    a = np.arange(NA_QROWS)[:, None]
    b = np.arange(NA_SLAB)[None, :]
    dr_l, ok_l = [], []
    for i in (0, 1, nblk - 1):
        base = int(np.clip(i * NA_QROWS - NA_WIN_H // 2, 0, rows - NA_SLAB))
        r = i * NA_QROWS + a
        r0 = np.clip(r - NA_WIN_H // 2, 0, rows - NA_WIN_H)
        krow = base + b
        ok_l.append((krow >= r0) & (krow < r0 + NA_WIN_H))
        dr_l.append(np.clip(krow - r + NA_WIN_H - 1, 0, 2 * NA_WIN_H - 2))
    dr = np.stack(dr_l).reshape(-1)
    ok = np.stack(ok_l)[:, :, None, :, None] & ok_c[None, None, :, None, :]
    lo = GRID_W - NA_WIN_W
    padded = jnp.pad(rpb.astype(F32), ((0, 0), (0, 0), (0, 0), (lo, lo)))
    by_col = jnp.stack([padded[..., GRID_W - 1 - q:2 * GRID_W - 1 - q] for q in range(GRID_W)], axis=3)
    by_row = jnp.stack([by_col[:, :, int(d_)] for d_ in dr], axis=2)
    bias = by_row.reshape(n_, h_, 3, NA_QROWS, NA_SLAB, GRID_W, GRID_W).transpose(0, 2, 1, 3, 5, 4, 6)
    bias = jnp.where(jnp.asarray(ok)[None, :, None], bias, NEG)
    return bias.reshape(n_, 3, h_, NA_QROWS * GRID_W, NA_SLAB * GRID_W)


def _na_kernel(q_ref, k_ref, v_ref, kc_ref, vc_ref, t_ref, o_ref, *, rows):
    i = pl.program_id(1)
    base = jnp.clip(i * NA_QROWS - NA_WIN_H // 2, 0, rows - NA_SLAB) * GRID_W
    base = pl.multiple_of(base, NA_QROWS * GRID_W)
    n_keys = NA_SLAB * GRID_W
    scale = NA_HEAD_DIM ** -0.5
    q = q_ref[...]
    k = k_ref[pl.ds(base, n_keys), :]
    v = v_ref[pl.ds(base, n_keys), :]
    s = _dot_nt(q, k) * scale + t_ref[...]
    sc = _dot_nt(q, kc_ref[...]) * scale
    m = jnp.maximum(jnp.max(s, axis=-1, keepdims=True), jnp.max(sc, axis=-1, keepdims=True))
    p = jnp.exp(s - m)
    pc = jnp.exp(sc - m)
    den = jnp.sum(p, axis=-1, keepdims=True) + jnp.sum(pc, axis=-1, keepdims=True)
    o = _dot(p.astype(BF16), v) + _dot(pc.astype(BF16), vc_ref[...])
    o_ref[...] = (o / den).astype(o_ref.dtype)


def neighbourhood_attention(u, uc, tables):
    s = u.shape[0]
    c = uc.shape[0]
    rows = s // GRID_W
    nblk = rows // NA_QROWS
    tq = NA_QROWS * GRID_W
    h_ = NA_HEADS
    hd = NA_HEAD_DIM

    def variant(i):
        return jnp.where(i == 0, 0, jnp.where(i == nblk - 1, 2, 1))

    return pl.pallas_call(
        functools.partial(_na_kernel, rows=rows),
        grid=(h_, nblk),
        in_specs=[pl.BlockSpec((tq, hd), lambda h, i: (i, h)),
                  pl.BlockSpec((s, hd), lambda h, i: (0, h_ + h)),
                  pl.BlockSpec((s, hd), lambda h, i: (0, 2 * h_ + h)),
                  pl.BlockSpec((c, hd), lambda h, i: (0, h_ + h)),
                  pl.BlockSpec((c, hd), lambda h, i: (0, 2 * h_ + h)),
                  pl.BlockSpec((None, None, tq, NA_SLAB * GRID_W), lambda h, i: (variant(i), h, 0, 0))],
        out_specs=pl.BlockSpec((tq, hd), lambda h, i: (i, h)),
        out_shape=jax.ShapeDtypeStruct((s, NA_WIDTH), BF16),
        compiler_params=_params(("arbitrary", "arbitrary"),
                                4 * s * hd * 2 + 2 * tq * NA_SLAB * GRID_W * 4 * 4 + (4 << 20)),
        name="na_latent",
    )(u, u, u, uc, uc, tables)


def _full_attn_kernel(q_ref, k_ref, v_ref, o_ref):
    s = _dot_nt(q_ref[...], k_ref[...]) * NA_HEAD_DIM ** -0.5
    p = jnp.exp(s - jnp.max(s, axis=-1, keepdims=True))
    den = jnp.sum(p, axis=-1, keepdims=True)
    o_ref[...] = (_dot(p.astype(BF16), v_ref[...]) / den).astype(o_ref.dtype)


def full_attention(uc):
    c = uc.shape[0]
    h_ = NA_HEADS
    hd = NA_HEAD_DIM
    return pl.pallas_call(
        _full_attn_kernel,
        grid=(h_,),
        in_specs=[pl.BlockSpec((c, hd), lambda h: (0, h)),
                  pl.BlockSpec((c, hd), lambda h: (0, h_ + h)),
                  pl.BlockSpec((c, hd), lambda h: (0, 2 * h_ + h))],
        out_specs=pl.BlockSpec((c, hd), lambda h: (0, h)),
        out_shape=jax.ShapeDtypeStruct((c, NA_WIDTH), BF16),
        compiler_params=_params(("arbitrary",), 16 << 20),
        name="na_context",
    )(uc, uc, uc)


def _dft_cs(n):
    idx = np.arange(n)
    ang = 2.0 * np.pi * ((idx[:, None] * idx[None, :]) % n) / n
    return np.cos(ang), np.sin(ang)


FN_K1B = 8


def _fn1_kernel(w_ref, x_ref, tr_ref, ti_ref, z_ref, *, r):
    y = _dot(w_ref[...], x_ref[...])
    yr, yi = y[:r], y[r:]
    tr, ti = tr_ref[...], ti_ref[...]
    z_ref[:r, :] = (yr * tr - yi * ti).astype(z_ref.dtype)
    z_ref[r:, :] = (yr * ti + yi * tr).astype(z_ref.dtype)


def _fn2_kernel(zr_ref, zi_ref, kc_ref, ks_ref, c_ref, s_ref, w_ref, o_ref, *, scale):
    zr, zi = zr_ref[...], zi_ref[...]
    kc, ks = kc_ref[...], ks_ref[...]
    xr = (_dot(kc, zr) + _dot(ks, zi)).astype(BF16)
    xi = (_dot(kc, zi) - _dot(ks, zr)).astype(BF16)
    cw = GRID_W
    for g in range(FN_GROUPS):
        lo = g * FN_GROUP_DIM
        f = _dot(xr[:, lo:lo + FN_GROUP_DIM], c_ref[...]) + _dot(xi[:, lo:lo + FN_GROUP_DIM], s_ref[...])
        y = _dot((f * scale).astype(BF16), w_ref[g].astype(BF16))
        for kk in range(FN_K1B):
            o_ref[:, kk * FN_WIDTH + lo:kk * FN_WIDTH + lo + FN_GROUP_DIM] = (
                y[kk * cw:(kk + 1) * cw].astype(o_ref.dtype))


def fourier_latent(u, w_fn):
    l = u.shape[0]
    cw = GRID_W
    r = l // cw
    c_r, s_r = _dft_cs(r)
    w1 = jnp.asarray(np.concatenate([c_r, -s_r], axis=0), BF16)
    ang = 2.0 * np.pi * (np.arange(cw)[:, None] * np.arange(r)[None, :]) / l
    tr = jnp.asarray(np.cos(ang)[:, :, None], F32)
    ti = jnp.asarray(-np.sin(ang)[:, :, None], F32)
    z = pl.pallas_call(
        functools.partial(_fn1_kernel, r=r),
        grid=(cw,),
        in_specs=[pl.BlockSpec((2 * r, r), lambda j: (0, 0)),
                  pl.BlockSpec((r, FN_WIDTH), lambda j: (0, j)),
                  pl.BlockSpec((None, r, 1), lambda j: (j, 0, 0)),
                  pl.BlockSpec((None, r, 1), lambda j: (j, 0, 0))],
        out_specs=pl.BlockSpec((2 * r, FN_WIDTH), lambda j: (0, j)),
        out_shape=jax.ShapeDtypeStruct((2 * r, cw * FN_WIDTH), BF16),
        compiler_params=_params(("arbitrary",), 24 << 20),
        name="fourier_stage1",
    )(w1, u.reshape(r, cw * FN_WIDTH), tr, ti)
    z2 = z.reshape(2 * r * cw, FN_WIDTH)
    c_w, s_w = _dft_cs(cw)
    eye = np.eye(FN_K1B)
    kc = jnp.asarray(np.kron(eye, c_w), BF16)
    ks = jnp.asarray(np.kron(eye, s_w), BF16)
    c_c, s_c = _dft_cs(FN_GROUP_DIM)
    nb = r // FN_K1B
    tb = FN_K1B * cw
    out = pl.pallas_call(
        functools.partial(_fn2_kernel, scale=float((l * FN_GROUP_DIM) ** -0.5)),
        grid=(nb,),
        in_specs=[pl.BlockSpec((tb, FN_WIDTH), lambda b: (b, 0)),
                  pl.BlockSpec((tb, FN_WIDTH), lambda b: (nb + b, 0)),
                  pl.BlockSpec((tb, tb), lambda b: (0, 0)),
                  pl.BlockSpec((tb, tb), lambda b: (0, 0)),
                  pl.BlockSpec((FN_GROUP_DIM, FN_GROUP_DIM), lambda b: (0, 0)),
                  pl.BlockSpec((FN_GROUP_DIM, FN_GROUP_DIM), lambda b: (0, 0)),
                  pl.BlockSpec((FN_GROUPS, FN_GROUP_DIM, FN_GROUP_DIM), lambda b: (0, 0, 0))],
        out_specs=pl.BlockSpec((cw, FN_K1B * FN_WIDTH), lambda b: (0, b)),
        out_shape=jax.ShapeDtypeStruct((cw, r * FN_WIDTH), BF16),
        compiler_params=_params(("arbitrary",), 24 << 20),
        name="fourier_stage2",
    )(z2, z2, kc, ks, jnp.asarray(c_c, BF16), jnp.asarray(s_c, BF16), w_fn)
    return out.reshape(l, FN_WIDTH)


def _fn_ctx_kernel(u_ref, cl_ref, sl_ref, c_ref, s_ref, w_ref, o_ref, *, scale):
    u = u_ref[...]
    gr = _dot(cl_ref[...], u).astype(BF16)
    gi = (-_dot(sl_ref[...], u)).astype(BF16)
    for g in range(FN_GROUPS):
        lo = g * FN_GROUP_DIM
        f = _dot(gr[:, lo:lo + FN_GROUP_DIM], c_ref[...]) + _dot(gi[:, lo:lo + FN_GROUP_DIM], s_ref[...])
        o_ref[:, lo:lo + FN_GROUP_DIM] = _dot((f * scale).astype(BF16), w_ref[g].astype(BF16)).astype(o_ref.dtype)


def fourier_context(u, w_fn):
    c = u.shape[0]
    c_l, s_l = _dft_cs(c)
    c_c, s_c = _dft_cs(FN_GROUP_DIM)
    return pl.pallas_call(
        functools.partial(_fn_ctx_kernel, scale=float((c * FN_GROUP_DIM) ** -0.5)),
        out_shape=jax.ShapeDtypeStruct((c, FN_WIDTH), BF16),
        compiler_params=_params((), 16 << 20),
        name="fourier_context",
    )(u, jnp.asarray(c_l, BF16), jnp.asarray(s_l, BF16), jnp.asarray(c_c, BF16), jnp.asarray(s_c, BF16), w_fn)


def _split_hi_lo(x):
    hi = x.astype(BF16)
    return hi, (x - hi.astype(F32)).astype(BF16)


def _gla_kernel(q_ref, k_ref, v_ref, lr_ref, cos_ref, sin_ref, w2_ref, bg_ref, perm_ref, s0_ref,
                o_ref, sfin_ref, st_ref, *, rev, nchunks):
    step = pl.program_id(0)

    @pl.when(step == 0)
    def _():
        st_ref[...] = s0_ref[...]

    cs = GLA_CHUNK
    sub = GLA_SUB
    nsub = cs // sub
    z = _dot(lr_ref[...].astype(BF16), w2_ref[...].astype(BF16)) + bg_ref[...]
    la = -(jnp.maximum(-z, 0.0) + jnp.log(1.0 + jnp.exp(-jnp.abs(z)))) * (1.0 / GLA_GATE_TEMP)
    ri = lax.broadcasted_iota(jnp.int32, (cs, cs), 0)
    ci = lax.broadcasted_iota(jnp.int32, (cs, cs), 1)
    tri = jnp.where((ci >= ri) if rev else (ci <= ri), 1.0, 0.0).astype(BF16)
    la_hi, la_lo = _split_hi_lo(la)
    bcum = _dot(tri, la_hi) + _dot(tri, la_lo)
    edge = 0 if rev else cs - 1
    row_id = lax.broadcasted_iota(jnp.int32, (cs, GLA_DKP), 0)
    sub_r = lax.broadcasted_iota(jnp.int32, (sub, GLA_DKP), 0)
    lane_c = lax.broadcasted_iota(jnp.int32, (sub, cs), 1)
    cos, sin = cos_ref[...], sin_ref[...]
    perm = perm_ref[...]
    qscale = GLA_DK ** -0.5
    for h in range(GLA_HEADS):
        ks_ = slice(h * GLA_DKP, (h + 1) * GLA_DKP)
        vs_ = slice(h * GLA_DV, (h + 1) * GLA_DV)
        qb, kb = q_ref[:, ks_], k_ref[:, ks_]
        q = (qb.astype(F32) * cos + _dot(qb, perm) * sin) * qscale
        k = kb.astype(F32) * cos + _dot(kb, perm) * sin
        v = v_ref[:, vs_]
        b = bcum[:, ks_]
        b_edge = b[edge:edge + 1, :]
        st = st_ref[h]
        o = _dot_nt((q * jnp.exp(b)).astype(BF16), st.astype(BF16))
        slabs = []
        for blk in range(nsub):
            lo = blk * sub
            q_i = q[lo:lo + sub]
            b_i = b[lo:lo + sub]
            k_i = k[lo:lo + sub]
            acc = jnp.zeros((sub, cs), F32)
            if rev and blk < nsub - 1:
                ref_row = b[lo + sub:lo + sub + 1, :]
                outside = row_id >= lo + sub
            elif (not rev) and blk > 0:
                ref_row = b[lo - 1:lo, :]
                outside = row_id < lo
            else:
                ref_row = None
            if ref_row is not None:
                qe = q_i * jnp.exp(b_i - ref_row)
                ke = jnp.where(outside, k * jnp.exp(jnp.where(outside, ref_row - b, 0.0)), 0.0)
                acc = acc + _dot_nt(qe.astype(BF16), ke.astype(BF16))
            for j in range(sub):
                keep = (sub_r <= j) if rev else (sub_r >= j)
                d = jnp.exp(jnp.where(keep, b_i - b_i[j:j + 1, :], NEG))
                col = jnp.sum(q_i * d * k_i[j:j + 1, :], axis=-1, keepdims=True)
                acc = acc + jnp.where(lane_c == lo + j, col, 0.0)
            slabs.append(acc)
        attn = jnp.concatenate(slabs, axis=0)
        o = o + _dot(attn.astype(BF16), v)
        o_ref[:, vs_] = o
        kend = (k * jnp.exp(b_edge - b)).astype(BF16)
        st_ref[h] = st * jnp.exp(b_edge) + _dot_tn(v, kend)

    @pl.when(step == nchunks - 1)
    def _():
        sfin_ref[...] = st_ref[...]


def gla_scan(u_qk, u_vr, lr, cos, sin, w2p, bgp, perm, s0, rev):
    l = u_qk.shape[0]
    n = l // GLA_CHUNK
    cs = GLA_CHUNK
    kw = GLA_HEADS * GLA_DKP
    ch = (lambda s: n - 1 - s) if rev else (lambda s: s)
    full2 = lambda s: (0, 0)
    o, sfin = pl.pallas_call(
        functools.partial(_gla_kernel, rev=rev, nchunks=n),
        grid=(n,),
        in_specs=[pl.BlockSpec((cs, kw), lambda s: (ch(s), 0)),
                  pl.BlockSpec((cs, kw), lambda s: (ch(s), 1)),
                  pl.BlockSpec((cs, GLA_VW), lambda s: (ch(s), 0)),
                  pl.BlockSpec((cs, LANE), lambda s: (ch(s), 0)),
                  pl.BlockSpec((cs, GLA_DKP), lambda s: (ch(s), 0)),
                  pl.BlockSpec((cs, GLA_DKP), lambda s: (ch(s), 0)),
                  pl.BlockSpec((LANE, kw), full2),
                  pl.BlockSpec((1, kw), full2),
                  pl.BlockSpec((GLA_DKP, GLA_DKP), full2),
                  pl.BlockSpec((GLA_HEADS, GLA_DV, GLA_DKP), lambda s: (0, 0, 0))],
        out_specs=[pl.BlockSpec((cs, GLA_VW), lambda s: (ch(s), 0)),
                   pl.BlockSpec((GLA_HEADS, GLA_DV, GLA_DKP), lambda s: (0, 0, 0))],
        out_shape=[jax.ShapeDtypeStruct((l, GLA_VW), F32),
                   jax.ShapeDtypeStruct((GLA_HEADS, GLA_DV, GLA_DKP), F32)],
        scratch_shapes=[pltpu.VMEM((GLA_HEADS, GLA_DV, GLA_DKP), F32)],
        compiler_params=_params(("arbitrary",), 32 << 20),
        name="gla_bwd" if rev else "gla_fwd",
    )(u_qk, u_qk, u_vr, lr, cos, sin, w2p, bgp, perm, s0)
    return o, sfin


def _gla_out_kernel(of_ref, ob_ref, r_ref, g_ref, o_ref):
    o = of_ref[...] + ob_ref[...]
    g = g_ref[...]
    r = r_ref[...].astype(F32)
    for h in range(GLA_HEADS):
        sl = slice(h * GLA_DV, (h + 1) * GLA_DV)
        oh = o[:, sl]
        on = oh * lax.rsqrt(jnp.mean(oh * oh, axis=-1, keepdims=True) + EPS) * g[:, sl]
        o_ref[:, sl] = (on * _silu(r[:, sl])).astype(o_ref.dtype)


def gla_output(o_f, o_b, u_vr, g_gla, tm=256):
    l = o_f.shape[0]
    tm = min(tm, l)
    return pl.pallas_call(
        _gla_out_kernel,
        grid=(l // tm,),
        in_specs=[pl.BlockSpec((tm, GLA_VW), lambda i: (i, 0)),
                  pl.BlockSpec((tm, GLA_VW), lambda i: (i, 0)),
                  pl.BlockSpec((tm, GLA_VW), lambda i: (i, 1)),
                  pl.BlockSpec((1, GLA_VW), lambda i: (0, 0))],
        out_specs=pl.BlockSpec((tm, GLA_VW), lambda i: (i, 0)),
        out_shape=jax.ShapeDtypeStruct((l, GLA_VW), BF16),
        compiler_params=_params(("arbitrary",), 24 << 20),
        name="gla_output",
    )(o_f, o_b, u_vr, g_gla.reshape(1, GLA_VW))


def rope_tables(n_tokens):
    seg = GLA_DK // 2
    half = seg // 2
    inv = ROPE_THETA ** (-jnp.arange(half, dtype=F32) / half)
    pos = jnp.arange(n_tokens)
    ang_r = (pos // GRID_W).astype(F32)[:, None] * inv
    ang_c = (pos % GRID_W).astype(F32)[:, None] * inv
    pad1 = jnp.ones((n_tokens, GLA_DKP - GLA_DK), F32)
    pad0 = jnp.zeros((n_tokens, GLA_DKP - GLA_DK), F32)
    cos = jnp.concatenate([jnp.cos(ang_r)] * 2 + [jnp.cos(ang_c)] * 2 + [pad1], axis=1)
    sin = jnp.concatenate([jnp.sin(ang_r)] * 2 + [jnp.sin(ang_c)] * 2 + [pad0], axis=1)
    return cos, sin


def rope_perm():
    seg = GLA_DK // 2
    half = seg // 2
    p = np.zeros((GLA_DKP, GLA_DKP), np.float32)
    for s0 in (0, seg):
        for j in range(half):
            p[s0 + half + j, s0 + j] = -1.0
            p[s0 + j, s0 + half + j] = 1.0
    return jnp.asarray(p, BF16)


def _pad_heads(w):
    lead = w.shape[:-1]
    w = w.reshape(lead + (GLA_HEADS, GLA_DK))
    w = jnp.pad(w, [(0, 0)] * len(lead) + [(0, 0), (0, GLA_DKP - GLA_DK)])
    return w.reshape(lead + (GLA_HEADS * GLA_DKP,))


def _moe_up_kernel(a_ref, w_ref, comb_ref, h_ref):
    p = pl.program_id(1)
    gu = _dot(a_ref[...], w_ref[...])
    f2 = 2 * MOE_HIDDEN
    gate, up = gu[:, :f2], gu[:, f2:]
    comb = comb_ref[...]
    lane = lax.broadcasted_iota(jnp.int32, comb.shape, 1)
    c0 = _lane_pick(comb, lane, 2 * p)
    c1 = _lane_pick(comb, lane, 2 * p + 1)
    lane2 = lax.broadcasted_iota(jnp.int32, gate.shape, 1)
    h_ref[...] = (_silu(gate) * up * jnp.where(lane2 < MOE_HIDDEN, c0, c1)).astype(h_ref.dtype)


def moe_up(xn, w_gu2, comb, tm=1024):
    t, d = xn.shape
    tm = min(tm, t)
    npair = w_gu2.shape[0]
    f2 = 2 * MOE_HIDDEN
    return pl.pallas_call(
        _moe_up_kernel,
        grid=(t // tm, npair),
        in_specs=[pl.BlockSpec((tm, d), lambda i, p: (i, 0)),
                  pl.BlockSpec((None, d, 2 * f2), lambda i, p: (p, 0, 0)),
                  pl.BlockSpec((tm, LANE), lambda i, p: (i, 0))],
        out_specs=pl.BlockSpec((tm, f2), lambda i, p: (i, p)),
        out_shape=jax.ShapeDtypeStruct((t, npair * f2), BF16),
        compiler_params=_params(("arbitrary", "arbitrary"),
                                2 * tm * d * 2 + 2 * d * 2 * f2 * 2 + 8 * tm * 2 * f2 * 4),
        name="moe_up",
    )(xn, w_gu2, comb)


def _layer(l, x, xc, mod, last, consts, w):
    (g_mix, w_in, rpb, w_fn, w_g2, b_g, g_gla, w_out, g_ffn, w_rg, b_rg, w_re, b_re, w_gu, w_dn) = w
    cos, sin, cos_c, sin_c, perm, na_tab = consts
    d = D_MODEL
    s_len = x.shape[0]
    m_x = mod[l, 0].reshape(N_MOD, d)
    m_c = mod[l, 1].reshape(N_MOD, d)

    w_lr = jnp.pad(w_in[l, :, OFF_LR:OFF_LR + 2 * GLA_GATE_RANK], ((0, 0), (0, LANE - 2 * GLA_GATE_RANK)))
    xn, lr = norm_proj(x, g_mix[l], m_x[0], m_x[1], w_lr)
    xcn, lr_c = norm_proj(xc, g_mix[l], m_c[0], m_c[1], w_lr)

    w_qk = jnp.concatenate([_pad_heads(w_in[l, :, OFF_GQK:OFF_GQK + GLA_KW]),
                            _pad_heads(w_in[l, :, OFF_GQK + GLA_KW:OFF_GVR])], axis=1)
    proj = []
    for a in (xn, xcn):
        u_na = matmul(a, w_in, (l,), OFF_NA, 3 * NA_WIDTH, 512, 1024, BF16, name="in_na")
        u_fn = matmul(a, w_in, (l,), OFF_FN, FN_WIDTH, 512, 1024, BF16, name="in_fn")
        u_qk = matmul(a, w_qk, (), 0, 2 * GLA_HEADS * GLA_DKP, 512, 1024, BF16, name="in_gqk")
        u_vr = matmul(a, w_in, (l,), OFF_GVR, 2 * GLA_VW, 512, 1024, BF16, name="in_gvr")
        proj.append((u_na, u_fn, u_qk, u_vr))
    (u_na, u_fn, u_qk, u_vr), (c_na, c_fn, c_qk, c_vr) = proj

    w2p = [jnp.pad(_pad_heads(w_g2[l, dr]), ((dr * GLA_GATE_RANK, LANE - (dr + 1) * GLA_GATE_RANK), (0, 0)))
           for dr in (0, 1)]
    bgp = [_pad_heads(b_g[l, dr]).reshape(1, -1) for dr in (0, 1)]
    zero = jnp.zeros((GLA_HEADS, GLA_DV, GLA_DKP), F32)
    outs_c, outs_x = [], []
    for dr in (0, 1):
        o_c, s_c = gla_scan(c_qk, c_vr, lr_c, cos_c, sin_c, w2p[dr], bgp[dr], perm, zero, rev=bool(dr))
        o_x, _ = gla_scan(u_qk, u_vr, lr, cos, sin, w2p[dr], bgp[dr], perm, s_c, rev=bool(dr))
        outs_c.append(o_c)
        outs_x.append(o_x)

    mix = jnp.concatenate([neighbourhood_attention(u_na, c_na, na_tab[l]),
                           fourier_latent(u_fn, w_fn[l]),
                           gla_output(outs_x[0], outs_x[1], u_vr, g_gla[l])], axis=1)
    x = matmul(mix, w_out, (l,), 0, d, 512, 1024, F32, res=x, gate=m_x[2], name="out_proj")

    w_route = jnp.pad(jnp.concatenate([w_re[l], w_rg[l]], axis=1),
                      ((0, 0), (0, LANE - MOE_EXPERTS - MOE_GROUPS)))
    b_route = jnp.pad(jnp.concatenate([b_re[l], b_rg[l]]), (0, LANE - MOE_EXPERTS - MOE_GROUPS)).reshape(1, LANE)
    f = MOE_HIDDEN
    w_gu2 = (w_gu[l].reshape(MOE_EXPERTS // 2, 2, d, 2, f).transpose(0, 2, 3, 1, 4)
             .reshape(MOE_EXPERTS // 2, d, 4 * f).astype(BF16))
    w_dn2 = w_dn.reshape(DEPTH, MOE_EXPERTS * f, d)

    def ffn(y, m):
        yn, comb = norm_proj(y, g_ffn[l], m[3], m[4], w_route, b_route)
        hid = moe_up(yn, w_gu2, comb)
        return matmul(hid, w_dn2, (l,), 0, d, 256, 1024, F32, res=y, gate=m[5], name="moe_down")

    x = ffn(x, m_x)
    if not last:
        mix_c = jnp.concatenate([full_attention(c_na),
                                 fourier_context(c_fn, w_fn[l]),
                                 gla_output(outs_c[0], outs_c[1], c_vr, g_gla[l])], axis=1)
        xc = matmul(mix_c, w_out, (l,), 0, d, 512, 1024, F32, res=xc, gate=m_c[2], name="out_proj_ctx")
        xc = ffn(xc, m_c)
    return x, xc


def kernel(x, c, ctx, c_ctx, w_ada, b_ada, g_mix, w_in, rpb, w_fn, w_g2, b_g, g_gla, w_out,
           g_ffn, w_rg, b_rg, w_re, b_re, w_gu, w_dn, g_final):
    assert x.shape[0] == 1 and c.shape[0] == 1
    d = D_MODEL
    s_len = x.shape[1]
    c_len = ctx.shape[1]
    cc = jnp.concatenate([c, c_ctx[None], jnp.zeros((6, d), F32)], axis=0)
    mod = ada_mod(cc, w_ada, b_ada)
    cos, sin = rope_tables(s_len)
    cos_c = jnp.ones((c_len, GLA_DKP), F32)
    sin_c = jnp.zeros((c_len, GLA_DKP), F32)
    consts = (cos, sin, cos_c, sin_c, rope_perm(), na_tables(rpb, s_len // GRID_W))
    w = (g_mix, w_in, rpb, w_fn, w_g2, b_g, g_gla, w_out, g_ffn, w_rg, b_rg, w_re, b_re, w_gu, w_dn)
    xs, xc = x[0], ctx[0]
    for l in range(DEPTH):
        xs, xc = _layer(l, xs, xc, mod, l == DEPTH - 1, consts, w)
    return final_norm(xs, g_final)[None]
```

```python
import functools
import math

import numpy as np
import jax
import jax.numpy as jnp
from jax import lax
from jax.experimental import pallas as pl
from jax.experimental.pallas import tpu as pltpu

F32 = jnp.float32
BF16 = jnp.bfloat16

D_MODEL = 4096
DEPTH = 4
GRID_W = 64
EPS = 1e-6
N_MOD = 6

NA_HEADS = 12
NA_HEAD_DIM = 128
NA_WIN_H = 8
NA_WIN_W = 16
NA_WIDTH = NA_HEADS * NA_HEAD_DIM

FN_GROUPS = 8
FN_GROUP_DIM = 128
FN_WIDTH = FN_GROUPS * FN_GROUP_DIM

GLA_HEADS = 4
GLA_DK = 192
GLA_DKP = 256
GLA_DV = 384
GLA_KW = GLA_HEADS * GLA_DK
GLA_VW = GLA_HEADS * GLA_DV
GLA_GATE_RANK = 16
GLA_GATE_TEMP = 16.0
GLA_CHUNK = 64
GLA_SUB = 16
ROPE_THETA = 10000.0

MOE_GROUPS = 4
MOE_EXPERTS_PER_GROUP = 8
MOE_EXPERTS = MOE_GROUPS * MOE_EXPERTS_PER_GROUP
MOE_HIDDEN = 192

OFF_NA = 0
OFF_FN = 3 * NA_WIDTH
OFF_GQK = OFF_FN + FN_WIDTH
OFF_GVR = OFF_GQK + 2 * GLA_KW
OFF_LR = OFF_GVR + 2 * GLA_VW

LANE = 128
NEG = -1e30
V7X_VMEM_BUDGET = 56 * 1024 * 1024


def _params(sem, vmem_bytes):
    return pltpu.CompilerParams(dimension_semantics=sem,
                                vmem_limit_bytes=int(min(max(vmem_bytes, 16 << 20), V7X_VMEM_BUDGET)))


def _dot(a, b):
    return jnp.dot(a, b, preferred_element_type=F32)


def _dot_nt(a, b):
    return lax.dot_general(a, b, (((1,), (1,)), ((), ())), preferred_element_type=F32)


def _dot_tn(a, b):
    return lax.dot_general(a, b, (((0,), (0,)), ((), ())), preferred_element_type=F32)


def _silu(x):
    return x * (1.0 / (1.0 + jnp.exp(-x)))


def _ada_kernel(c_ref, w_ref, b_ref, o_ref):
    s = _silu(c_ref[...]).astype(BF16)
    o_ref[0] = _dot(s, w_ref[0].astype(BF16)) + b_ref[0]


def ada_mod(cc, w_ada, b_ada, tn=512):
    n, d, nd = w_ada.shape
    return pl.pallas_call(
        _ada_kernel,
        grid=(n, nd // tn),
        in_specs=[pl.BlockSpec((8, d), lambda l, j: (0, 0)),
                  pl.BlockSpec((1, d, tn), lambda l, j: (l, 0, j)),
                  pl.BlockSpec((1, 1, tn), lambda l, j: (l, 0, j))],
        out_specs=pl.BlockSpec((1, 8, tn), lambda l, j: (l, 0, j)),
        out_shape=jax.ShapeDtypeStruct((n, 8, nd), F32),
        compiler_params=_params(("arbitrary", "arbitrary"), 3 * d * tn * 4 + (4 << 20)),
        name="ada_mod",
    )(cc, w_ada, b_ada.reshape(n, 1, nd))


def _normed(x_ref, g_ref, sh_ref, sc_ref):
    x = x_ref[...]
    y = x * lax.rsqrt(jnp.mean(x * x, axis=-1, keepdims=True) + EPS)
    return (y * g_ref[...]) * (1.0 + sc_ref[...]) + sh_ref[...]


def _norm_proj_kernel(x_ref, g_ref, sh_ref, sc_ref, w_ref, xn_ref, p_ref):
    hb = _normed(x_ref, g_ref, sh_ref, sc_ref).astype(BF16)
    xn_ref[...] = hb
    p_ref[...] = _dot(hb, w_ref[...].astype(BF16))


def _lane_pick(v, lane, idx):
    return jnp.sum(jnp.where(lane == idx, v, 0.0), axis=-1, keepdims=True)


def _norm_route_kernel(x_ref, g_ref, sh_ref, sc_ref, w_ref, b_ref, xn_ref, route_ref):
    h = _normed(x_ref, g_ref, sh_ref, sc_ref)
    xn_ref[...] = h
    hb = h.astype(BF16)
    logits = _dot(hb, w_ref[...].astype(BF16)) + b_ref[...]
    lane = lax.broadcasted_iota(jnp.int32, logits.shape, 1).astype(F32)
    far = float(4 * LANE)
    is_g = (lane >= MOE_EXPERTS) & (lane < MOE_EXPERTS + MOE_GROUPS)
    gl = jnp.where(is_g, logits, NEG)
    gmax = jnp.max(gl, axis=-1, keepdims=True)
    g_top = jnp.min(jnp.where(gl == gmax, lane, far), axis=-1, keepdims=True) - MOE_EXPERTS
    g_w = 1.0 / jnp.sum(jnp.where(is_g, jnp.exp(gl - gmax), 0.0), axis=-1, keepdims=True)
    in_grp = (lane >= g_top * MOE_EXPERTS_PER_GROUP) & (lane < (g_top + 1) * MOE_EXPERTS_PER_GROUP)
    el = jnp.where(in_grp, logits, NEG)
    v1 = jnp.max(el, axis=-1, keepdims=True)
    i1 = jnp.min(jnp.where(el == v1, lane, far), axis=-1, keepdims=True)
    el2 = jnp.where(lane == i1, NEG, el)
    v2 = jnp.max(el2, axis=-1, keepdims=True)
    i2 = jnp.min(jnp.where(el2 == v2, lane, far), axis=-1, keepdims=True)
    e2 = jnp.exp(v2 - v1)
    w1 = g_w / (1.0 + e2)
    w2 = g_w * e2 / (1.0 + e2)
    route_ref[...] = (jnp.where(lane == 0.0, i1, 0.0) + jnp.where(lane == 1.0, i2, 0.0)
                      + jnp.where(lane == 2.0, w1, 0.0) + jnp.where(lane == 3.0, w2, 0.0))


def norm_proj(x, g, sh, sc, w_small, b_small=None, tm=256):
    t, d = x.shape
    row = lambda i: (0, 0)
    vec = pl.BlockSpec((1, d), row)
    ins = [pl.BlockSpec((tm, d), lambda i: (i, 0)), vec, vec, vec, pl.BlockSpec((d, LANE), row)]
    args = [x, g.reshape(1, d), sh.reshape(1, d), sc.reshape(1, d), w_small]
    if b_small is None:
        body = _norm_proj_kernel
    else:
        body = _norm_route_kernel
        ins.append(pl.BlockSpec((1, LANE), row))
        args.append(b_small)
    return pl.pallas_call(
        body,
        grid=(t // tm,),
        in_specs=ins,
        out_specs=[pl.BlockSpec((tm, d), lambda i: (i, 0)), pl.BlockSpec((tm, LANE), lambda i: (i, 0))],
        out_shape=[jax.ShapeDtypeStruct((t, d), BF16 if b_small is None else F32),
                   jax.ShapeDtypeStruct((t, LANE), F32)],
        compiler_params=_params(("arbitrary",), 8 * tm * d * 4 + 4 * d * LANE * 4),
        name="norm_proj" if b_small is None else "norm_route",
    )(*args)


def _final_norm_kernel(x_ref, g_ref, o_ref):
    x = x_ref[...]
    o_ref[...] = x * lax.rsqrt(jnp.mean(x * x, axis=-1, keepdims=True) + EPS) * g_ref[...]


def final_norm(x, g, tm=256):
    t, d = x.shape
    return pl.pallas_call(
        _final_norm_kernel,
        grid=(t // tm,),
        in_specs=[pl.BlockSpec((tm, d), lambda i: (i, 0)), pl.BlockSpec((1, d), lambda i: (0, 0))],
        out_specs=pl.BlockSpec((tm, d), lambda i: (i, 0)),
        out_shape=jax.ShapeDtypeStruct((t, d), F32),
        compiler_params=_params(("arbitrary",), 6 * tm * d * 4),
        name="final_norm",
    )(x, g.reshape(1, d))


def _mm_kernel(a_ref, w_ref, o_ref):
    o_ref[...] = _dot(a_ref[...], w_ref[...].astype(BF16)).astype(o_ref.dtype)


def _mm_res_kernel(a_ref, w_ref, r_ref, g_ref, o_ref):
    o_ref[...] = r_ref[...] + g_ref[...] * _dot(a_ref[...], w_ref[...].astype(BF16))


def matmul(a, w, w_lead, col0, ncols, tn, tm, out_dtype, res=None, gate=None, name="mm"):
    t, k = a.shape
    tm = min(tm, t)
    nlead = len(w_lead)
    j0 = col0 // tn
    w_spec = pl.BlockSpec((None,) * nlead + (k, tn), lambda i, j: tuple(w_lead) + (0, j + j0))
    ins = [pl.BlockSpec((tm, k), lambda i, j: (i, 0)), w_spec]
    args = [a, w]
    wbytes = jnp.dtype(w.dtype).itemsize
    vmem = 2 * tm * k * 2 + 3 * k * tn * wbytes + 6 * tm * tn * 4
    if res is None:
        body = _mm_kernel
    else:
        body = _mm_res_kernel
        ins += [pl.BlockSpec((tm, tn), lambda i, j: (i, j)), pl.BlockSpec((1, tn), lambda i, j: (0, j))]
        args += [res, gate.reshape(1, ncols)]
    return pl.pallas_call(
        body,
        grid=(t // tm, ncols // tn),
        in_specs=ins,
        out_specs=pl.BlockSpec((tm, tn), lambda i, j: (i, j)),
        out_shape=jax.ShapeDtypeStruct((t, ncols), out_dtype),
        compiler_params=_params(("arbitrary", "arbitrary"), vmem),
        name=name,
    )(*args)


NA_QROWS = 4
NA_SLAB = 12


def na_tables(rpb, rows):
    nblk = rows // NA_QROWS
    n_, h_ = rpb.shape[:2]
    qc = np.arange(GRID_W)[:, None]
    kc = np.arange(GRID_W)[None, :]
    c0 = np.clip(qc - NA_WIN_W // 2, 0, GRID_W - NA_WIN_W)
    ok_c = (kc >= c0) & (kc < c0 + NA_WIN_W)
    a = np.arange(NA_QROWS)[:, None]
    b = np.arange(NA_SLAB)[None, :]
    dr_l, ok_l = [], []
    for i in (0, 1, nblk - 1):
        base = int(np.clip(i * NA_QROWS - NA_WIN_H // 2, 0, rows - NA_SLAB))
        r = i * NA_QROWS + a
        r0 = np.clip(r - NA_WIN_H // 2, 0, rows - NA_WIN_H)
        krow = base + b
        ok_l.append((krow >= r0) & (krow < r0 + NA_WIN_H))
        dr_l.append(np.clip(krow - r + NA_WIN_H - 1, 0, 2 * NA_WIN_H - 2))
    dr = np.stack(dr_l).reshape(-1)
    ok = np.stack(ok_l)[:, :, None, :, None] & ok_c[None, None, :, None, :]
    lo = GRID_W - NA_WIN_W
    padded = jnp.pad(rpb.astype(F32), ((0, 0), (0, 0), (0, 0), (lo, lo)))
    by_col = jnp.stack([padded[..., GRID_W - 1 - q:2 * GRID_W - 1 - q] for q in range(GRID_W)], axis=3)
    by_row = jnp.stack([by_col[:, :, int(d_)] for d_ in dr], axis=2)
    bias = by_row.reshape(n_, h_, 3, NA_QROWS, NA_SLAB, GRID_W, GRID_W).transpose(0, 2, 1, 3, 5, 4, 6)
    bias = jnp.where(jnp.asarray(ok)[None, :, None], bias, NEG)
    return bias.reshape(n_, 3, h_, NA_QROWS * GRID_W, NA_SLAB * GRID_W)


def _na_kernel(q_ref, k_ref, v_ref, kc_ref, vc_ref, t_ref, o_ref, *, rows):
    i = pl.program_id(1)
    base = jnp.clip(i * NA_QROWS - NA_WIN_H // 2, 0, rows - NA_SLAB) * GRID_W
    base = pl.multiple_of(base, NA_QROWS * GRID_W)
    n_keys = NA_SLAB * GRID_W
    scale = NA_HEAD_DIM ** -0.5
    q = q_ref[...]
    k = k_ref[pl.ds(base, n_keys), :]
    v = v_ref[pl.ds(base, n_keys), :]
    s = _dot_nt(q, k) * scale + t_ref[...]
    sc = _dot_nt(q, kc_ref[...]) * scale
    m = jnp.maximum(jnp.max(s, axis=-1, keepdims=True), jnp.max(sc, axis=-1, keepdims=True))
    p = jnp.exp(s - m)
    pc = jnp.exp(sc - m)
    den = jnp.sum(p, axis=-1, keepdims=True) + jnp.sum(pc, axis=-1, keepdims=True)
    o = _dot(p.astype(BF16), v) + _dot(pc.astype(BF16), vc_ref[...])
    o_ref[...] = (o / den).astype(o_ref.dtype)


def neighbourhood_attention(u, uc, tables):
    s = u.shape[0]
    c = uc.shape[0]
    rows = s // GRID_W
    nblk = rows // NA_QROWS
    tq = NA_QROWS * GRID_W
    h_ = NA_HEADS
    hd = NA_HEAD_DIM

    def variant(i):
        return jnp.where(i == 0, 0, jnp.where(i == nblk - 1, 2, 1))

    return pl.pallas_call(
        functools.partial(_na_kernel, rows=rows),
        grid=(h_, nblk),
        in_specs=[pl.BlockSpec((tq, hd), lambda h, i: (i, h)),
                  pl.BlockSpec((s, hd), lambda h, i: (0, h_ + h)),
                  pl.BlockSpec((s, hd), lambda h, i: (0, 2 * h_ + h)),
                  pl.BlockSpec((c, hd), lambda h, i: (0, h_ + h)),
                  pl.BlockSpec((c, hd), lambda h, i: (0, 2 * h_ + h)),
                  pl.BlockSpec((None, None, tq, NA_SLAB * GRID_W), lambda h, i: (variant(i), h, 0, 0))],
        out_specs=pl.BlockSpec((tq, hd), lambda h, i: (i, h)),
        out_shape=jax.ShapeDtypeStruct((s, NA_WIDTH), BF16),
        compiler_params=_params(("arbitrary", "arbitrary"),
                                4 * s * hd * 2 + 2 * tq * NA_SLAB * GRID_W * 4 * 4 + (4 << 20)),
        name="na_latent",
    )(u, u, u, uc, uc, tables)


def _full_attn_kernel(q_ref, k_ref, v_ref, o_ref):
    s = _dot_nt(q_ref[...], k_ref[...]) * NA_HEAD_DIM ** -0.5
    p = jnp.exp(s - jnp.max(s, axis=-1, keepdims=True))
    den = jnp.sum(p, axis=-1, keepdims=True)
    o_ref[...] = (_dot(p.astype(BF16), v_ref[...]) / den).astype(o_ref.dtype)


def full_attention(uc):
    c = uc.shape[0]
    h_ = NA_HEADS
    hd = NA_HEAD_DIM
    return pl.pallas_call(
        _full_attn_kernel,
        grid=(h_,),
        in_specs=[pl.BlockSpec((c, hd), lambda h: (0, h)),
                  pl.BlockSpec((c, hd), lambda h: (0, h_ + h)),
                  pl.BlockSpec((c, hd), lambda h: (0, 2 * h_ + h))],
        out_specs=pl.BlockSpec((c, hd), lambda h: (0, h)),
        out_shape=jax.ShapeDtypeStruct((c, NA_WIDTH), BF16),
        compiler_params=_params(("arbitrary",), 16 << 20),
        name="na_context",
    )(uc, uc, uc)


def _dft_cs(n):
    idx = np.arange(n)
    ang = 2.0 * np.pi * ((idx[:, None] * idx[None, :]) % n) / n
    return np.cos(ang), np.sin(ang)


FN_K1B = 8


def _fn1_kernel(w_ref, x_ref, tr_ref, ti_ref, z_ref, *, r):
    y = _dot(w_ref[...], x_ref[...])
    yr, yi = y[:r], y[r:]
    tr, ti = tr_ref[...], ti_ref[...]
    z_ref[:r, :] = (yr * tr - yi * ti).astype(z_ref.dtype)
    z_ref[r:, :] = (yr * ti + yi * tr).astype(z_ref.dtype)


def _fn2_kernel(zr_ref, zi_ref, kc_ref, ks_ref, c_ref, s_ref, w_ref, o_ref, *, scale):
    zr, zi = zr_ref[...], zi_ref[...]
    kc, ks = kc_ref[...], ks_ref[...]
    xr = (_dot(kc, zr) + _dot(ks, zi)).astype(BF16)
    xi = (_dot(kc, zi) - _dot(ks, zr)).astype(BF16)
    cw = GRID_W
    for g in range(FN_GROUPS):
        lo = g * FN_GROUP_DIM
        f = _dot(xr[:, lo:lo + FN_GROUP_DIM], c_ref[...]) + _dot(xi[:, lo:lo + FN_GROUP_DIM], s_ref[...])
        y = _dot((f * scale).astype(BF16), w_ref[g].astype(BF16))
        for kk in range(FN_K1B):
            o_ref[:, kk * FN_WIDTH + lo:kk * FN_WIDTH + lo + FN_GROUP_DIM] = (
                y[kk * cw:(kk + 1) * cw].astype(o_ref.dtype))


def fourier_latent(u, w_fn):
    l = u.shape[0]
    cw = GRID_W
    r = l // cw
    c_r, s_r = _dft_cs(r)
    w1 = jnp.asarray(np.concatenate([c_r, -s_r], axis=0), BF16)
    ang = 2.0 * np.pi * (np.arange(cw)[:, None] * np.arange(r)[None, :]) / l
    tr = jnp.asarray(np.cos(ang)[:, :, None], F32)
    ti = jnp.asarray(-np.sin(ang)[:, :, None], F32)
    z = pl.pallas_call(
        functools.partial(_fn1_kernel, r=r),
        grid=(cw,),
        in_specs=[pl.BlockSpec((2 * r, r), lambda j: (0, 0)),
                  pl.BlockSpec((r, FN_WIDTH), lambda j: (0, j)),
                  pl.BlockSpec((None, r, 1), lambda j: (j, 0, 0)),
                  pl.BlockSpec((None, r, 1), lambda j: (j, 0, 0))],
        out_specs=pl.BlockSpec((2 * r, FN_WIDTH), lambda j: (0, j)),
        out_shape=jax.ShapeDtypeStruct((2 * r, cw * FN_WIDTH), BF16),
        compiler_params=_params(("arbitrary",), 24 << 20),
        name="fourier_stage1",
    )(w1, u.reshape(r, cw * FN_WIDTH), tr, ti)
    z2 = z.reshape(2 * r * cw, FN_WIDTH)
    c_w, s_w = _dft_cs(cw)
    eye = np.eye(FN_K1B)
    kc = jnp.asarray(np.kron(eye, c_w), BF16)
    ks = jnp.asarray(np.kron(eye, s_w), BF16)
    c_c, s_c = _dft_cs(FN_GROUP_DIM)
    nb = r // FN_K1B
    tb = FN_K1B * cw
    out = pl.pallas_call(
        functools.partial(_fn2_kernel, scale=float((l * FN_GROUP_DIM) ** -0.5)),
        grid=(nb,),
        in_specs=[pl.BlockSpec((tb, FN_WIDTH), lambda b: (b, 0)),
                  pl.BlockSpec((tb, FN_WIDTH), lambda b: (nb + b, 0)),
                  pl.BlockSpec((tb, tb), lambda b: (0, 0)),
                  pl.BlockSpec((tb, tb), lambda b: (0, 0)),
                  pl.BlockSpec((FN_GROUP_DIM, FN_GROUP_DIM), lambda b: (0, 0)),
                  pl.BlockSpec((FN_GROUP_DIM, FN_GROUP_DIM), lambda b: (0, 0)),
                  pl.BlockSpec((FN_GROUPS, FN_GROUP_DIM, FN_GROUP_DIM), lambda b: (0, 0, 0))],
        out_specs=pl.BlockSpec((cw, FN_K1B * FN_WIDTH), lambda b: (0, b)),
        out_shape=jax.ShapeDtypeStruct((cw, r * FN_WIDTH), BF16),
        compiler_params=_params(("arbitrary",), 24 << 20),
        name="fourier_stage2",
    )(z2, z2, kc, ks, jnp.asarray(c_c, BF16), jnp.asarray(s_c, BF16), w_fn)
    return out.reshape(l, FN_WIDTH)


def _fn_ctx_kernel(u_ref, cl_ref, sl_ref, c_ref, s_ref, w_ref, o_ref, *, scale):
    u = u_ref[...]
    gr = _dot(cl_ref[...], u).astype(BF16)
    gi = (-_dot(sl_ref[...], u)).astype(BF16)
    for g in range(FN_GROUPS):
        lo = g * FN_GROUP_DIM
        f = _dot(gr[:, lo:lo + FN_GROUP_DIM], c_ref[...]) + _dot(gi[:, lo:lo + FN_GROUP_DIM], s_ref[...])
        o_ref[:, lo:lo + FN_GROUP_DIM] = _dot((f * scale).astype(BF16), w_ref[g].astype(BF16)).astype(o_ref.dtype)


def fourier_context(u, w_fn):
    c = u.shape[0]
    c_l, s_l = _dft_cs(c)
    c_c, s_c = _dft_cs(FN_GROUP_DIM)
    return pl.pallas_call(
        functools.partial(_fn_ctx_kernel, scale=float((c * FN_GROUP_DIM) ** -0.5)),
        out_shape=jax.ShapeDtypeStruct((c, FN_WIDTH), BF16),
        compiler_params=_params((), 16 << 20),
        name="fourier_context",
    )(u, jnp.asarray(c_l, BF16), jnp.asarray(s_l, BF16), jnp.asarray(c_c, BF16), jnp.asarray(s_c, BF16), w_fn)


def _split_hi_lo(x):
    hi = x.astype(BF16)
    return hi, (x - hi.astype(F32)).astype(BF16)


def _gla_kernel(q_ref, k_ref, v_ref, lr_ref, cos_ref, sin_ref, w2_ref, bg_ref, perm_ref, s0_ref,
                o_ref, sfin_ref, st_ref, *, rev, nchunks):
    step = pl.program_id(0)

    @pl.when(step == 0)
    def _():
        st_ref[...] = s0_ref[...]

    cs = GLA_CHUNK
    sub = GLA_SUB
    nsub = cs // sub
    z = _dot(lr_ref[...].astype(BF16), w2_ref[...].astype(BF16)) + bg_ref[...]
    la = -(jnp.maximum(-z, 0.0) + jnp.log(1.0 + jnp.exp(-jnp.abs(z)))) * (1.0 / GLA_GATE_TEMP)
    ri = lax.broadcasted_iota(jnp.int32, (cs, cs), 0)
    ci = lax.broadcasted_iota(jnp.int32, (cs, cs), 1)
    tri = jnp.where((ci >= ri) if rev else (ci <= ri), 1.0, 0.0).astype(BF16)
    la_hi, la_lo = _split_hi_lo(la)
    bcum = _dot(tri, la_hi) + _dot(tri, la_lo)
    edge = 0 if rev else cs - 1
    row_id = lax.broadcasted_iota(jnp.int32, (cs, GLA_DKP), 0)
    sub_r = lax.broadcasted_iota(jnp.int32, (sub, GLA_DKP), 0)
    lane_c = lax.broadcasted_iota(jnp.int32, (sub, cs), 1)
    cos, sin = cos_ref[...], sin_ref[...]
    perm = perm_ref[...]
    qscale = GLA_DK ** -0.5
    for h in range(GLA_HEADS):
        ks_ = slice(h * GLA_DKP, (h + 1) * GLA_DKP)
        vs_ = slice(h * GLA_DV, (h + 1) * GLA_DV)
        qb, kb = q_ref[:, ks_], k_ref[:, ks_]
        q = (qb.astype(F32) * cos + _dot(qb, perm) * sin) * qscale
        k = kb.astype(F32) * cos + _dot(kb, perm) * sin
        v = v_ref[:, vs_]
        b = bcum[:, ks_]
        b_edge = b[edge:edge + 1, :]
        st = st_ref[h]
        o = _dot_nt((q * jnp.exp(b)).astype(BF16), st.astype(BF16))
        slabs = []
        for blk in range(nsub):
            lo = blk * sub
            q_i = q[lo:lo + sub]
            b_i = b[lo:lo + sub]
            k_i = k[lo:lo + sub]
            acc = jnp.zeros((sub, cs), F32)
            if rev and blk < nsub - 1:
                ref_row = b[lo + sub:lo + sub + 1, :]
                outside = row_id >= lo + sub
            elif (not rev) and blk > 0:
                ref_row = b[lo - 1:lo, :]
                outside = row_id < lo
            else:
                ref_row = None
            if ref_row is not None:
                qe = q_i * jnp.exp(b_i - ref_row)
                ke = jnp.where(outside, k * jnp.exp(jnp.where(outside, ref_row - b, 0.0)), 0.0)
                acc = acc + _dot_nt(qe.astype(BF16), ke.astype(BF16))
            for j in range(sub):
                keep = (sub_r <= j) if rev else (sub_r >= j)
                d = jnp.exp(jnp.where(keep, b_i - b_i[j:j + 1, :], NEG))
                col = jnp.sum(q_i * d * k_i[j:j + 1, :], axis=-1, keepdims=True)
                acc = acc + jnp.where(lane_c == lo + j, col, 0.0)
            slabs.append(acc)
        attn = jnp.concatenate(slabs, axis=0)
        o = o + _dot(attn.astype(BF16), v)
        o_ref[:, vs_] = o
        kend = (k * jnp.exp(b_edge - b)).astype(BF16)
        st_ref[h] = st * jnp.exp(b_edge) + _dot_tn(v, kend)

    @pl.when(step == nchunks - 1)
    def _():
        sfin_ref[...] = st_ref[...]


def gla_scan(u_qk, u_vr, lr, cos, sin, w2p, bgp, perm, s0, rev):
    l = u_qk.shape[0]
    n = l // GLA_CHUNK
    cs = GLA_CHUNK
    kw = GLA_HEADS * GLA_DKP
    ch = (lambda s: n - 1 - s) if rev else (lambda s: s)
    full2 = lambda s: (0, 0)
    o, sfin = pl.pallas_call(
        functools.partial(_gla_kernel, rev=rev, nchunks=n),
        grid=(n,),
        in_specs=[pl.BlockSpec((cs, kw), lambda s: (ch(s), 0)),
                  pl.BlockSpec((cs, kw), lambda s: (ch(s), 1)),
                  pl.BlockSpec((cs, GLA_VW), lambda s: (ch(s), 0)),
                  pl.BlockSpec((cs, LANE), lambda s: (ch(s), 0)),
                  pl.BlockSpec((cs, GLA_DKP), lambda s: (ch(s), 0)),
                  pl.BlockSpec((cs, GLA_DKP), lambda s: (ch(s), 0)),
                  pl.BlockSpec((LANE, kw), full2),
                  pl.BlockSpec((1, kw), full2),
                  pl.BlockSpec((GLA_DKP, GLA_DKP), full2),
                  pl.BlockSpec((GLA_HEADS, GLA_DV, GLA_DKP), lambda s: (0, 0, 0))],
        out_specs=[pl.BlockSpec((cs, GLA_VW), lambda s: (ch(s), 0)),
                   pl.BlockSpec((GLA_HEADS, GLA_DV, GLA_DKP), lambda s: (0, 0, 0))],
        out_shape=[jax.ShapeDtypeStruct((l, GLA_VW), F32),
                   jax.ShapeDtypeStruct((GLA_HEADS, GLA_DV, GLA_DKP), F32)],
        scratch_shapes=[pltpu.VMEM((GLA_HEADS, GLA_DV, GLA_DKP), F32)],
        compiler_params=_params(("arbitrary",), 32 << 20),
        name="gla_bwd" if rev else "gla_fwd",
    )(u_qk, u_qk, u_vr, lr, cos, sin, w2p, bgp, perm, s0)
    return o, sfin


def _gla_out_kernel(of_ref, ob_ref, r_ref, g_ref, o_ref):
    o = of_ref[...] + ob_ref[...]
    g = g_ref[...]
    r = r_ref[...].astype(F32)
    for h in range(GLA_HEADS):
        sl = slice(h * GLA_DV, (h + 1) * GLA_DV)
        oh = o[:, sl]
        on = oh * lax.rsqrt(jnp.mean(oh * oh, axis=-1, keepdims=True) + EPS) * g[:, sl]
        o_ref[:, sl] = (on * _silu(r[:, sl])).astype(o_ref.dtype)


def gla_output(o_f, o_b, u_vr, g_gla, tm=256):
    l = o_f.shape[0]
    tm = min(tm, l)
    return pl.pallas_call(
        _gla_out_kernel,
        grid=(l // tm,),
        in_specs=[pl.BlockSpec((tm, GLA_VW), lambda i: (i, 0)),
                  pl.BlockSpec((tm, GLA_VW), lambda i: (i, 0)),
                  pl.BlockSpec((tm, GLA_VW), lambda i: (i, 1)),
                  pl.BlockSpec((1, GLA_VW), lambda i: (0, 0))],
        out_specs=pl.BlockSpec((tm, GLA_VW), lambda i: (i, 0)),
        out_shape=jax.ShapeDtypeStruct((l, GLA_VW), BF16),
        compiler_params=_params(("arbitrary",), 24 << 20),
        name="gla_output",
    )(o_f, o_b, u_vr, g_gla.reshape(1, GLA_VW))


def rope_tables(n_tokens):
    seg = GLA_DK // 2
    half = seg // 2
    inv = ROPE_THETA ** (-jnp.arange(half, dtype=F32) / half)
    pos = jnp.arange(n_tokens)
    ang_r = (pos // GRID_W).astype(F32)[:, None] * inv
    ang_c = (pos % GRID_W).astype(F32)[:, None] * inv
    pad1 = jnp.ones((n_tokens, GLA_DKP - GLA_DK), F32)
    pad0 = jnp.zeros((n_tokens, GLA_DKP - GLA_DK), F32)
    cos = jnp.concatenate([jnp.cos(ang_r)] * 2 + [jnp.cos(ang_c)] * 2 + [pad1], axis=1)
    sin = jnp.concatenate([jnp.sin(ang_r)] * 2 + [jnp.sin(ang_c)] * 2 + [pad0], axis=1)
    return cos, sin


def rope_perm():
    seg = GLA_DK // 2
    half = seg // 2
    p = np.zeros((GLA_DKP, GLA_DKP), np.float32)
    for s0 in (0, seg):
        for j in range(half):
            p[s0 + half + j, s0 + j] = -1.0
            p[s0 + j, s0 + half + j] = 1.0
    return jnp.asarray(p, BF16)


def _pad_heads(w):
    lead = w.shape[:-1]
    w = w.reshape(lead + (GLA_HEADS, GLA_DK))
    w = jnp.pad(w, [(0, 0)] * len(lead) + [(0, 0), (0, GLA_DKP - GLA_DK)])
    return w.reshape(lead + (GLA_HEADS * GLA_DKP,))


MOE_TR = 256


def moe_plan(route):
    t = route.shape[0]
    tr = MOE_TR
    e = route[:, :2].astype(jnp.int32).reshape(-1)
    onehot = (e[:, None] == jnp.arange(MOE_EXPERTS, dtype=jnp.int32)[None]).astype(jnp.int32)
    csum = jnp.cumsum(onehot, axis=0)
    counts = csum[-1]
    padded = ((counts + tr - 1) // tr) * tr
    ends = jnp.cumsum(padded)
    starts = ends - padded
    pos = jnp.sum(onehot * (starts[None] + csum - 1), axis=1)
    npad = 2 * t + MOE_EXPERTS * tr
    ntiles = npad // tr
    tile_e = jnp.minimum(jnp.searchsorted(ends, jnp.arange(ntiles, dtype=jnp.int32) * tr, side="right"),
                         MOE_EXPERTS - 1).astype(jnp.int32)
    order = jnp.argsort(e, stable=True).astype(jnp.int32)
    row = jnp.arange(npad, dtype=jnp.int32)
    row_e = jnp.repeat(tile_e, tr)
    within = row - starts[row_e]
    dense_start = jnp.cumsum(counts) - counts
    valid = within < counts[row_e]
    src = jnp.where(valid, order[jnp.clip(dense_start[row_e] + within, 0, 2 * t - 1)] // 2, 0)
    n_used = (ends[-1] // tr).reshape(1).astype(jnp.int32)
    return src.astype(jnp.int32), tile_e, n_used, pos.astype(jnp.int32)


def _row_gather_start(idx_ref, first, n, src_hbm, dst, sem):
    def body(j, carry):
        pltpu.make_async_copy(src_hbm.at[pl.ds(idx_ref[first + j], 1)], dst.at[pl.ds(j, 1)], sem).start()
        return carry
    lax.fori_loop(0, n, body, 0, unroll=8)


def _row_gather_wait(n, src_hbm, dst, sem):
    pltpu.make_async_copy(src_hbm.at[pl.ds(0, n)], dst, sem).wait()


def _moe_ffn_kernel(src_ref, te_ref, nu_ref, x_hbm, wgu_ref, wdn_ref, o_ref, buf, sem):
    i = pl.program_id(0)
    n_used = nu_ref[0]
    tr = MOE_TR
    slot = lax.rem(i, 2)

    @pl.when(i == 0)
    def _():
        _row_gather_start(src_ref, 0, tr, x_hbm, buf.at[0], sem.at[0])

    @pl.when(i < n_used)
    def _():
        @pl.when(i + 1 < n_used)
        def _():
            _row_gather_start(src_ref, (i + 1) * tr, tr, x_hbm, buf.at[1 - slot], sem.at[1 - slot])

        _row_gather_wait(tr, x_hbm, buf.at[slot], sem.at[slot])
        x = buf[slot].astype(BF16)
        gu = _dot(x, wgu_ref[...].astype(BF16))
        hid = _silu(gu[:, :MOE_HIDDEN]) * gu[:, MOE_HIDDEN:]
        o_ref[...] = _dot(hid.astype(BF16), wdn_ref[...].astype(BF16))

    @pl.when(i >= n_used)
    def _():
        o_ref[...] = jnp.zeros(o_ref.shape, o_ref.dtype)


def moe_ffn(xn, w_gu, w_dn, l, src, tile_e, n_used):
    t, d = xn.shape
    tr = MOE_TR
    npad = src.shape[0]
    f = MOE_HIDDEN
    grid_spec = pltpu.PrefetchScalarGridSpec(
        num_scalar_prefetch=3,
        grid=(npad // tr,),
        in_specs=[pl.BlockSpec(memory_space=pl.ANY),
                  pl.BlockSpec((None, None, d, 2 * f), lambda i, s, te, nu: (l, te[i], 0, 0)),
                  pl.BlockSpec((None, None, f, d), lambda i, s, te, nu: (l, te[i], 0, 0))],
        out_specs=pl.BlockSpec((tr, d), lambda i, s, te, nu: (i, 0)),
        scratch_shapes=[pltpu.VMEM((2, tr, d), F32), pltpu.SemaphoreType.DMA((2,))],
    )
    return pl.pallas_call(
        _moe_ffn_kernel,
        grid_spec=grid_spec,
        out_shape=jax.ShapeDtypeStruct((npad, d), F32),
        compiler_params=_params(("arbitrary",), 4 * tr * d * 4 + 2 * 3 * d * f * 4 + 6 * tr * d * 4),
        name="moe_ffn",
    )(src, tile_e, n_used, xn, w_gu, w_dn)


def _moe_combine_kernel(pos_ref, y_hbm, x_ref, route_ref, gate_ref, o_ref, buf, sem):
    i = pl.program_id(0)
    n = pl.num_programs(0)
    tm = x_ref.shape[0]
    slot = lax.rem(i, 2)

    def start(tile, s):
        _row_gather_start(pos_ref, 2 * tm * tile, 2 * tm, y_hbm, buf.at[s], sem.at[s])

    @pl.when(i == 0)
    def _():
        start(0, 0)

    @pl.when(i + 1 < n)
    def _():
        start(i + 1, 1 - slot)

    _row_gather_wait(2 * tm, y_hbm, buf.at[slot], sem.at[slot])
    route = route_ref[...]
    lane = lax.broadcasted_iota(jnp.int32, route.shape, 1)
    w1 = _lane_pick(route, lane, 2)
    w2 = _lane_pick(route, lane, 3)
    cur = buf.at[slot]
    o_ref[...] = x_ref[...] + gate_ref[...] * (w1 * cur[:tm, :] + w2 * cur[tm:, :])


def moe_combine(y, x, route, gate, pos, tm=128):
    t, d = x.shape
    pos = pos.reshape(t // tm, tm, 2).transpose(0, 2, 1).reshape(-1)
    grid_spec = pltpu.PrefetchScalarGridSpec(
        num_scalar_prefetch=1,
        grid=(t // tm,),
        in_specs=[pl.BlockSpec(memory_space=pl.ANY),
                  pl.BlockSpec((tm, d), lambda i, p: (i, 0)),
                  pl.BlockSpec((tm, LANE), lambda i, p: (i, 0)),
                  pl.BlockSpec((1, d), lambda i, p: (0, 0))],
        out_specs=pl.BlockSpec((tm, d), lambda i, p: (i, 0)),
        scratch_shapes=[pltpu.VMEM((2, 2 * tm, d), F32), pltpu.SemaphoreType.DMA((2,))],
    )
    return pl.pallas_call(
        _moe_combine_kernel,
        grid_spec=grid_spec,
        out_shape=jax.ShapeDtypeStruct((t, d), F32),
        compiler_params=_params(("arbitrary",), 4 * tm * d * 4 + 8 * tm * d * 4),
        name="moe_combine",
    )(pos, y, x, route, gate.reshape(1, d))


def _layer(l, x, xc, mod, last, consts, w):
    (g_mix, w_in, rpb, w_fn, w_g2, b_g, g_gla, w_out, g_ffn, w_rg, b_rg, w_re, b_re, w_gu, w_dn) = w
    cos, sin, cos_c, sin_c, perm, na_tab = consts
    d = D_MODEL
    s_len = x.shape[0]
    m_x = mod[l, 0].reshape(N_MOD, d)
    m_c = mod[l, 1].reshape(N_MOD, d)

    w_lr = jnp.pad(w_in[l, :, OFF_LR:OFF_LR + 2 * GLA_GATE_RANK], ((0, 0), (0, LANE - 2 * GLA_GATE_RANK)))
    xn, lr = norm_proj(x, g_mix[l], m_x[0], m_x[1], w_lr)
    xcn, lr_c = norm_proj(xc, g_mix[l], m_c[0], m_c[1], w_lr)

    w_qk = jnp.concatenate([_pad_heads(w_in[l, :, OFF_GQK:OFF_GQK + GLA_KW]),
                            _pad_heads(w_in[l, :, OFF_GQK + GLA_KW:OFF_GVR])], axis=1)
    proj = []
    for a in (xn, xcn):
        u_na = matmul(a, w_in, (l,), OFF_NA, 3 * NA_WIDTH, 512, 1024, BF16, name="in_na")
        u_fn = matmul(a, w_in, (l,), OFF_FN, FN_WIDTH, 512, 1024, BF16, name="in_fn")
        u_qk = matmul(a, w_qk, (), 0, 2 * GLA_HEADS * GLA_DKP, 512, 1024, BF16, name="in_gqk")
        u_vr = matmul(a, w_in, (l,), OFF_GVR, 2 * GLA_VW, 512, 1024, BF16, name="in_gvr")
        proj.append((u_na, u_fn, u_qk, u_vr))
    (u_na, u_fn, u_qk, u_vr), (c_na, c_fn, c_qk, c_vr) = proj

    w2p = [jnp.pad(_pad_heads(w_g2[l, dr]), ((dr * GLA_GATE_RANK, LANE - (dr + 1) * GLA_GATE_RANK), (0, 0)))
           for dr in (0, 1)]
    bgp = [_pad_heads(b_g[l, dr]).reshape(1, -1) for dr in (0, 1)]
    zero = jnp.zeros((GLA_HEADS, GLA_DV, GLA_DKP), F32)
    outs_c, outs_x = [], []
    for dr in (0, 1):
        o_c, s_c = gla_scan(c_qk, c_vr, lr_c, cos_c, sin_c, w2p[dr], bgp[dr], perm, zero, rev=bool(dr))
        o_x, _ = gla_scan(u_qk, u_vr, lr, cos, sin, w2p[dr], bgp[dr], perm, s_c, rev=bool(dr))
        outs_c.append(o_c)
        outs_x.append(o_x)

    mix = jnp.concatenate([neighbourhood_attention(u_na, c_na, na_tab[l]),
                           fourier_latent(u_fn, w_fn[l]),
                           gla_output(outs_x[0], outs_x[1], u_vr, g_gla[l])], axis=1)
    x = matmul(mix, w_out, (l,), 0, d, 512, 1024, F32, res=x, gate=m_x[2], name="out_proj")

    w_route = jnp.pad(jnp.concatenate([w_re[l], w_rg[l]], axis=1),
                      ((0, 0), (0, LANE - MOE_EXPERTS - MOE_GROUPS)))
    b_route = jnp.pad(jnp.concatenate([b_re[l], b_rg[l]]), (0, LANE - MOE_EXPERTS - MOE_GROUPS)).reshape(1, LANE)
    streams = [(x, m_x)]
    if not last:
        mix_c = jnp.concatenate([full_attention(c_na),
                                 fourier_context(c_fn, w_fn[l]),
                                 gla_output(outs_c[0], outs_c[1], c_vr, g_gla[l])], axis=1)
        xc = matmul(mix_c, w_out, (l,), 0, d, 512, 1024, F32, res=xc, gate=m_c[2], name="out_proj_ctx")
        streams.append((xc, m_c))

    normed = [norm_proj(y, g_ffn[l], m[3], m[4], w_route, b_route) for y, m in streams]
    yn = jnp.concatenate([n_[0] for n_ in normed], axis=0)
    route = jnp.concatenate([n_[1] for n_ in normed], axis=0)
    src, tile_e, n_used, pos = moe_plan(route)
    y_exp = moe_ffn(yn, w_gu, w_dn, l, src, tile_e, n_used)
    outs, row0 = [], 0
    for (y, m), n_ in zip(streams, normed):
        t = y.shape[0]
        outs.append(moe_combine(y_exp, y, n_[1], m[5], pos[2 * row0:2 * (row0 + t)]))
        row0 += t
    return outs[0], (outs[1] if not last else xc)


def kernel(x, c, ctx, c_ctx, w_ada, b_ada, g_mix, w_in, rpb, w_fn, w_g2, b_g, g_gla, w_out,
           g_ffn, w_rg, b_rg, w_re, b_re, w_gu, w_dn, g_final):
    assert x.shape[0] == 1 and c.shape[0] == 1
    d = D_MODEL
    s_len = x.shape[1]
    c_len = ctx.shape[1]
    cc = jnp.concatenate([c, c_ctx[None], jnp.zeros((6, d), F32)], axis=0)
    mod = ada_mod(cc, w_ada, b_ada)
    cos, sin = rope_tables(s_len)
    cos_c = jnp.ones((c_len, GLA_DKP), F32)
    sin_c = jnp.zeros((c_len, GLA_DKP), F32)
    consts = (cos, sin, cos_c, sin_c, rope_perm(), na_tables(rpb, s_len // GRID_W))
    w = (g_mix, w_in, rpb, w_fn, w_g2, b_g, g_gla, w_out, g_ffn, w_rg, b_rg, w_re, b_re, w_gu, w_dn)
    xs, xc = x[0], ctx[0]
    for l in range(DEPTH):
        xs, xc = _layer(l, xs, xc, mod, l == DEPTH - 1, consts, w)
    return final_norm(xs, g_final)[None]
```

```python
import functools
import math

import numpy as np
import jax
import jax.numpy as jnp
from jax import lax
from jax.experimental import pallas as pl
from jax.experimental.pallas import tpu as pltpu

F32 = jnp.float32
BF16 = jnp.bfloat16

D_MODEL = 4096
DEPTH = 4
GRID_W = 64
EPS = 1e-6
N_MOD = 6

NA_HEADS = 12
NA_HEAD_DIM = 128
NA_WIN_H = 8
NA_WIN_W = 16
NA_WIDTH = NA_HEADS * NA_HEAD_DIM

FN_GROUPS = 8
FN_GROUP_DIM = 128
FN_WIDTH = FN_GROUPS * FN_GROUP_DIM

GLA_HEADS = 4
GLA_DK = 192
GLA_DKP = 256
GLA_DV = 384
GLA_KW = GLA_HEADS * GLA_DK
GLA_VW = GLA_HEADS * GLA_DV
GLA_GATE_RANK = 16
GLA_GATE_TEMP = 16.0
GLA_CHUNK = 64
GLA_SUB = 8
ROPE_THETA = 10000.0

MOE_GROUPS = 4
MOE_EXPERTS_PER_GROUP = 8
MOE_EXPERTS = MOE_GROUPS * MOE_EXPERTS_PER_GROUP
MOE_HIDDEN = 192

OFF_NA = 0
OFF_FN = 3 * NA_WIDTH
OFF_GQK = OFF_FN + FN_WIDTH
OFF_GVR = OFF_GQK + 2 * GLA_KW
OFF_LR = OFF_GVR + 2 * GLA_VW

LANE = 128
NEG = -1e30
V7X_VMEM_BUDGET = 56 * 1024 * 1024


def _params(sem, vmem_bytes):
    return pltpu.CompilerParams(dimension_semantics=sem,
                                vmem_limit_bytes=int(min(max(vmem_bytes, 16 << 20), V7X_VMEM_BUDGET)))


def _dot(a, b):
    return jnp.dot(a, b, preferred_element_type=F32)


def _dot_nt(a, b):
    return lax.dot_general(a, b, (((1,), (1,)), ((), ())), preferred_element_type=F32)


def _dot_tn(a, b):
    return lax.dot_general(a, b, (((0,), (0,)), ((), ())), preferred_element_type=F32)


def _silu(x):
    return x * (1.0 / (1.0 + jnp.exp(-x)))


def _ada_kernel(c_ref, w_ref, b_ref, o_ref):
    s = _silu(c_ref[...]).astype(BF16)
    o_ref[0] = _dot(s, w_ref[0].astype(BF16)) + b_ref[0]


def ada_mod(cc, w_ada, b_ada, tn=512):
    n, d, nd = w_ada.shape
    return pl.pallas_call(
        _ada_kernel,
        grid=(n, nd // tn),
        in_specs=[pl.BlockSpec((8, d), lambda l, j: (0, 0)),
                  pl.BlockSpec((1, d, tn), lambda l, j: (l, 0, j)),
                  pl.BlockSpec((1, 1, tn), lambda l, j: (l, 0, j))],
        out_specs=pl.BlockSpec((1, 8, tn), lambda l, j: (l, 0, j)),
        out_shape=jax.ShapeDtypeStruct((n, 8, nd), F32),
        compiler_params=_params(("arbitrary", "arbitrary"), 3 * d * tn * 4 + (4 << 20)),
        name="ada_mod",
    )(cc, w_ada, b_ada.reshape(n, 1, nd))


def _normed(x_ref, g_ref, sh_ref, sc_ref):
    x = x_ref[...]
    y = x * lax.rsqrt(jnp.mean(x * x, axis=-1, keepdims=True) + EPS)
    return (y * g_ref[...]) * (1.0 + sc_ref[...]) + sh_ref[...]


def _norm_proj_kernel(x_ref, g_ref, sh_ref, sc_ref, w_ref, xn_ref, p_ref):
    hb = _normed(x_ref, g_ref, sh_ref, sc_ref).astype(BF16)
    xn_ref[...] = hb
    p_ref[...] = _dot(hb, w_ref[...].astype(BF16))


def _lane_pick(v, lane, idx):
    return jnp.sum(jnp.where(lane == idx, v, 0.0), axis=-1, keepdims=True)


def _norm_route_kernel(x_ref, g_ref, sh_ref, sc_ref, w_ref, b_ref, xn_ref, route_ref):
    hb = _normed(x_ref, g_ref, sh_ref, sc_ref).astype(BF16)
    hf = hb.astype(F32)
    half = hf.shape[1] // 2
    lo = lax.bitcast_convert_type(hf[:, :half], jnp.uint32)
    hi = lax.bitcast_convert_type(hf[:, half:], jnp.uint32)
    xn_ref[...] = hi | (lo >> 16)
    logits = _dot(hb, w_ref[...].astype(BF16)) + b_ref[...]
    lane = lax.broadcasted_iota(jnp.int32, logits.shape, 1).astype(F32)
    far = float(4 * LANE)
    is_g = (lane >= MOE_EXPERTS) & (lane < MOE_EXPERTS + MOE_GROUPS)
    gl = jnp.where(is_g, logits, NEG)
    gmax = jnp.max(gl, axis=-1, keepdims=True)
    g_top = jnp.min(jnp.where(gl == gmax, lane, far), axis=-1, keepdims=True) - MOE_EXPERTS
    g_w = 1.0 / jnp.sum(jnp.where(is_g, jnp.exp(gl - gmax), 0.0), axis=-1, keepdims=True)
    in_grp = (lane >= g_top * MOE_EXPERTS_PER_GROUP) & (lane < (g_top + 1) * MOE_EXPERTS_PER_GROUP)
    el = jnp.where(in_grp, logits, NEG)
    v1 = jnp.max(el, axis=-1, keepdims=True)
    i1 = jnp.min(jnp.where(el == v1, lane, far), axis=-1, keepdims=True)
    el2 = jnp.where(lane == i1, NEG, el)
    v2 = jnp.max(el2, axis=-1, keepdims=True)
    i2 = jnp.min(jnp.where(el2 == v2, lane, far), axis=-1, keepdims=True)
    e2 = jnp.exp(v2 - v1)
    w1 = g_w / (1.0 + e2)
    w2 = g_w * e2 / (1.0 + e2)
    route_ref[...] = (jnp.where(lane == 0.0, i1, 0.0) + jnp.where(lane == 1.0, i2, 0.0)
                      + jnp.where(lane == 2.0, w1, 0.0) + jnp.where(lane == 3.0, w2, 0.0))


def _norm_route_streams_kernel(*refs, blocks):
    ns = len(blocks)
    x_refs, g_ref = refs[:ns], refs[ns]
    mods = refs[ns + 1:ns + 1 + 2 * ns]
    w_ref, b_ref, xn_ref, route_ref = refs[ns + 1 + 2 * ns:]
    i = pl.program_id(0)
    first = 0
    for s_, nb in enumerate(blocks):
        @pl.when((i >= first) & (i < first + nb))
        def _(s_=s_):
            _norm_route_kernel(x_refs[s_], g_ref, mods[2 * s_], mods[2 * s_ + 1], w_ref, b_ref, xn_ref, route_ref)
        first += nb


def norm_proj(x, g, sh, sc, w_small, tm=256):
    t, d = x.shape
    row = lambda i: (0, 0)
    vec = pl.BlockSpec((1, d), row)
    return pl.pallas_call(
        _norm_proj_kernel,
        grid=(t // tm,),
        in_specs=[pl.BlockSpec((tm, d), lambda i: (i, 0)), vec, vec, vec, pl.BlockSpec((d, LANE), row)],
        out_specs=[pl.BlockSpec((tm, d), lambda i: (i, 0)), pl.BlockSpec((tm, LANE), lambda i: (i, 0))],
        out_shape=[jax.ShapeDtypeStruct((t, d), BF16), jax.ShapeDtypeStruct((t, LANE), F32)],
        compiler_params=_params(("arbitrary",), 8 * tm * d * 4 + 4 * d * LANE * 4),
        name="norm_proj",
    )(x, g.reshape(1, d), sh.reshape(1, d), sc.reshape(1, d), w_small)


def norm_route(streams, g, w_small, b_small, tm=256):
    d = streams[0][0].shape[1]
    blocks = tuple(x.shape[0] // tm for x, _, _ in streams)
    row = lambda i: (0, 0)
    vec = pl.BlockSpec((1, d), row)
    ins, args, first = [], [], 0
    for (x, _, _), nb in zip(streams, blocks):
        ins.append(pl.BlockSpec((tm, d), lambda i, first=first, nb=nb: (jnp.clip(i - first, 0, nb - 1), 0)))
        args.append(x)
        first += nb
    ins.append(vec)
    args.append(g.reshape(1, d))
    for _, sh, sc in streams:
        ins += [vec, vec]
        args += [sh.reshape(1, d), sc.reshape(1, d)]
    ins += [pl.BlockSpec((d, LANE), row), pl.BlockSpec((1, LANE), row)]
    args += [w_small, b_small]
    rows = sum(blocks) * tm
    return pl.pallas_call(
        functools.partial(_norm_route_streams_kernel, blocks=blocks),
        grid=(sum(blocks),),
        in_specs=ins,
        out_specs=[pl.BlockSpec((tm, d // 2), lambda i: (i, 0)), pl.BlockSpec((tm, LANE), lambda i: (i, 0))],
        out_shape=[jax.ShapeDtypeStruct((rows, d // 2), jnp.uint32), jax.ShapeDtypeStruct((rows, LANE), F32)],
        compiler_params=_params(("arbitrary",), (4 + 4 * len(streams)) * tm * d * 4 + 4 * d * LANE * 4),
        name="norm_route",
    )(*args)


def _final_norm_kernel(x_ref, g_ref, o_ref):
    x = x_ref[...]
    o_ref[...] = x * lax.rsqrt(jnp.mean(x * x, axis=-1, keepdims=True) + EPS) * g_ref[...]


def final_norm(x, g, tm=256):
    t, d = x.shape
    return pl.pallas_call(
        _final_norm_kernel,
        grid=(t // tm,),
        in_specs=[pl.BlockSpec((tm, d), lambda i: (i, 0)), pl.BlockSpec((1, d), lambda i: (0, 0))],
        out_specs=pl.BlockSpec((tm, d), lambda i: (i, 0)),
        out_shape=jax.ShapeDtypeStruct((t, d), F32),
        compiler_params=_params(("arbitrary",), 6 * tm * d * 4),
        name="final_norm",
    )(x, g.reshape(1, d))


def _mm_kernel(a_ref, w_ref, o_ref):
    o_ref[...] = _dot(a_ref[...], w_ref[...].astype(BF16)).astype(o_ref.dtype)


def _mm_res_kernel(a_ref, w_ref, r_ref, g_ref, o_ref):
    o_ref[...] = r_ref[...] + g_ref[...] * _dot(a_ref[...], w_ref[...].astype(BF16))


def _mm_parts_res_kernel(*refs, widths):
    n = len(widths)
    w_ref, r_ref, g_ref, o_ref = refs[n:]
    acc, k0 = None, 0
    for a_ref, wd in zip(refs[:n], widths):
        part = _dot(a_ref[...], w_ref[k0:k0 + wd, :].astype(BF16))
        acc = part if acc is None else acc + part
        k0 += wd
    o_ref[...] = r_ref[...] + g_ref[...] * acc


def _mm_head_pad_kernel(a_ref, w0_ref, w1_ref, w2_ref, o_ref):
    a = a_ref[...]
    acc = jnp.concatenate([_dot(a, w_ref[...].astype(BF16)) for w_ref in (w0_ref, w1_ref, w2_ref)], axis=1)
    pad = jnp.zeros((a.shape[0], GLA_DKP - GLA_DK), o_ref.dtype)
    for h in range(GLA_HEADS):
        o_ref[:, h * GLA_DKP:h * GLA_DKP + GLA_DK] = acc[:, h * GLA_DK:(h + 1) * GLA_DK].astype(o_ref.dtype)
        o_ref[:, h * GLA_DKP + GLA_DK:(h + 1) * GLA_DKP] = pad


def gla_qk_proj(a, w_in, l, tm=1024):
    t, k = a.shape
    tm = min(tm, t)
    tn = GLA_KW // 3
    j0 = OFF_GQK // tn
    w_spec = lambda p: pl.BlockSpec((None, k, tn), lambda i, j: (l, 0, j0 + 3 * j + p))
    kwp = GLA_HEADS * GLA_DKP
    return pl.pallas_call(
        _mm_head_pad_kernel,
        grid=(t // tm, 2),
        in_specs=[pl.BlockSpec((tm, k), lambda i, j: (i, 0)), w_spec(0), w_spec(1), w_spec(2)],
        out_specs=pl.BlockSpec((tm, kwp), lambda i, j: (i, j)),
        out_shape=jax.ShapeDtypeStruct((t, 2 * kwp), BF16),
        compiler_params=_params(("arbitrary", "arbitrary"), 2 * tm * k * 2 + 7 * k * tn * 4 + 8 * tm * kwp * 4),
        name="in_gqk",
    )(a, w_in, w_in, w_in)


def matmul(a, w, w_lead, col0, ncols, tn, tm, out_dtype, res=None, gate=None, name="mm"):
    parts = a if isinstance(a, (tuple, list)) else (a,)
    t = parts[0].shape[0]
    k = sum(p.shape[1] for p in parts)
    tm = min(tm, t)
    nlead = len(w_lead)
    j0 = col0 // tn
    w_spec = pl.BlockSpec((None,) * nlead + (k, tn), lambda i, j: tuple(w_lead) + (0, j + j0))
    ins = [pl.BlockSpec((tm, p.shape[1]), lambda i, j: (i, 0)) for p in parts] + [w_spec]
    args = list(parts) + [w]
    wbytes = jnp.dtype(w.dtype).itemsize
    vmem = 2 * tm * k * 2 + 3 * k * tn * wbytes + 6 * tm * tn * 4
    if res is None:
        body = _mm_kernel
    else:
        body = _mm_res_kernel if len(parts) == 1 else functools.partial(
            _mm_parts_res_kernel, widths=tuple(p.shape[1] for p in parts))
        ins += [pl.BlockSpec((tm, tn), lambda i, j: (i, j)), pl.BlockSpec((1, tn), lambda i, j: (0, j))]
        args += [res, gate.reshape(1, ncols)]
    return pl.pallas_call(
        body,
        grid=(t // tm, ncols // tn),
        in_specs=ins,
        out_specs=pl.BlockSpec((tm, tn), lambda i, j: (i, j)),
        out_shape=jax.ShapeDtypeStruct((t, ncols), out_dtype),
        compiler_params=_params(("arbitrary", "arbitrary"), vmem),
        name=name,
    )(*args)


NA_QROWS = 4
NA_SLAB = 12
NA_HEADS_PER_STEP = 3


def na_tables(rpb, rows):
    nblk = rows // NA_QROWS
    n_, h_ = rpb.shape[:2]
    qc = np.arange(GRID_W)[:, None]
    kc = np.arange(GRID_W)[None, :]
    c0 = np.clip(qc - NA_WIN_W // 2, 0, GRID_W - NA_WIN_W)
    ok_c = (kc >= c0) & (kc < c0 + NA_WIN_W)
    a = np.arange(NA_QROWS)[:, None]
    b = np.arange(NA_SLAB)[None, :]
    dr_l, ok_l = [], []
    for i in (0, 1, nblk - 1):
        base = int(np.clip(i * NA_QROWS - NA_WIN_H // 2, 0, rows - NA_SLAB))
        r = i * NA_QROWS + a
        r0 = np.clip(r - NA_WIN_H // 2, 0, rows - NA_WIN_H)
        krow = base + b
        ok_l.append((krow >= r0) & (krow < r0 + NA_WIN_H))
        dr_l.append(np.clip(krow - r + NA_WIN_H - 1, 0, 2 * NA_WIN_H - 2))
    dr = np.stack(dr_l)
    ok = np.stack(ok_l)[:, :, None, :, None] & ok_c[None, None, :, None, :]
    ok = jnp.asarray(ok.reshape(3, NA_QROWS * GRID_W, NA_SLAB * GRID_W))
    lo = GRID_W - NA_WIN_W
    padded = jnp.pad(rpb.astype(F32), ((0, 0), (0, 0), (0, 0), (lo, lo)))
    by_col = jnp.stack([padded[..., GRID_W - 1 - q:2 * GRID_W - 1 - q] for q in range(GRID_W)], axis=3)
    variants = []
    for v in range(3):
        rows_ = [jnp.concatenate([by_col[:, :, int(dr[v, a_, b_])] for b_ in range(NA_SLAB)], axis=-1)
                 for a_ in range(NA_QROWS)]
        variants.append(jnp.concatenate(rows_, axis=2))
    bias = jnp.stack(variants, axis=1)
    return jnp.where(ok[None, :, None], bias, NEG)


def _na_kernel(q_ref, k_ref, v_ref, kc_ref, vc_ref, t_ref, o_ref, *, rows):
    i = pl.program_id(1)
    base = jnp.clip(i * NA_QROWS - NA_WIN_H // 2, 0, rows - NA_SLAB) * GRID_W
    base = pl.multiple_of(base, NA_QROWS * GRID_W)
    n_keys = NA_SLAB * GRID_W
    scale = NA_HEAD_DIM ** -0.5
    for hh in range(NA_HEADS_PER_STEP):
        cols = slice(hh * NA_HEAD_DIM, (hh + 1) * NA_HEAD_DIM)
        q = q_ref[:, cols]
        k = k_ref[pl.ds(base, n_keys), cols]
        v = v_ref[pl.ds(base, n_keys), cols]
        s = _dot_nt(q, k) * scale + t_ref[hh]
        sc = _dot_nt(q, kc_ref[:, cols]) * scale
        m = jnp.maximum(jnp.max(s, axis=-1, keepdims=True), jnp.max(sc, axis=-1, keepdims=True))
        p = jnp.exp(s - m)
        pc = jnp.exp(sc - m)
        den = jnp.sum(p, axis=-1, keepdims=True) + jnp.sum(pc, axis=-1, keepdims=True)
        o = _dot(p.astype(BF16), v) + _dot(pc.astype(BF16), vc_ref[:, cols])
        o_ref[:, cols] = (o / den).astype(o_ref.dtype)


def neighbourhood_attention(u, uc, tables, l):
    s = u.shape[0]
    c = uc.shape[0]
    rows = s // GRID_W
    nblk = rows // NA_QROWS
    tq = NA_QROWS * GRID_W
    hps = NA_HEADS_PER_STEP
    ng = NA_HEADS // hps
    hw = hps * NA_HEAD_DIM

    def variant(i):
        return jnp.where(i == 0, 0, jnp.where(i == nblk - 1, 2, 1))

    return pl.pallas_call(
        functools.partial(_na_kernel, rows=rows),
        grid=(ng, nblk),
        in_specs=[pl.BlockSpec((tq, hw), lambda h, i: (i, h)),
                  pl.BlockSpec((s, hw), lambda h, i: (0, ng + h)),
                  pl.BlockSpec((s, hw), lambda h, i: (0, 2 * ng + h)),
                  pl.BlockSpec((c, hw), lambda h, i: (0, ng + h)),
                  pl.BlockSpec((c, hw), lambda h, i: (0, 2 * ng + h)),
                  pl.BlockSpec((None, None, hps, tq, NA_SLAB * GRID_W), lambda h, i: (l, variant(i), h, 0, 0))],
        out_specs=pl.BlockSpec((tq, hw), lambda h, i: (i, h)),
        out_shape=jax.ShapeDtypeStruct((s, NA_WIDTH), BF16),
        compiler_params=_params(("arbitrary", "arbitrary"),
                                4 * s * hw * 2 + 2 * hps * tq * NA_SLAB * GRID_W * 4 + (12 << 20)),
        name="na_latent",
    )(u, u, u, uc, uc, tables)


def _full_attn_kernel(q_ref, k_ref, v_ref, o_ref):
    s = _dot_nt(q_ref[...], k_ref[...]) * NA_HEAD_DIM ** -0.5
    p = jnp.exp(s - jnp.max(s, axis=-1, keepdims=True))
    den = jnp.sum(p, axis=-1, keepdims=True)
    o_ref[...] = (_dot(p.astype(BF16), v_ref[...]) / den).astype(o_ref.dtype)


def full_attention(uc):
    c = uc.shape[0]
    h_ = NA_HEADS
    hd = NA_HEAD_DIM
    return pl.pallas_call(
        _full_attn_kernel,
        grid=(h_,),
        in_specs=[pl.BlockSpec((c, hd), lambda h: (0, h)),
                  pl.BlockSpec((c, hd), lambda h: (0, h_ + h)),
                  pl.BlockSpec((c, hd), lambda h: (0, 2 * h_ + h))],
        out_specs=pl.BlockSpec((c, hd), lambda h: (0, h)),
        out_shape=jax.ShapeDtypeStruct((c, NA_WIDTH), BF16),
        compiler_params=_params(("arbitrary",), 16 << 20),
        name="na_context",
    )(uc, uc, uc)


def _dft_cs(n):
    idx = np.arange(n)
    ang = 2.0 * np.pi * ((idx[:, None] * idx[None, :]) % n) / n
    return np.cos(ang), np.sin(ang)


FN_K1B = 8


def _fn1_kernel(w_ref, x_ref, tr_ref, ti_ref, z_ref, *, r):
    y = _dot(w_ref[...], x_ref[...])
    yr, yi = y[:r], y[r:]
    tr, ti = tr_ref[...], ti_ref[...]
    z_ref[:r, :] = (yr * tr - yi * ti).astype(z_ref.dtype)
    z_ref[r:, :] = (yr * ti + yi * tr).astype(z_ref.dtype)


def _fn2_kernel(zr_ref, zi_ref, kc_ref, ks_ref, c_ref, s_ref, w_ref, o_ref, *, scale):
    zr, zi = zr_ref[...], zi_ref[...]
    kc, ks = kc_ref[...], ks_ref[...]
    xr = (_dot(kc, zr) + _dot(ks, zi)).astype(BF16)
    xi = (_dot(kc, zi) - _dot(ks, zr)).astype(BF16)
    cw = GRID_W
    for g in range(FN_GROUPS):
        lo = g * FN_GROUP_DIM
        f = _dot(xr[:, lo:lo + FN_GROUP_DIM], c_ref[...]) + _dot(xi[:, lo:lo + FN_GROUP_DIM], s_ref[...])
        y = _dot((f * scale).astype(BF16), w_ref[g].astype(BF16))
        for kk in range(FN_K1B):
            o_ref[:, kk * FN_WIDTH + lo:kk * FN_WIDTH + lo + FN_GROUP_DIM] = (
                y[kk * cw:(kk + 1) * cw].astype(o_ref.dtype))


def fourier_latent(u, w_fn):
    l = u.shape[0]
    cw = GRID_W
    r = l // cw
    c_r, s_r = _dft_cs(r)
    w1 = jnp.asarray(np.concatenate([c_r, -s_r], axis=0), BF16)
    ang = 2.0 * np.pi * (np.arange(cw)[:, None] * np.arange(r)[None, :]) / l
    tr = jnp.asarray(np.cos(ang)[:, :, None], F32)
    ti = jnp.asarray(-np.sin(ang)[:, :, None], F32)
    z = pl.pallas_call(
        functools.partial(_fn1_kernel, r=r),
        grid=(cw,),
        in_specs=[pl.BlockSpec((2 * r, r), lambda j: (0, 0)),
                  pl.BlockSpec((r, FN_WIDTH), lambda j: (0, j)),
                  pl.BlockSpec((None, r, 1), lambda j: (j, 0, 0)),
                  pl.BlockSpec((None, r, 1), lambda j: (j, 0, 0))],
        out_specs=pl.BlockSpec((2 * r, FN_WIDTH), lambda j: (0, j)),
        out_shape=jax.ShapeDtypeStruct((2 * r, cw * FN_WIDTH), BF16),
        compiler_params=_params(("arbitrary",), 24 << 20),
        name="fourier_stage1",
    )(w1, u.reshape(r, cw * FN_WIDTH), tr, ti)
    z2 = z.reshape(2 * r * cw, FN_WIDTH)
    c_w, s_w = _dft_cs(cw)
    eye = np.eye(FN_K1B)
    kc = jnp.asarray(np.kron(eye, c_w), BF16)
    ks = jnp.asarray(np.kron(eye, s_w), BF16)
    c_c, s_c = _dft_cs(FN_GROUP_DIM)
    nb = r // FN_K1B
    tb = FN_K1B * cw
    out = pl.pallas_call(
        functools.partial(_fn2_kernel, scale=float((l * FN_GROUP_DIM) ** -0.5)),
        grid=(nb,),
        in_specs=[pl.BlockSpec((tb, FN_WIDTH), lambda b: (b, 0)),
                  pl.BlockSpec((tb, FN_WIDTH), lambda b: (nb + b, 0)),
                  pl.BlockSpec((tb, tb), lambda b: (0, 0)),
                  pl.BlockSpec((tb, tb), lambda b: (0, 0)),
                  pl.BlockSpec((FN_GROUP_DIM, FN_GROUP_DIM), lambda b: (0, 0)),
                  pl.BlockSpec((FN_GROUP_DIM, FN_GROUP_DIM), lambda b: (0, 0)),
                  pl.BlockSpec((FN_GROUPS, FN_GROUP_DIM, FN_GROUP_DIM), lambda b: (0, 0, 0))],
        out_specs=pl.BlockSpec((cw, FN_K1B * FN_WIDTH), lambda b: (0, b)),
        out_shape=jax.ShapeDtypeStruct((cw, r * FN_WIDTH), BF16),
        compiler_params=_params(("arbitrary",), 24 << 20),
        name="fourier_stage2",
    )(z2, z2, kc, ks, jnp.asarray(c_c, BF16), jnp.asarray(s_c, BF16), w_fn)
    return out.reshape(l, FN_WIDTH)


def _fn_ctx_kernel(u_ref, cl_ref, sl_ref, c_ref, s_ref, w_ref, o_ref, *, scale):
    u = u_ref[...]
    gr = _dot(cl_ref[...], u).astype(BF16)
    gi = (-_dot(sl_ref[...], u)).astype(BF16)
    for g in range(FN_GROUPS):
        lo = g * FN_GROUP_DIM
        f = _dot(gr[:, lo:lo + FN_GROUP_DIM], c_ref[...]) + _dot(gi[:, lo:lo + FN_GROUP_DIM], s_ref[...])
        o_ref[:, lo:lo + FN_GROUP_DIM] = _dot((f * scale).astype(BF16), w_ref[g].astype(BF16)).astype(o_ref.dtype)


def fourier_context(u, w_fn):
    c = u.shape[0]
    c_l, s_l = _dft_cs(c)
    c_c, s_c = _dft_cs(FN_GROUP_DIM)
    return pl.pallas_call(
        functools.partial(_fn_ctx_kernel, scale=float((c * FN_GROUP_DIM) ** -0.5)),
        out_shape=jax.ShapeDtypeStruct((c, FN_WIDTH), BF16),
        compiler_params=_params((), 16 << 20),
        name="fourier_context",
    )(u, jnp.asarray(c_l, BF16), jnp.asarray(s_l, BF16), jnp.asarray(c_c, BF16), jnp.asarray(s_c, BF16), w_fn)


def _split_hi_lo(x):
    hi = x.astype(BF16)
    return hi, (x - hi.astype(F32)).astype(BF16)


def _gla_kernel(q_ref, k_ref, v_ref, lr_ref, cos_ref, sin_ref, w2_ref, bg_ref, perm_ref, s0_ref,
                o_ref, sfin_ref, st_ref, *, rev, nchunks):
    step = pl.program_id(0)

    @pl.when(step == 0)
    def _():
        st_ref[...] = s0_ref[...]

    cs = GLA_CHUNK
    sub = GLA_SUB
    nsub = cs // sub
    z = _dot(lr_ref[...].astype(BF16), w2_ref[...].astype(BF16)) + bg_ref[...]
    la = -(jnp.maximum(-z, 0.0) + jnp.log(1.0 + jnp.exp(-jnp.abs(z)))) * (1.0 / GLA_GATE_TEMP)
    ri = lax.broadcasted_iota(jnp.int32, (cs, cs), 0)
    ci = lax.broadcasted_iota(jnp.int32, (cs, cs), 1)
    tri = jnp.where((ci >= ri) if rev else (ci <= ri), 1.0, 0.0).astype(BF16)
    la_hi, la_lo = _split_hi_lo(la)
    bcum = _dot(tri, la_hi) + _dot(tri, la_lo)
    edge = 0 if rev else cs - 1
    row_id = lax.broadcasted_iota(jnp.int32, (cs, GLA_DKP), 0)
    sub_r = lax.broadcasted_iota(jnp.int32, (sub, GLA_DKP), 0)
    lane_c = lax.broadcasted_iota(jnp.int32, (sub, cs), 1)
    cos, sin = cos_ref[...], sin_ref[...]
    perm = perm_ref[...]
    qscale = GLA_DK ** -0.5
    for h in range(GLA_HEADS):
        ks_ = slice(h * GLA_DKP, (h + 1) * GLA_DKP)
        vs_ = slice(h * GLA_DV, (h + 1) * GLA_DV)
        qb, kb = q_ref[:, ks_], k_ref[:, ks_]
        q = (qb.astype(F32) * cos + _dot(qb, perm) * sin) * qscale
        k = kb.astype(F32) * cos + _dot(kb, perm) * sin
        v = v_ref[:, vs_]
        b = bcum[:, ks_]
        b_edge = b[edge:edge + 1, :]
        st = st_ref[h]
        o = _dot_nt((q * jnp.exp(b)).astype(BF16), st.astype(BF16))
        slabs = []
        for blk in range(nsub):
            lo = blk * sub
            q_i = q[lo:lo + sub]
            b_i = b[lo:lo + sub]
            k_i = k[lo:lo + sub]
            acc = jnp.zeros((sub, cs), F32)
            if rev and blk < nsub - 1:
                ref_row = b[lo + sub:lo + sub + 1, :]
                outside = row_id >= lo + sub
            elif (not rev) and blk > 0:
                ref_row = b[lo - 1:lo, :]
                outside = row_id < lo
            else:
                ref_row = None
            if ref_row is not None:
                qe = q_i * jnp.exp(b_i - ref_row)
                ke = jnp.where(outside, k * jnp.exp(jnp.where(outside, ref_row - b, 0.0)), 0.0)
                acc = acc + _dot_nt(qe.astype(BF16), ke.astype(BF16))
            for j in range(sub):
                keep = (sub_r <= j) if rev else (sub_r >= j)
                d = jnp.exp(jnp.where(keep, b_i - b_i[j:j + 1, :], NEG))
                col = jnp.sum(q_i * d * k_i[j:j + 1, :], axis=-1, keepdims=True)
                acc = acc + jnp.where(lane_c == lo + j, col, 0.0)
            slabs.append(acc)
        attn = jnp.concatenate(slabs, axis=0)
        o = o + _dot(attn.astype(BF16), v)
        o_ref[:, vs_] = o
        kend = (k * jnp.exp(b_edge - b)).astype(BF16)
        st_ref[h] = st * jnp.exp(b_edge) + _dot_tn(v, kend)

    @pl.when(step == nchunks - 1)
    def _():
        sfin_ref[...] = st_ref[...]


def gla_scan(u_qk, u_vr, lr, cos, sin, w2p, bgp, perm, s0, rev):
    l = u_qk.shape[0]
    n = l // GLA_CHUNK
    cs = GLA_CHUNK
    kw = GLA_HEADS * GLA_DKP
    ch = (lambda s: n - 1 - s) if rev else (lambda s: s)
    full2 = lambda s: (0, 0)
    o, sfin = pl.pallas_call(
        functools.partial(_gla_kernel, rev=rev, nchunks=n),
        grid=(n,),
        in_specs=[pl.BlockSpec((cs, kw), lambda s: (ch(s), 0)),
                  pl.BlockSpec((cs, kw), lambda s: (ch(s), 1)),
                  pl.BlockSpec((cs, GLA_VW), lambda s: (ch(s), 0)),
                  pl.BlockSpec((cs, LANE), lambda s: (ch(s), 0)),
                  pl.BlockSpec((cs, GLA_DKP), lambda s: (ch(s), 0)),
                  pl.BlockSpec((cs, GLA_DKP), lambda s: (ch(s), 0)),
                  pl.BlockSpec((LANE, kw), full2),
                  pl.BlockSpec((1, kw), full2),
                  pl.BlockSpec((GLA_DKP, GLA_DKP), full2),
                  pl.BlockSpec((GLA_HEADS, GLA_DV, GLA_DKP), lambda s: (0, 0, 0))],
        out_specs=[pl.BlockSpec((cs, GLA_VW), lambda s: (ch(s), 0)),
                   pl.BlockSpec((GLA_HEADS, GLA_DV, GLA_DKP), lambda s: (0, 0, 0))],
        out_shape=[jax.ShapeDtypeStruct((l, GLA_VW), F32),
                   jax.ShapeDtypeStruct((GLA_HEADS, GLA_DV, GLA_DKP), F32)],
        scratch_shapes=[pltpu.VMEM((GLA_HEADS, GLA_DV, GLA_DKP), F32)],
        compiler_params=_params(("arbitrary",), 32 << 20),
        name="gla_bwd" if rev else "gla_fwd",
    )(u_qk, u_qk, u_vr, lr, cos, sin, w2p, bgp, perm, s0)
    return o, sfin


def _gla_out_kernel(of_ref, ob_ref, r_ref, g_ref, o_ref):
    o = of_ref[...] + ob_ref[...]
    g = g_ref[...]
    r = r_ref[...].astype(F32)
    for h in range(GLA_HEADS):
        sl = slice(h * GLA_DV, (h + 1) * GLA_DV)
        oh = o[:, sl]
        on = oh * lax.rsqrt(jnp.mean(oh * oh, axis=-1, keepdims=True) + EPS) * g[:, sl]
        o_ref[:, sl] = (on * _silu(r[:, sl])).astype(o_ref.dtype)


def gla_output(o_f, o_b, u_vr, g_gla, tm=256):
    l = o_f.shape[0]
    tm = min(tm, l)
    return pl.pallas_call(
        _gla_out_kernel,
        grid=(l // tm,),
        in_specs=[pl.BlockSpec((tm, GLA_VW), lambda i: (i, 0)),
                  pl.BlockSpec((tm, GLA_VW), lambda i: (i, 0)),
                  pl.BlockSpec((tm, GLA_VW), lambda i: (i, 1)),
                  pl.BlockSpec((1, GLA_VW), lambda i: (0, 0))],
        out_specs=pl.BlockSpec((tm, GLA_VW), lambda i: (i, 0)),
        out_shape=jax.ShapeDtypeStruct((l, GLA_VW), BF16),
        compiler_params=_params(("arbitrary",), 24 << 20),
        name="gla_output",
    )(o_f, o_b, u_vr, g_gla.reshape(1, GLA_VW))


def rope_tables(n_tokens):
    seg = GLA_DK // 2
    half = seg // 2
    inv = ROPE_THETA ** (-jnp.arange(half, dtype=F32) / half)
    pos = jnp.arange(n_tokens)
    ang_r = (pos // GRID_W).astype(F32)[:, None] * inv
    ang_c = (pos % GRID_W).astype(F32)[:, None] * inv
    pad1 = jnp.ones((n_tokens, GLA_DKP - GLA_DK), F32)
    pad0 = jnp.zeros((n_tokens, GLA_DKP - GLA_DK), F32)
    cos = jnp.concatenate([jnp.cos(ang_r)] * 2 + [jnp.cos(ang_c)] * 2 + [pad1], axis=1)
    sin = jnp.concatenate([jnp.sin(ang_r)] * 2 + [jnp.sin(ang_c)] * 2 + [pad0], axis=1)
    return cos, sin


def rope_perm():
    seg = GLA_DK // 2
    half = seg // 2
    p = np.zeros((GLA_DKP, GLA_DKP), np.float32)
    for s0 in (0, seg):
        for j in range(half):
            p[s0 + half + j, s0 + j] = -1.0
            p[s0 + j, s0 + half + j] = 1.0
    return jnp.asarray(p, BF16)


def _pad_heads(w):
    lead = w.shape[:-1]
    w = w.reshape(lead + (GLA_HEADS, GLA_DK))
    w = jnp.pad(w, [(0, 0)] * len(lead) + [(0, 0), (0, GLA_DKP - GLA_DK)])
    return w.reshape(lead + (GLA_HEADS * GLA_DKP,))


MOE_TR = 256


def moe_plan(route):
    t = route.shape[0]
    tr = MOE_TR
    e = route[:, :2].astype(jnp.int32).reshape(-1)
    onehot = (e[:, None] == jnp.arange(MOE_EXPERTS, dtype=jnp.int32)[None]).astype(jnp.int32)
    csum = jnp.cumsum(onehot, axis=0)
    counts = csum[-1]
    padded = ((counts + tr - 1) // tr) * tr
    ends = jnp.cumsum(padded)
    starts = ends - padded
    pos = jnp.sum(onehot * (starts[None] + csum - 1), axis=1)
    npad = 2 * t + MOE_EXPERTS * tr
    ntiles = npad // tr
    tile_e = jnp.minimum(jnp.searchsorted(ends, jnp.arange(ntiles, dtype=jnp.int32) * tr, side="right"),
                         MOE_EXPERTS - 1).astype(jnp.int32)
    order = jnp.argsort(e, stable=True).astype(jnp.int32)
    row = jnp.arange(npad, dtype=jnp.int32)
    row_e = jnp.repeat(tile_e, tr)
    within = row - starts[row_e]
    dense_start = jnp.cumsum(counts) - counts
    valid = within < counts[row_e]
    src = jnp.where(valid, order[jnp.clip(dense_start[row_e] + within, 0, 2 * t - 1)] // 2, 0)
    n_used = (ends[-1] // tr).reshape(1).astype(jnp.int32)
    return src.astype(jnp.int32), tile_e, n_used, pos.astype(jnp.int32)


def _row_gather_start(idx_ref, first, n, src_hbm, dst, sem):
    def body(j, carry):
        pltpu.make_async_copy(src_hbm.at[pl.ds(idx_ref[first + j], 1)], dst.at[pl.ds(j, 1)], sem).start()
        return carry
    lax.fori_loop(0, n, body, 0, unroll=8)


def _row_gather_wait(n, src_hbm, dst, sem):
    pltpu.make_async_copy(src_hbm.at[pl.ds(0, n)], dst, sem).wait()


def _moe_ffn_kernel(src_ref, te_ref, nu_ref, x_hbm, wgu_ref, wdn_ref, o_ref, buf, sem):
    i = pl.program_id(0)
    n_used = nu_ref[0]
    tr = MOE_TR
    slot = lax.rem(i, 2)

    @pl.when(i == 0)
    def _():
        _row_gather_start(src_ref, 0, tr, x_hbm, buf.at[0], sem.at[0])

    @pl.when(i < n_used)
    def _():
        @pl.when(i + 1 < n_used)
        def _():
            _row_gather_start(src_ref, (i + 1) * tr, tr, x_hbm, buf.at[1 - slot], sem.at[1 - slot])

        _row_gather_wait(tr, x_hbm, buf.at[slot], sem.at[slot])
        packed = buf[slot]
        half = packed.shape[1]
        x_lo = lax.bitcast_convert_type(packed << 16, F32).astype(BF16)
        x_hi = lax.bitcast_convert_type(packed & jnp.uint32(0xFFFF0000), F32).astype(BF16)
        gu = _dot(x_lo, wgu_ref[:half, :].astype(BF16)) + _dot(x_hi, wgu_ref[half:, :].astype(BF16))
        hid = _silu(gu[:, :MOE_HIDDEN]) * gu[:, MOE_HIDDEN:]
        o_ref[...] = _dot(hid.astype(BF16), wdn_ref[...].astype(BF16))

    @pl.when(i >= n_used)
    def _():
        o_ref[...] = jnp.zeros(o_ref.shape, o_ref.dtype)


def moe_ffn(xn, w_gu, w_dn, l, src, tile_e, n_used):
    t = xn.shape[0]
    d = 2 * xn.shape[1]
    tr = MOE_TR
    npad = src.shape[0]
    f = MOE_HIDDEN
    grid_spec = pltpu.PrefetchScalarGridSpec(
        num_scalar_prefetch=3,
        grid=(npad // tr,),
        in_specs=[pl.BlockSpec(memory_space=pl.ANY),
                  pl.BlockSpec((None, None, d, 2 * f), lambda i, s, te, nu: (l, te[i], 0, 0)),
                  pl.BlockSpec((None, None, f, d), lambda i, s, te, nu: (l, te[i], 0, 0))],
        out_specs=pl.BlockSpec((tr, d), lambda i, s, te, nu: (i, 0)),
        scratch_shapes=[pltpu.VMEM((2, tr, d // 2), jnp.uint32), pltpu.SemaphoreType.DMA((2,))],
    )
    return pl.pallas_call(
        _moe_ffn_kernel,
        grid_spec=grid_spec,
        out_shape=jax.ShapeDtypeStruct((npad, d), F32),
        compiler_params=_params(("arbitrary",), 4 * tr * d * 4 + 2 * 3 * d * f * 4 + 6 * tr * d * 4),
        name="moe_ffn",
    )(src, tile_e, n_used, xn, w_gu, w_dn)


def _moe_combine_kernel(pos_ref, y_hbm, x_ref, route_ref, gate_ref, o_ref, buf, sem):
    i = pl.program_id(0)
    n = pl.num_programs(0)
    tm = x_ref.shape[0]
    slot = lax.rem(i, 2)

    def start(tile, s):
        _row_gather_start(pos_ref, 2 * tm * tile, 2 * tm, y_hbm, buf.at[s], sem.at[s])

    @pl.when(i == 0)
    def _():
        start(0, 0)

    @pl.when(i + 1 < n)
    def _():
        start(i + 1, 1 - slot)

    _row_gather_wait(2 * tm, y_hbm, buf.at[slot], sem.at[slot])
    route = route_ref[...]
    lane = lax.broadcasted_iota(jnp.int32, route.shape, 1)
    w1 = _lane_pick(route, lane, 2)
    w2 = _lane_pick(route, lane, 3)
    cur = buf.at[slot]
    o_ref[...] = x_ref[...] + gate_ref[...] * (w1 * cur[:tm, :] + w2 * cur[tm:, :])


def moe_combine(y, x, route, gate, pos, row0, tm=128):
    t, d = x.shape
    pos = pos[2 * row0:2 * (row0 + t)].reshape(t // tm, tm, 2).transpose(0, 2, 1).reshape(-1)
    blk0 = row0 // tm
    grid_spec = pltpu.PrefetchScalarGridSpec(
        num_scalar_prefetch=1,
        grid=(t // tm,),
        in_specs=[pl.BlockSpec(memory_space=pl.ANY),
                  pl.BlockSpec((tm, d), lambda i, p: (i, 0)),
                  pl.BlockSpec((tm, LANE), lambda i, p: (i + blk0, 0)),
                  pl.BlockSpec((1, d), lambda i, p: (0, 0))],
        out_specs=pl.BlockSpec((tm, d), lambda i, p: (i, 0)),
        scratch_shapes=[pltpu.VMEM((2, 2 * tm, d), F32), pltpu.SemaphoreType.DMA((2,))],
    )
    return pl.pallas_call(
        _moe_combine_kernel,
        grid_spec=grid_spec,
        out_shape=jax.ShapeDtypeStruct((t, d), F32),
        compiler_params=_params(("arbitrary",), 4 * tm * d * 4 + 8 * tm * d * 4),
        name="moe_combine",
    )(pos, y, x, route, gate.reshape(1, d))


def _layer(l, x, xc, mod, last, consts, w):
    (g_mix, w_in, rpb, w_fn, w_g2, b_g, g_gla, w_out, g_ffn, w_rg, b_rg, w_re, b_re, w_gu, w_dn) = w
    cos, sin, cos_c, sin_c, perm, na_tab = consts
    d = D_MODEL
    s_len = x.shape[0]
    m_x = mod[l, 0].reshape(N_MOD, d)
    m_c = mod[l, 1].reshape(N_MOD, d)

    w_lr = jnp.pad(w_in[l, :, OFF_LR:OFF_LR + 2 * GLA_GATE_RANK], ((0, 0), (0, LANE - 2 * GLA_GATE_RANK)))
    xn, lr = norm_proj(x, g_mix[l], m_x[0], m_x[1], w_lr)
    xcn, lr_c = norm_proj(xc, g_mix[l], m_c[0], m_c[1], w_lr)

    proj = []
    for a in (xn, xcn):
        u_na = matmul(a, w_in, (l,), OFF_NA, 3 * NA_WIDTH, 512, 1024, BF16, name="in_na")
        u_fn = matmul(a, w_in, (l,), OFF_FN, FN_WIDTH, 512, 1024, BF16, name="in_fn")
        u_qk = gla_qk_proj(a, w_in, l)
        u_vr = matmul(a, w_in, (l,), OFF_GVR, 2 * GLA_VW, 512, 1024, BF16, name="in_gvr")
        proj.append((u_na, u_fn, u_qk, u_vr))
    (u_na, u_fn, u_qk, u_vr), (c_na, c_fn, c_qk, c_vr) = proj

    w2p = [jnp.pad(_pad_heads(w_g2[l, dr]), ((dr * GLA_GATE_RANK, LANE - (dr + 1) * GLA_GATE_RANK), (0, 0)))
           for dr in (0, 1)]
    bgp = [_pad_heads(b_g[l, dr]).reshape(1, -1) for dr in (0, 1)]
    zero = jnp.zeros((GLA_HEADS, GLA_DV, GLA_DKP), F32)
    outs_c, outs_x = [], []
    for dr in (0, 1):
        o_c, s_c = gla_scan(c_qk, c_vr, lr_c, cos_c, sin_c, w2p[dr], bgp[dr], perm, zero, rev=bool(dr))
        o_x, _ = gla_scan(u_qk, u_vr, lr, cos, sin, w2p[dr], bgp[dr], perm, s_c, rev=bool(dr))
        outs_c.append(o_c)
        outs_x.append(o_x)

    mix = (neighbourhood_attention(u_na, c_na, na_tab, l),
           fourier_latent(u_fn, w_fn[l]),
           gla_output(outs_x[0], outs_x[1], u_vr, g_gla[l]))
    x = matmul(mix, w_out, (l,), 0, d, 512, 1024, F32, res=x, gate=m_x[2], name="out_proj")

    w_route = jnp.pad(jnp.concatenate([w_re[l], w_rg[l]], axis=1),
                      ((0, 0), (0, LANE - MOE_EXPERTS - MOE_GROUPS)))
    b_route = jnp.pad(jnp.concatenate([b_re[l], b_rg[l]]), (0, LANE - MOE_EXPERTS - MOE_GROUPS)).reshape(1, LANE)
    streams = [(x, m_x)]
    if not last:
        mix_c = (full_attention(c_na), fourier_context(c_fn, w_fn[l]),
                 gla_output(outs_c[0], outs_c[1], c_vr, g_gla[l]))
        xc = matmul(mix_c, w_out, (l,), 0, d, 512, 1024, F32, res=xc, gate=m_c[2], name="out_proj_ctx")
        streams.append((xc, m_c))

    yn, route = norm_route([(y, m[3], m[4]) for y, m in streams], g_ffn[l], w_route, b_route)
    src, tile_e, n_used, pos = moe_plan(route)
    y_exp = moe_ffn(yn, w_gu, w_dn, l, src, tile_e, n_used)
    outs, row0 = [], 0
    for y, m in streams:
        outs.append(moe_combine(y_exp, y, route, m[5], pos, row0))
        row0 += y.shape[0]
    return outs[0], (outs[1] if not last else xc)


def kernel(x, c, ctx, c_ctx, w_ada, b_ada, g_mix, w_in, rpb, w_fn, w_g2, b_g, g_gla, w_out,
           g_ffn, w_rg, b_rg, w_re, b_re, w_gu, w_dn, g_final):
    assert x.shape[0] == 1 and c.shape[0] == 1
    d = D_MODEL
    s_len = x.shape[1]
    c_len = ctx.shape[1]
    cc = jnp.concatenate([c, c_ctx[None], jnp.zeros((6, d), F32)], axis=0)
    mod = ada_mod(cc, w_ada, b_ada)
    cos, sin = rope_tables(s_len)
    cos_c = jnp.ones((c_len, GLA_DKP), F32)
    sin_c = jnp.zeros((c_len, GLA_DKP), F32)
    consts = (cos, sin, cos_c, sin_c, rope_perm(), na_tables(rpb, s_len // GRID_W))
    w = (g_mix, w_in, rpb, w_fn, w_g2, b_g, g_gla, w_out, g_ffn, w_rg, b_rg, w_re, b_re, w_gu, w_dn)
    xs, xc = x[0], ctx[0]
    for l in range(DEPTH):
        xs, xc = _layer(l, xs, xc, mod, l == DEPTH - 1, consts, w)
    return final_norm(xs, g_final)[None]
```

```python
import functools
import math

import numpy as np
import jax
import jax.numpy as jnp
from jax import lax
from jax.experimental import pallas as pl
from jax.experimental.pallas import tpu as pltpu

F32 = jnp.float32
BF16 = jnp.bfloat16

D_MODEL = 4096
DEPTH = 4
GRID_W = 64
EPS = 1e-6
N_MOD = 6

NA_HEADS = 12
NA_HEAD_DIM = 128
NA_WIN_H = 8
NA_WIN_W = 16
NA_WIDTH = NA_HEADS * NA_HEAD_DIM

FN_GROUPS = 8
FN_GROUP_DIM = 128
FN_WIDTH = FN_GROUPS * FN_GROUP_DIM

GLA_HEADS = 4
GLA_DK = 192
GLA_DKP = 256
GLA_DV = 384
GLA_KW = GLA_HEADS * GLA_DK
GLA_VW = GLA_HEADS * GLA_DV
GLA_GATE_RANK = 16
GLA_GATE_TEMP = 16.0
GLA_CHUNK = 64
GLA_SUB = 8
ROPE_THETA = 10000.0

MOE_GROUPS = 4
MOE_EXPERTS_PER_GROUP = 8
MOE_EXPERTS = MOE_GROUPS * MOE_EXPERTS_PER_GROUP
MOE_HIDDEN = 192

OFF_NA = 0
OFF_FN = 3 * NA_WIDTH
OFF_GQK = OFF_FN + FN_WIDTH
OFF_GVR = OFF_GQK + 2 * GLA_KW
OFF_LR = OFF_GVR + 2 * GLA_VW

LANE = 128
NEG = -1e30
V7X_VMEM_BUDGET = 56 * 1024 * 1024


def _params(sem, vmem_bytes):
    return pltpu.CompilerParams(dimension_semantics=sem,
                                vmem_limit_bytes=int(min(max(vmem_bytes, 16 << 20), V7X_VMEM_BUDGET)))


def _dot(a, b):
    return jnp.dot(a, b, preferred_element_type=F32)


def _dot_nt(a, b):
    return lax.dot_general(a, b, (((1,), (1,)), ((), ())), preferred_element_type=F32)


def _dot_tn(a, b):
    return lax.dot_general(a, b, (((0,), (0,)), ((), ())), preferred_element_type=F32)


def _silu(x):
    return x * (1.0 / (1.0 + jnp.exp(-x)))


def _ada_kernel(c_ref, w_ref, b_ref, o_ref):
    s = _silu(c_ref[...]).astype(BF16)
    o_ref[0] = _dot(s, w_ref[0].astype(BF16)) + b_ref[0]


def ada_mod(cc, w_ada, b_ada, tn=512):
    n, d, nd = w_ada.shape
    return pl.pallas_call(
        _ada_kernel,
        grid=(n, nd // tn),
        in_specs=[pl.BlockSpec((8, d), lambda l, j: (0, 0)),
                  pl.BlockSpec((1, d, tn), lambda l, j: (l, 0, j)),
                  pl.BlockSpec((1, 1, tn), lambda l, j: (l, 0, j))],
        out_specs=pl.BlockSpec((1, 8, tn), lambda l, j: (l, 0, j)),
        out_shape=jax.ShapeDtypeStruct((n, 8, nd), F32),
        compiler_params=_params(("arbitrary", "arbitrary"), 3 * d * tn * 4 + (4 << 20)),
        name="ada_mod",
    )(cc, w_ada, b_ada.reshape(n, 1, nd))


def _normed(x_ref, g_ref, sh_ref, sc_ref):
    x = x_ref[...]
    y = x * lax.rsqrt(jnp.mean(x * x, axis=-1, keepdims=True) + EPS)
    return (y * g_ref[...]) * (1.0 + sc_ref[...]) + sh_ref[...]


def _norm_proj_kernel(x_ref, g_ref, sh_ref, sc_ref, w_hbm, xn_ref, p_ref, w_buf, sem, *, layer, col0):
    @pl.when(pl.program_id(0) == 0)
    def _():
        cp = pltpu.make_async_copy(w_hbm.at[layer, :, pl.ds(col0, w_buf.shape[1])], w_buf, sem)
        cp.start()
        cp.wait()

    hb = _normed(x_ref, g_ref, sh_ref, sc_ref).astype(BF16)
    xn_ref[...] = hb
    p_ref[...] = _dot(hb, w_buf[...].astype(BF16))


def _lane_pick(v, lane, idx):
    return jnp.sum(jnp.where(lane == idx, v, 0.0), axis=-1, keepdims=True)


def _pack_bf16_pairs(x):
    xf = x.astype(BF16).astype(F32)
    half = xf.shape[1] // 2
    lo = lax.bitcast_convert_type(xf[:, :half], jnp.uint32)
    hi = lax.bitcast_convert_type(xf[:, half:], jnp.uint32)
    return hi | (lo >> 16)


def _unpack_bf16_pairs(p):
    lo = lax.bitcast_convert_type(p << 16, F32)
    hi = lax.bitcast_convert_type(p & jnp.uint32(0xFFFF0000), F32)
    return lo, hi


def _norm_route_kernel(x_ref, g_ref, sh_ref, sc_ref, w_ref, b_ref, xn_ref, route_ref):
    h = _normed(x_ref, g_ref, sh_ref, sc_ref)
    hb = h.astype(BF16)
    xn_ref[...] = _pack_bf16_pairs(h)
    logits = _dot(hb, w_ref[...].astype(BF16)) + b_ref[...]
    lane = lax.broadcasted_iota(jnp.int32, logits.shape, 1).astype(F32)
    far = float(4 * LANE)
    is_g = (lane >= MOE_EXPERTS) & (lane < MOE_EXPERTS + MOE_GROUPS)
    gl = jnp.where(is_g, logits, NEG)
    gmax = jnp.max(gl, axis=-1, keepdims=True)
    g_top = jnp.min(jnp.where(gl == gmax, lane, far), axis=-1, keepdims=True) - MOE_EXPERTS
    g_w = 1.0 / jnp.sum(jnp.where(is_g, jnp.exp(gl - gmax), 0.0), axis=-1, keepdims=True)
    in_grp = (lane >= g_top * MOE_EXPERTS_PER_GROUP) & (lane < (g_top + 1) * MOE_EXPERTS_PER_GROUP)
    el = jnp.where(in_grp, logits, NEG)
    v1 = jnp.max(el, axis=-1, keepdims=True)
    i1 = jnp.min(jnp.where(el == v1, lane, far), axis=-1, keepdims=True)
    el2 = jnp.where(lane == i1, NEG, el)
    v2 = jnp.max(el2, axis=-1, keepdims=True)
    i2 = jnp.min(jnp.where(el2 == v2, lane, far), axis=-1, keepdims=True)
    e2 = jnp.exp(v2 - v1)
    w1 = g_w / (1.0 + e2)
    w2 = g_w * e2 / (1.0 + e2)
    route_ref[...] = (jnp.where(lane == 0.0, i1, 0.0) + jnp.where(lane == 1.0, i2, 0.0)
                      + jnp.where(lane == 2.0, w1, 0.0) + jnp.where(lane == 3.0, w2, 0.0))


def _norm_route_streams_kernel(*refs, blocks):
    ns = len(blocks)
    x_refs, g_ref = refs[:ns], refs[ns]
    mods = refs[ns + 1:ns + 1 + 2 * ns]
    w_ref, b_ref, xn_ref, route_ref = refs[ns + 1 + 2 * ns:]
    i = pl.program_id(0)
    first = 0
    for s_, nb in enumerate(blocks):
        @pl.when((i >= first) & (i < first + nb))
        def _(s_=s_):
            _norm_route_kernel(x_refs[s_], g_ref, mods[2 * s_], mods[2 * s_ + 1], w_ref, b_ref, xn_ref, route_ref)
        first += nb


def norm_proj(x, g, sh, sc, w, layer, col0, ncols, tm=256):
    t, d = x.shape
    row = lambda i: (0, 0)
    vec = pl.BlockSpec((1, d), row)
    return pl.pallas_call(
        functools.partial(_norm_proj_kernel, layer=layer, col0=col0),
        grid=(t // tm,),
        in_specs=[pl.BlockSpec((tm, d), lambda i: (i, 0)), vec, vec, vec, pl.BlockSpec(memory_space=pl.ANY)],
        out_specs=[pl.BlockSpec((tm, d), lambda i: (i, 0)), pl.BlockSpec((tm, ncols), lambda i: (i, 0))],
        out_shape=[jax.ShapeDtypeStruct((t, d), BF16), jax.ShapeDtypeStruct((t, ncols), F32)],
        scratch_shapes=[pltpu.VMEM((d, ncols), F32), pltpu.SemaphoreType.DMA(())],
        compiler_params=_params(("arbitrary",), 8 * tm * d * 4 + 2 * d * LANE * 4),
        name="norm_proj",
    )(x, g.reshape(1, d), sh.reshape(1, d), sc.reshape(1, d), w)


def norm_route(streams, g, w_small, b_small, tm=256):
    d = streams[0][0].shape[1]
    blocks = tuple(x.shape[0] // tm for x, _, _ in streams)
    row = lambda i: (0, 0)
    vec = pl.BlockSpec((1, d), row)
    ins, args, first = [], [], 0
    for (x, _, _), nb in zip(streams, blocks):
        ins.append(pl.BlockSpec((tm, d), lambda i, first=first, nb=nb: (jnp.clip(i - first, 0, nb - 1), 0)))
        args.append(x)
        first += nb
    ins.append(vec)
    args.append(g.reshape(1, d))
    for _, sh, sc in streams:
        ins += [vec, vec]
        args += [sh.reshape(1, d), sc.reshape(1, d)]
    ins += [pl.BlockSpec((d, LANE), row), pl.BlockSpec((1, LANE), row)]
    args += [w_small, b_small]
    rows = sum(blocks) * tm
    return pl.pallas_call(
        functools.partial(_norm_route_streams_kernel, blocks=blocks),
        grid=(sum(blocks),),
        in_specs=ins,
        out_specs=[pl.BlockSpec((tm, d // 2), lambda i: (i, 0)), pl.BlockSpec((tm, LANE), lambda i: (i, 0))],
        out_shape=[jax.ShapeDtypeStruct((rows, d // 2), jnp.uint32), jax.ShapeDtypeStruct((rows, LANE), F32)],
        compiler_params=_params(("arbitrary",), (4 + 4 * len(streams)) * tm * d * 4 + 4 * d * LANE * 4),
        name="norm_route",
    )(*args)


def _final_norm_kernel(x_ref, g_ref, o_ref):
    x = x_ref[...]
    o_ref[...] = x * lax.rsqrt(jnp.mean(x * x, axis=-1, keepdims=True) + EPS) * g_ref[...]


def final_norm(x, g, tm=256):
    t, d = x.shape
    return pl.pallas_call(
        _final_norm_kernel,
        grid=(t // tm,),
        in_specs=[pl.BlockSpec((tm, d), lambda i: (i, 0)), pl.BlockSpec((1, d), lambda i: (0, 0))],
        out_specs=pl.BlockSpec((tm, d), lambda i: (i, 0)),
        out_shape=jax.ShapeDtypeStruct((t, d), F32),
        compiler_params=_params(("arbitrary",), 6 * tm * d * 4),
        name="final_norm",
    )(x, g.reshape(1, d))


def _mm_kernel(a_ref, w_ref, o_ref):
    o_ref[...] = _dot(a_ref[...], w_ref[...].astype(BF16)).astype(o_ref.dtype)


def _mm_res_kernel(a_ref, w_ref, r_ref, g_ref, o_ref):
    o_ref[...] = r_ref[...] + g_ref[...] * _dot(a_ref[...], w_ref[...].astype(BF16))


def _mm_parts_res_kernel(*refs, widths):
    n = len(widths)
    w_ref, r_ref, g_ref, o_ref = refs[n:]
    acc, k0 = None, 0
    for a_ref, wd in zip(refs[:n], widths):
        part = _dot(a_ref[...], w_ref[k0:k0 + wd, :].astype(BF16))
        acc = part if acc is None else acc + part
        k0 += wd
    o_ref[...] = r_ref[...] + g_ref[...] * acc


def _mm_head_pad_kernel(a_ref, w0_ref, w1_ref, w2_ref, o_ref):
    a = a_ref[...]
    acc = jnp.concatenate([_dot(a, w_ref[...].astype(BF16)) for w_ref in (w0_ref, w1_ref, w2_ref)], axis=1)
    pad = jnp.zeros((a.shape[0], GLA_DKP - GLA_DK), o_ref.dtype)
    for h in range(GLA_HEADS):
        o_ref[:, h * GLA_DKP:h * GLA_DKP + GLA_DK] = acc[:, h * GLA_DK:(h + 1) * GLA_DK].astype(o_ref.dtype)
        o_ref[:, h * GLA_DKP + GLA_DK:(h + 1) * GLA_DKP] = pad


def gla_qk_proj(a, w_in, l, tm=1024):
    t, k = a.shape
    tm = min(tm, t)
    tn = GLA_KW // 3
    j0 = OFF_GQK // tn
    w_spec = lambda p: pl.BlockSpec((None, k, tn), lambda i, j: (l, 0, j0 + 3 * j + p))
    kwp = GLA_HEADS * GLA_DKP
    return pl.pallas_call(
        _mm_head_pad_kernel,
        grid=(t // tm, 2),
        in_specs=[pl.BlockSpec((tm, k), lambda i, j: (i, 0)), w_spec(0), w_spec(1), w_spec(2)],
        out_specs=pl.BlockSpec((tm, kwp), lambda i, j: (i, j)),
        out_shape=jax.ShapeDtypeStruct((t, 2 * kwp), BF16),
        compiler_params=_params(("arbitrary", "arbitrary"), 2 * tm * k * 2 + 7 * k * tn * 4 + 8 * tm * kwp * 4),
        name="in_gqk",
    )(a, w_in, w_in, w_in)


def matmul(a, w, w_lead, col0, ncols, tn, tm, out_dtype, res=None, gate=None, name="mm"):
    parts = a if isinstance(a, (tuple, list)) else (a,)
    t = parts[0].shape[0]
    k = sum(p.shape[1] for p in parts)
    tm = min(tm, t)
    nlead = len(w_lead)
    j0 = col0 // tn
    w_spec = pl.BlockSpec((None,) * nlead + (k, tn), lambda i, j: tuple(w_lead) + (0, j + j0))
    ins = [pl.BlockSpec((tm, p.shape[1]), lambda i, j: (i, 0)) for p in parts] + [w_spec]
    args = list(parts) + [w]
    wbytes = jnp.dtype(w.dtype).itemsize
    vmem = 2 * tm * k * 2 + 3 * k * tn * wbytes + 6 * tm * tn * 4
    if res is None:
        body = _mm_kernel
    else:
        body = _mm_res_kernel if len(parts) == 1 else functools.partial(
            _mm_parts_res_kernel, widths=tuple(p.shape[1] for p in parts))
        ins += [pl.BlockSpec((tm, tn), lambda i, j: (i, j)), pl.BlockSpec((1, tn), lambda i, j: (0, j))]
        args += [res, gate.reshape(1, ncols)]
    return pl.pallas_call(
        body,
        grid=(t // tm, ncols // tn),
        in_specs=ins,
        out_specs=pl.BlockSpec((tm, tn), lambda i, j: (i, j)),
        out_shape=jax.ShapeDtypeStruct((t, ncols), out_dtype),
        compiler_params=_params(("arbitrary", "arbitrary"), vmem),
        name=name,
    )(*args)


NA_QROWS = 4
NA_SLAB = 12
NA_HEADS_PER_STEP = 4


def na_tables(rpb, rows):
    nblk = rows // NA_QROWS
    n_, h_ = rpb.shape[:2]
    qc = np.arange(GRID_W)[:, None]
    kc = np.arange(GRID_W)[None, :]
    c0 = np.clip(qc - NA_WIN_W // 2, 0, GRID_W - NA_WIN_W)
    ok_c = (kc >= c0) & (kc < c0 + NA_WIN_W)
    a = np.arange(NA_QROWS)[:, None]
    b = np.arange(NA_SLAB)[None, :]
    dr_l, ok_l = [], []
    for i in (0, 1, nblk - 1):
        base = int(np.clip(i * NA_QROWS - NA_WIN_H // 2, 0, rows - NA_SLAB))
        r = i * NA_QROWS + a
        r0 = np.clip(r - NA_WIN_H // 2, 0, rows - NA_WIN_H)
        krow = base + b
        ok_l.append((krow >= r0) & (krow < r0 + NA_WIN_H))
        dr_l.append(np.clip(krow - r + NA_WIN_H - 1, 0, 2 * NA_WIN_H - 2))
    dr = np.stack(dr_l)
    ok = np.stack(ok_l)[:, :, None, :, None] & ok_c[None, None, :, None, :]
    ok = jnp.asarray(ok.reshape(3, NA_QROWS * GRID_W, NA_SLAB * GRID_W))
    lo = GRID_W - NA_WIN_W
    padded = jnp.pad(rpb.astype(F32), ((0, 0), (0, 0), (0, 0), (lo, lo)))
    by_col = jnp.stack([padded[..., GRID_W - 1 - q:2 * GRID_W - 1 - q] for q in range(GRID_W)], axis=3)
    variants = []
    for v in range(3):
        rows_ = [jnp.concatenate([by_col[:, :, int(dr[v, a_, b_])] for b_ in range(NA_SLAB)], axis=-1)
                 for a_ in range(NA_QROWS)]
        variants.append(jnp.concatenate(rows_, axis=2))
    bias = jnp.stack(variants, axis=1)
    return jnp.where(ok[None, :, None], bias, NEG)


def _na_kernel(q_ref, k_ref, v_ref, kc_ref, vc_ref, t_ref, o_ref, *, rows):
    i = pl.program_id(1)
    base = jnp.clip(i * NA_QROWS - NA_WIN_H // 2, 0, rows - NA_SLAB) * GRID_W
    base = pl.multiple_of(base, NA_QROWS * GRID_W)
    n_keys = NA_SLAB * GRID_W
    scale = NA_HEAD_DIM ** -0.5
    for hh in range(NA_HEADS_PER_STEP):
        cols = slice(hh * NA_HEAD_DIM, (hh + 1) * NA_HEAD_DIM)
        q = q_ref[:, cols]
        k = k_ref[pl.ds(base, n_keys), cols]
        v = v_ref[pl.ds(base, n_keys), cols]
        s = _dot_nt(q, k) * scale + t_ref[hh]
        sc = _dot_nt(q, kc_ref[:, cols]) * scale
        m = jnp.maximum(jnp.max(s, axis=-1, keepdims=True), jnp.max(sc, axis=-1, keepdims=True))
        p = jnp.exp(s - m)
        pc = jnp.exp(sc - m)
        den = jnp.sum(p, axis=-1, keepdims=True) + jnp.sum(pc, axis=-1, keepdims=True)
        o = _dot(p.astype(BF16), v) + _dot(pc.astype(BF16), vc_ref[:, cols])
        o_ref[:, cols] = (o / den).astype(o_ref.dtype)


def neighbourhood_attention(u, uc, tables, l):
    s = u.shape[0]
    c = uc.shape[0]
    rows = s // GRID_W
    nblk = rows // NA_QROWS
    tq = NA_QROWS * GRID_W
    hps = NA_HEADS_PER_STEP
    ng = NA_HEADS // hps
    hw = hps * NA_HEAD_DIM

    def variant(i):
        return jnp.where(i == 0, 0, jnp.where(i == nblk - 1, 2, 1))

    return pl.pallas_call(
        functools.partial(_na_kernel, rows=rows),
        grid=(ng, nblk),
        in_specs=[pl.BlockSpec((tq, hw), lambda h, i: (i, h)),
                  pl.BlockSpec((s, hw), lambda h, i: (0, ng + h)),
                  pl.BlockSpec((s, hw), lambda h, i: (0, 2 * ng + h)),
                  pl.BlockSpec((c, hw), lambda h, i: (0, ng + h)),
                  pl.BlockSpec((c, hw), lambda h, i: (0, 2 * ng + h)),
                  pl.BlockSpec((None, None, hps, tq, NA_SLAB * GRID_W), lambda h, i: (l, variant(i), h, 0, 0))],
        out_specs=pl.BlockSpec((tq, hw), lambda h, i: (i, h)),
        out_shape=jax.ShapeDtypeStruct((s, NA_WIDTH), BF16),
        compiler_params=_params(("arbitrary", "arbitrary"),
                                4 * s * hw * 2 + 2 * hps * tq * NA_SLAB * GRID_W * 4 + (12 << 20)),
        name="na_latent",
    )(u, u, u, uc, uc, tables)


def _full_attn_kernel(q_ref, k_ref, v_ref, o_ref):
    s = _dot_nt(q_ref[...], k_ref[...]) * NA_HEAD_DIM ** -0.5
    p = jnp.exp(s - jnp.max(s, axis=-1, keepdims=True))
    den = jnp.sum(p, axis=-1, keepdims=True)
    o_ref[...] = (_dot(p.astype(BF16), v_ref[...]) / den).astype(o_ref.dtype)


def full_attention(uc):
    c = uc.shape[0]
    h_ = NA_HEADS
    hd = NA_HEAD_DIM
    return pl.pallas_call(
        _full_attn_kernel,
        grid=(h_,),
        in_specs=[pl.BlockSpec((c, hd), lambda h: (0, h)),
                  pl.BlockSpec((c, hd), lambda h: (0, h_ + h)),
                  pl.BlockSpec((c, hd), lambda h: (0, 2 * h_ + h))],
        out_specs=pl.BlockSpec((c, hd), lambda h: (0, h)),
        out_shape=jax.ShapeDtypeStruct((c, NA_WIDTH), BF16),
        compiler_params=_params(("arbitrary",), 16 << 20),
        name="na_context",
    )(uc, uc, uc)


def _dft_cs(n):
    idx = np.arange(n)
    ang = 2.0 * np.pi * ((idx[:, None] * idx[None, :]) % n) / n
    return np.cos(ang), np.sin(ang)


FN_K1B = 8


def _fn1_kernel(w_ref, x_ref, tr_ref, ti_ref, z_ref, *, r):
    y = _dot(w_ref[...], x_ref[...])
    yr, yi = y[:r], y[r:]
    tr, ti = tr_ref[...], ti_ref[...]
    z_ref[:r, :] = (yr * tr - yi * ti).astype(z_ref.dtype)
    z_ref[r:, :] = (yr * ti + yi * tr).astype(z_ref.dtype)


def _fn2_kernel(zr_ref, zi_ref, kc_ref, ks_ref, c_ref, s_ref, w_ref, o_ref, *, scale):
    zr, zi = zr_ref[...], zi_ref[...]
    kc, ks = kc_ref[...], ks_ref[...]
    xr = (_dot(kc, zr) + _dot(ks, zi)).astype(BF16)
    xi = (_dot(kc, zi) - _dot(ks, zr)).astype(BF16)
    cw = GRID_W
    for g in range(FN_GROUPS):
        lo = g * FN_GROUP_DIM
        f = _dot(xr[:, lo:lo + FN_GROUP_DIM], c_ref[...]) + _dot(xi[:, lo:lo + FN_GROUP_DIM], s_ref[...])
        y = _dot((f * scale).astype(BF16), w_ref[g].astype(BF16))
        for kk in range(FN_K1B):
            o_ref[:, kk * FN_WIDTH + lo:kk * FN_WIDTH + lo + FN_GROUP_DIM] = (
                y[kk * cw:(kk + 1) * cw].astype(o_ref.dtype))


def fourier_latent(u, w_fn):
    l = u.shape[0]
    cw = GRID_W
    r = l // cw
    c_r, s_r = _dft_cs(r)
    w1 = jnp.asarray(np.concatenate([c_r, -s_r], axis=0), BF16)
    ang = 2.0 * np.pi * (np.arange(cw)[:, None] * np.arange(r)[None, :]) / l
    tr = jnp.asarray(np.cos(ang)[:, :, None], F32)
    ti = jnp.asarray(-np.sin(ang)[:, :, None], F32)
    z = pl.pallas_call(
        functools.partial(_fn1_kernel, r=r),
        grid=(cw,),
        in_specs=[pl.BlockSpec((2 * r, r), lambda j: (0, 0)),
                  pl.BlockSpec((r, FN_WIDTH), lambda j: (0, j)),
                  pl.BlockSpec((None, r, 1), lambda j: (j, 0, 0)),
                  pl.BlockSpec((None, r, 1), lambda j: (j, 0, 0))],
        out_specs=pl.BlockSpec((2 * r, FN_WIDTH), lambda j: (0, j)),
        out_shape=jax.ShapeDtypeStruct((2 * r, cw * FN_WIDTH), BF16),
        compiler_params=_params(("arbitrary",), 24 << 20),
        name="fourier_stage1",
    )(w1, u.reshape(r, cw * FN_WIDTH), tr, ti)
    z2 = z.reshape(2 * r * cw, FN_WIDTH)
    c_w, s_w = _dft_cs(cw)
    eye = np.eye(FN_K1B)
    kc = jnp.asarray(np.kron(eye, c_w), BF16)
    ks = jnp.asarray(np.kron(eye, s_w), BF16)
    c_c, s_c = _dft_cs(FN_GROUP_DIM)
    nb = r // FN_K1B
    tb = FN_K1B * cw
    out = pl.pallas_call(
        functools.partial(_fn2_kernel, scale=float((l * FN_GROUP_DIM) ** -0.5)),
        grid=(nb,),
        in_specs=[pl.BlockSpec((tb, FN_WIDTH), lambda b: (b, 0)),
                  pl.BlockSpec((tb, FN_WIDTH), lambda b: (nb + b, 0)),
                  pl.BlockSpec((tb, tb), lambda b: (0, 0)),
                  pl.BlockSpec((tb, tb), lambda b: (0, 0)),
                  pl.BlockSpec((FN_GROUP_DIM, FN_GROUP_DIM), lambda b: (0, 0)),
                  pl.BlockSpec((FN_GROUP_DIM, FN_GROUP_DIM), lambda b: (0, 0)),
                  pl.BlockSpec((FN_GROUPS, FN_GROUP_DIM, FN_GROUP_DIM), lambda b: (0, 0, 0))],
        out_specs=pl.BlockSpec((cw, FN_K1B * FN_WIDTH), lambda b: (0, b)),
        out_shape=jax.ShapeDtypeStruct((cw, r * FN_WIDTH), BF16),
        compiler_params=_params(("arbitrary",), 24 << 20),
        name="fourier_stage2",
    )(z2, z2, kc, ks, jnp.asarray(c_c, BF16), jnp.asarray(s_c, BF16), w_fn)
    return out.reshape(l, FN_WIDTH)


def _fn_ctx_kernel(u_ref, cl_ref, sl_ref, c_ref, s_ref, w_ref, o_ref, *, scale):
    u = u_ref[...]
    gr = _dot(cl_ref[...], u).astype(BF16)
    gi = (-_dot(sl_ref[...], u)).astype(BF16)
    for g in range(FN_GROUPS):
        lo = g * FN_GROUP_DIM
        f = _dot(gr[:, lo:lo + FN_GROUP_DIM], c_ref[...]) + _dot(gi[:, lo:lo + FN_GROUP_DIM], s_ref[...])
        o_ref[:, lo:lo + FN_GROUP_DIM] = _dot((f * scale).astype(BF16), w_ref[g].astype(BF16)).astype(o_ref.dtype)


def fourier_context(u, w_fn):
    c = u.shape[0]
    c_l, s_l = _dft_cs(c)
    c_c, s_c = _dft_cs(FN_GROUP_DIM)
    return pl.pallas_call(
        functools.partial(_fn_ctx_kernel, scale=float((c * FN_GROUP_DIM) ** -0.5)),
        out_shape=jax.ShapeDtypeStruct((c, FN_WIDTH), BF16),
        compiler_params=_params((), 16 << 20),
        name="fourier_context",
    )(u, jnp.asarray(c_l, BF16), jnp.asarray(s_l, BF16), jnp.asarray(c_c, BF16), jnp.asarray(s_c, BF16), w_fn)


def _split_hi_lo(x):
    hi = x.astype(BF16)
    return hi, (x - hi.astype(F32)).astype(BF16)


def _gla_chunk(q_ref, k_ref, v_ref, lr_ref, cos_ref, sin_ref, w2_ref, bg_ref, perm_ref, st_ref, o_ref, rev):
    cs = GLA_CHUNK
    sub = GLA_SUB
    nsub = cs // sub
    z = _dot(lr_ref[...].astype(BF16), w2_ref[...].astype(BF16)) + bg_ref[...]
    la = -(jnp.maximum(-z, 0.0) + jnp.log(1.0 + jnp.exp(-jnp.abs(z)))) * (1.0 / GLA_GATE_TEMP)
    ri = lax.broadcasted_iota(jnp.int32, (cs, cs), 0)
    ci = lax.broadcasted_iota(jnp.int32, (cs, cs), 1)
    tri = jnp.where((ci >= ri) if rev else (ci <= ri), 1.0, 0.0).astype(BF16)
    la_hi, la_lo = _split_hi_lo(la)
    bcum = _dot(tri, la_hi) + _dot(tri, la_lo)
    edge = 0 if rev else cs - 1
    row_id = lax.broadcasted_iota(jnp.int32, (cs, GLA_DKP), 0)
    sub_r = lax.broadcasted_iota(jnp.int32, (sub, GLA_DKP), 0)
    lane_c = lax.broadcasted_iota(jnp.int32, (sub, cs), 1)
    cos, sin = cos_ref[...], sin_ref[...]
    perm = perm_ref[...]
    qscale = GLA_DK ** -0.5
    for h in range(GLA_HEADS):
        ks_ = slice(h * GLA_DKP, (h + 1) * GLA_DKP)
        vs_ = slice(h * GLA_DV, (h + 1) * GLA_DV)
        qb, kb = q_ref[:, ks_], k_ref[:, ks_]
        q = (qb.astype(F32) * cos + _dot(qb, perm) * sin) * qscale
        k = kb.astype(F32) * cos + _dot(kb, perm) * sin
        v = v_ref[:, vs_]
        b = bcum[:, ks_]
        b_edge = b[edge:edge + 1, :]
        st = st_ref[h]
        o = _dot_nt((q * jnp.exp(b)).astype(BF16), st.astype(BF16))
        slabs = []
        for blk in range(nsub):
            lo = blk * sub
            q_i = q[lo:lo + sub]
            b_i = b[lo:lo + sub]
            k_i = k[lo:lo + sub]
            acc = jnp.zeros((sub, cs), F32)
            if rev and blk < nsub - 1:
                ref_row = b[lo + sub:lo + sub + 1, :]
                outside = row_id >= lo + sub
            elif (not rev) and blk > 0:
                ref_row = b[lo - 1:lo, :]
                outside = row_id < lo
            else:
                ref_row = None
            if ref_row is not None:
                qe = q_i * jnp.exp(b_i - ref_row)
                ke = jnp.where(outside, k * jnp.exp(jnp.where(outside, ref_row - b, 0.0)), 0.0)
                acc = acc + _dot_nt(qe.astype(BF16), ke.astype(BF16))
            for j in range(sub):
                keep = (sub_r <= j) if rev else (sub_r >= j)
                d = jnp.exp(jnp.where(keep, b_i - b_i[j:j + 1, :], NEG))
                col = jnp.sum(q_i * d * k_i[j:j + 1, :], axis=-1, keepdims=True)
                acc = acc + jnp.where(lane_c == lo + j, col, 0.0)
            slabs.append(acc)
        attn = jnp.concatenate(slabs, axis=0)
        o = o + _dot(attn.astype(BF16), v)
        o_ref[:, vs_] = o
        kend = (k * jnp.exp(b_edge - b)).astype(BF16)
        st_ref[h] = st * jnp.exp(b_edge) + _dot_tn(v, kend)


def _gla_kernel(*refs, nchunks):
    fwd, bwd = refs[0:6], refs[6:12]
    w2f_ref, bgf_ref, w2b_ref, bgb_ref, perm_ref, s0_ref = refs[12:18]
    of_ref, ob_ref, sfin_ref, st_ref = refs[18:22]
    step = pl.program_id(0)

    @pl.when(step == 0)
    def _():
        st_ref[...] = s0_ref[...]

    _gla_chunk(*fwd, w2f_ref, bgf_ref, perm_ref, st_ref.at[0], of_ref, False)
    _gla_chunk(*bwd, w2b_ref, bgb_ref, perm_ref, st_ref.at[1], ob_ref, True)

    @pl.when(step == nchunks - 1)
    def _():
        sfin_ref[...] = st_ref[...]


def gla_scan(u_qk, u_vr, lr, cos, sin, w2p, bgp, perm, s0):
    l = u_qk.shape[0]
    n = l // GLA_CHUNK
    cs = GLA_CHUNK
    kw = GLA_HEADS * GLA_DKP
    rank2 = lr.shape[1]
    st_shape = (2, GLA_HEADS, GLA_DV, GLA_DKP)
    ins = []
    for ch in (lambda s: s, lambda s: n - 1 - s):
        ins += [pl.BlockSpec((cs, kw), lambda s, ch=ch: (ch(s), 0)),
                pl.BlockSpec((cs, kw), lambda s, ch=ch: (ch(s), 1)),
                pl.BlockSpec((cs, GLA_VW), lambda s, ch=ch: (ch(s), 0)),
                pl.BlockSpec((cs, rank2), lambda s, ch=ch: (ch(s), 0)),
                pl.BlockSpec((cs, GLA_DKP), lambda s, ch=ch: (ch(s), 0)),
                pl.BlockSpec((cs, GLA_DKP), lambda s, ch=ch: (ch(s), 0))]
    for dr in (0, 1):
        ins += [pl.BlockSpec((None, rank2, kw), lambda s, dr=dr: (dr, 0, 0)),
                pl.BlockSpec((None, 1, kw), lambda s, dr=dr: (dr, 0, 0))]
    ins += [pl.BlockSpec((GLA_DKP, GLA_DKP), lambda s: (0, 0)),
            pl.BlockSpec(st_shape, lambda s: (0, 0, 0, 0))]
    seq = (u_qk, u_qk, u_vr, lr, cos, sin)
    return pl.pallas_call(
        functools.partial(_gla_kernel, nchunks=n),
        grid=(n,),
        in_specs=ins,
        out_specs=[pl.BlockSpec((cs, GLA_VW), lambda s: (s, 0)),
                   pl.BlockSpec((cs, GLA_VW), lambda s: (n - 1 - s, 0)),
                   pl.BlockSpec(st_shape, lambda s: (0, 0, 0, 0))],
        out_shape=[jax.ShapeDtypeStruct((l, GLA_VW), F32), jax.ShapeDtypeStruct((l, GLA_VW), F32),
                   jax.ShapeDtypeStruct(st_shape, F32)],
        scratch_shapes=[pltpu.VMEM(st_shape, F32)],
        compiler_params=_params(("arbitrary",), 40 << 20),
        name="gla_scan",
    )(*seq, *seq, w2p, bgp, w2p, bgp, perm, s0)


def _gla_out_kernel(of_ref, ob_ref, r_ref, g_ref, o_ref):
    o = of_ref[...] + ob_ref[...]
    g = g_ref[...]
    r = r_ref[...].astype(F32)
    for h in range(GLA_HEADS):
        sl = slice(h * GLA_DV, (h + 1) * GLA_DV)
        oh = o[:, sl]
        on = oh * lax.rsqrt(jnp.mean(oh * oh, axis=-1, keepdims=True) + EPS) * g[:, sl]
        o_ref[:, sl] = (on * _silu(r[:, sl])).astype(o_ref.dtype)


def gla_output(o_f, o_b, u_vr, g_gla, tm=256):
    l = o_f.shape[0]
    tm = min(tm, l)
    return pl.pallas_call(
        _gla_out_kernel,
        grid=(l // tm,),
        in_specs=[pl.BlockSpec((tm, GLA_VW), lambda i: (i, 0)),
                  pl.BlockSpec((tm, GLA_VW), lambda i: (i, 0)),
                  pl.BlockSpec((tm, GLA_VW), lambda i: (i, 1)),
                  pl.BlockSpec((1, GLA_VW), lambda i: (0, 0))],
        out_specs=pl.BlockSpec((tm, GLA_VW), lambda i: (i, 0)),
        out_shape=jax.ShapeDtypeStruct((l, GLA_VW), BF16),
        compiler_params=_params(("arbitrary",), 24 << 20),
        name="gla_output",
    )(o_f, o_b, u_vr, g_gla.reshape(1, GLA_VW))


def rope_tables(n_tokens):
    seg = GLA_DK // 2
    half = seg // 2
    inv = ROPE_THETA ** (-jnp.arange(half, dtype=F32) / half)
    pos = jnp.arange(n_tokens)
    ang_r = (pos // GRID_W).astype(F32)[:, None] * inv
    ang_c = (pos % GRID_W).astype(F32)[:, None] * inv
    pad1 = jnp.ones((n_tokens, GLA_DKP - GLA_DK), F32)
    pad0 = jnp.zeros((n_tokens, GLA_DKP - GLA_DK), F32)
    cos = jnp.concatenate([jnp.cos(ang_r)] * 2 + [jnp.cos(ang_c)] * 2 + [pad1], axis=1)
    sin = jnp.concatenate([jnp.sin(ang_r)] * 2 + [jnp.sin(ang_c)] * 2 + [pad0], axis=1)
    return cos, sin


def rope_perm():
    seg = GLA_DK // 2
    half = seg // 2
    p = np.zeros((GLA_DKP, GLA_DKP), np.float32)
    for s0 in (0, seg):
        for j in range(half):
            p[s0 + half + j, s0 + j] = -1.0
            p[s0 + j, s0 + half + j] = 1.0
    return jnp.asarray(p, BF16)


def _pad_heads(w):
    lead = w.shape[:-1]
    w = w.reshape(lead + (GLA_HEADS, GLA_DK))
    w = jnp.pad(w, [(0, 0)] * len(lead) + [(0, 0), (0, GLA_DKP - GLA_DK)])
    return w.reshape(lead + (GLA_HEADS * GLA_DKP,))


MOE_TR = 256


def moe_plan(route):
    t = route.shape[0]
    tr = MOE_TR
    e = route[:, :2].astype(jnp.int32).reshape(-1)
    onehot = (e[:, None] == jnp.arange(MOE_EXPERTS, dtype=jnp.int32)[None]).astype(jnp.int32)
    csum = jnp.cumsum(onehot, axis=0)
    counts = csum[-1]
    padded = ((counts + tr - 1) // tr) * tr
    ends = jnp.cumsum(padded)
    starts = ends - padded
    pos = jnp.sum(onehot * (starts[None] + csum - 1), axis=1)
    npad = 2 * t + MOE_EXPERTS * tr
    ntiles = npad // tr
    tile_e = jnp.minimum(jnp.searchsorted(ends, jnp.arange(ntiles, dtype=jnp.int32) * tr, side="right"),
                         MOE_EXPERTS - 1).astype(jnp.int32)
    order = jnp.argsort(e, stable=True).astype(jnp.int32)
    row = jnp.arange(npad, dtype=jnp.int32)
    row_e = jnp.repeat(tile_e, tr)
    within = row - starts[row_e]
    dense_start = jnp.cumsum(counts) - counts
    valid = within < counts[row_e]
    src = jnp.where(valid, order[jnp.clip(dense_start[row_e] + within, 0, 2 * t - 1)] // 2, 0)
    n_used = (ends[-1] // tr).reshape(1).astype(jnp.int32)
    return src.astype(jnp.int32), tile_e, n_used, pos.astype(jnp.int32)


def _row_gather_start(idx_ref, first, n, src_hbm, dst, sem):
    def body(j, carry):
        pltpu.make_async_copy(src_hbm.at[pl.ds(idx_ref[first + j], 1)], dst.at[pl.ds(j, 1)], sem).start()
        return carry
    lax.fori_loop(0, n, body, 0, unroll=8)


def _row_gather_wait(n, src_hbm, dst, sem):
    pltpu.make_async_copy(src_hbm.at[pl.ds(0, n)], dst, sem).wait()


def _moe_ffn_kernel(src_ref, te_ref, nu_ref, x_hbm, wgu_ref, wdn_ref, o_ref, buf, sem):
    i = pl.program_id(0)
    n_used = nu_ref[0]
    tr = MOE_TR
    slot = lax.rem(i, 2)

    @pl.when(i == 0)
    def _():
        _row_gather_start(src_ref, 0, tr, x_hbm, buf.at[0], sem.at[0])

    @pl.when(i < n_used)
    def _():
        @pl.when(i + 1 < n_used)
        def _():
            _row_gather_start(src_ref, (i + 1) * tr, tr, x_hbm, buf.at[1 - slot], sem.at[1 - slot])

        _row_gather_wait(tr, x_hbm, buf.at[slot], sem.at[slot])
        x_lo, x_hi = _unpack_bf16_pairs(buf[slot])
        half = x_lo.shape[1]
        gu = (_dot(x_lo.astype(BF16), wgu_ref[:half, :].astype(BF16))
              + _dot(x_hi.astype(BF16), wgu_ref[half:, :].astype(BF16)))
        hid = _silu(gu[:, :MOE_HIDDEN]) * gu[:, MOE_HIDDEN:]
        o_ref[...] = _pack_bf16_pairs(_dot(hid.astype(BF16), wdn_ref[...].astype(BF16)))

    @pl.when(i >= n_used)
    def _():
        o_ref[...] = jnp.zeros(o_ref.shape, o_ref.dtype)


def moe_ffn(xn, w_gu, w_dn, l, src, tile_e, n_used):
    t = xn.shape[0]
    d = 2 * xn.shape[1]
    tr = MOE_TR
    npad = src.shape[0]
    f = MOE_HIDDEN
    grid_spec = pltpu.PrefetchScalarGridSpec(
        num_scalar_prefetch=3,
        grid=(npad // tr,),
        in_specs=[pl.BlockSpec(memory_space=pl.ANY),
                  pl.BlockSpec((None, None, d, 2 * f), lambda i, s, te, nu: (l, te[i], 0, 0)),
                  pl.BlockSpec((None, None, f, d), lambda i, s, te, nu: (l, te[i], 0, 0))],
        out_specs=pl.BlockSpec((tr, d // 2), lambda i, s, te, nu: (i, 0)),
        scratch_shapes=[pltpu.VMEM((2, tr, d // 2), jnp.uint32), pltpu.SemaphoreType.DMA((2,))],
    )
    return pl.pallas_call(
        _moe_ffn_kernel,
        grid_spec=grid_spec,
        out_shape=jax.ShapeDtypeStruct((npad, d // 2), jnp.uint32),
        compiler_params=_params(("arbitrary",), 4 * tr * d * 4 + 2 * 3 * d * f * 4 + 6 * tr * d * 4),
        name="moe_ffn",
    )(src, tile_e, n_used, xn, w_gu, w_dn)


def _moe_combine_kernel(pos_ref, y_hbm, x_ref, route_ref, gate_ref, o_ref, buf, sem):
    i = pl.program_id(0)
    n = pl.num_programs(0)
    tm = x_ref.shape[0]
    slot = lax.rem(i, 2)

    def start(tile, s):
        _row_gather_start(pos_ref, 2 * tm * tile, 2 * tm, y_hbm, buf.at[s], sem.at[s])

    @pl.when(i == 0)
    def _():
        start(0, 0)

    @pl.when(i + 1 < n)
    def _():
        start(i + 1, 1 - slot)

    _row_gather_wait(2 * tm, y_hbm, buf.at[slot], sem.at[slot])
    route = route_ref[...]
    lane = lax.broadcasted_iota(jnp.int32, route.shape, 1)
    w1 = _lane_pick(route, lane, 2)
    w2 = _lane_pick(route, lane, 3)
    cur = buf.at[slot]
    half = cur.shape[1]
    cw = min(4 * LANE, half)
    for c0 in range(0, half, cw):
        first = _unpack_bf16_pairs(cur[:tm, c0:c0 + cw])
        second = _unpack_bf16_pairs(cur[tm:, c0:c0 + cw])
        for part in (0, 1):
            cols = slice(part * half + c0, part * half + c0 + cw)
            o_ref[:, cols] = x_ref[:, cols] + gate_ref[:, cols] * (w1 * first[part] + w2 * second[part])


def moe_combine(y, x, route, gate, pos, row0, tm=128):
    t, d = x.shape
    pos = pos[2 * row0:2 * (row0 + t)].reshape(t // tm, tm, 2).transpose(0, 2, 1).reshape(-1)
    blk0 = row0 // tm
    grid_spec = pltpu.PrefetchScalarGridSpec(
        num_scalar_prefetch=1,
        grid=(t // tm,),
        in_specs=[pl.BlockSpec(memory_space=pl.ANY),
                  pl.BlockSpec((tm, d), lambda i, p: (i, 0)),
                  pl.BlockSpec((tm, LANE), lambda i, p: (i + blk0, 0)),
                  pl.BlockSpec((1, d), lambda i, p: (0, 0))],
        out_specs=pl.BlockSpec((tm, d), lambda i, p: (i, 0)),
        scratch_shapes=[pltpu.VMEM((2, 2 * tm, d // 2), jnp.uint32), pltpu.SemaphoreType.DMA((2,))],
    )
    return pl.pallas_call(
        _moe_combine_kernel,
        grid_spec=grid_spec,
        out_shape=jax.ShapeDtypeStruct((t, d), F32),
        compiler_params=_params(("arbitrary",), 4 * tm * d * 4 + 8 * tm * d * 4),
        name="moe_combine",
    )(pos, y, x, route, gate.reshape(1, d))


def _layer(l, x, xc, mod, last, consts, w):
    (g_mix, w_in, rpb, w_fn, w_g2, b_g, g_gla, w_out, g_ffn, w_rg, b_rg, w_re, b_re, w_gu, w_dn) = w
    cos, sin, cos_c, sin_c, perm, na_tab = consts
    d = D_MODEL
    s_len = x.shape[0]
    m_x = mod[l, 0].reshape(N_MOD, d)
    m_c = mod[l, 1].reshape(N_MOD, d)

    xn, lr = norm_proj(x, g_mix[l], m_x[0], m_x[1], w_in, l, OFF_LR, 2 * GLA_GATE_RANK)
    xcn, lr_c = norm_proj(xc, g_mix[l], m_c[0], m_c[1], w_in, l, OFF_LR, 2 * GLA_GATE_RANK)

    proj = []
    for a in (xn, xcn):
        u_na = matmul(a, w_in, (l,), OFF_NA, 3 * NA_WIDTH, 512, 1024, BF16, name="in_na")
        u_fn = matmul(a, w_in, (l,), OFF_FN, FN_WIDTH, 512, 1024, BF16, name="in_fn")
        u_qk = gla_qk_proj(a, w_in, l)
        u_vr = matmul(a, w_in, (l,), OFF_GVR, 2 * GLA_VW, 512, 1024, BF16, name="in_gvr")
        proj.append((u_na, u_fn, u_qk, u_vr))
    (u_na, u_fn, u_qk, u_vr), (c_na, c_fn, c_qk, c_vr) = proj

    r_ = GLA_GATE_RANK
    w2p = jnp.stack([jnp.pad(_pad_heads(w_g2[l, dr]), ((dr * r_, (1 - dr) * r_), (0, 0))) for dr in (0, 1)])
    bgp = _pad_heads(b_g[l]).reshape(2, 1, -1)
    zero = jnp.zeros((2, GLA_HEADS, GLA_DV, GLA_DKP), F32)
    *outs_c, s_c = gla_scan(c_qk, c_vr, lr_c, cos_c, sin_c, w2p, bgp, perm, zero)
    *outs_x, _ = gla_scan(u_qk, u_vr, lr, cos, sin, w2p, bgp, perm, s_c)

    mix = (neighbourhood_attention(u_na, c_na, na_tab, l),
           fourier_latent(u_fn, w_fn[l]),
           gla_output(outs_x[0], outs_x[1], u_vr, g_gla[l]))
    x = matmul(mix, w_out, (l,), 0, d, 512, 1024, F32, res=x, gate=m_x[2], name="out_proj")

    w_route = jnp.pad(jnp.concatenate([w_re[l], w_rg[l]], axis=1),
                      ((0, 0), (0, LANE - MOE_EXPERTS - MOE_GROUPS)))
    b_route = jnp.pad(jnp.concatenate([b_re[l], b_rg[l]]), (0, LANE - MOE_EXPERTS - MOE_GROUPS)).reshape(1, LANE)
    streams = [(x, m_x)]
    if not last:
        mix_c = (full_attention(c_na), fourier_context(c_fn, w_fn[l]),
                 gla_output(outs_c[0], outs_c[1], c_vr, g_gla[l]))
        xc = matmul(mix_c, w_out, (l,), 0, d, 512, 1024, F32, res=xc, gate=m_c[2], name="out_proj_ctx")
        streams.append((xc, m_c))

    yn, route = norm_route([(y, m[3], m[4]) for y, m in streams], g_ffn[l], w_route, b_route)
    src, tile_e, n_used, pos = moe_plan(route)
    y_exp = moe_ffn(yn, w_gu, w_dn, l, src, tile_e, n_used)
    outs, row0 = [], 0
    for y, m in streams:
        outs.append(moe_combine(y_exp, y, route, m[5], pos, row0))
        row0 += y.shape[0]
    return outs[0], (outs[1] if not last else xc)


def kernel(x, c, ctx, c_ctx, w_ada, b_ada, g_mix, w_in, rpb, w_fn, w_g2, b_g, g_gla, w_out,
           g_ffn, w_rg, b_rg, w_re, b_re, w_gu, w_dn, g_final):
    assert x.shape[0] == 1 and c.shape[0] == 1
    d = D_MODEL
    s_len = x.shape[1]
    c_len = ctx.shape[1]
    cc = jnp.concatenate([c, c_ctx[None], jnp.zeros((6, d), F32)], axis=0)
    mod = ada_mod(cc, w_ada, b_ada)
    cos, sin = rope_tables(s_len)
    cos_c = jnp.ones((c_len, GLA_DKP), F32)
    sin_c = jnp.zeros((c_len, GLA_DKP), F32)
    consts = (cos, sin, cos_c, sin_c, rope_perm(), na_tables(rpb, s_len // GRID_W))
    w = (g_mix, w_in, rpb, w_fn, w_g2, b_g, g_gla, w_out, g_ffn, w_rg, b_rg, w_re, b_re, w_gu, w_dn)
    xs, xc = x[0], ctx[0]
    for l in range(DEPTH):
        xs, xc = _layer(l, xs, xc, mod, l == DEPTH - 1, consts, w)
    return final_norm(xs, g_final)[None]
```

```python
import functools
import math

import numpy as np
import jax
import jax.numpy as jnp
from jax import lax
from jax.experimental import pallas as pl
from jax.experimental.pallas import tpu as pltpu

F32 = jnp.float32
BF16 = jnp.bfloat16

D_MODEL = 4096
DEPTH = 4
GRID_W = 64
EPS = 1e-6
N_MOD = 6

NA_HEADS = 12
NA_HEAD_DIM = 128
NA_WIN_H = 8
NA_WIN_W = 16
NA_WIDTH = NA_HEADS * NA_HEAD_DIM

FN_GROUPS = 8
FN_GROUP_DIM = 128
FN_WIDTH = FN_GROUPS * FN_GROUP_DIM

GLA_HEADS = 4
GLA_DK = 192
GLA_DKP = 256
GLA_DV = 384
GLA_KW = GLA_HEADS * GLA_DK
GLA_VW = GLA_HEADS * GLA_DV
GLA_GATE_RANK = 16
GLA_GATE_TEMP = 16.0
GLA_CHUNK = 64
GLA_SUB = 8
ROPE_THETA = 10000.0

MOE_GROUPS = 4
MOE_EXPERTS_PER_GROUP = 8
MOE_EXPERTS = MOE_GROUPS * MOE_EXPERTS_PER_GROUP
MOE_HIDDEN = 192

OFF_NA = 0
OFF_FN = 3 * NA_WIDTH
OFF_GQK = OFF_FN + FN_WIDTH
OFF_GVR = OFF_GQK + 2 * GLA_KW
OFF_LR = OFF_GVR + 2 * GLA_VW

LANE = 128
NEG = -1e30
V7X_VMEM_BUDGET = 56 * 1024 * 1024


def _params(sem, vmem_bytes):
    return pltpu.CompilerParams(dimension_semantics=sem,
                                vmem_limit_bytes=int(min(max(vmem_bytes, 16 << 20), V7X_VMEM_BUDGET)))


def _dot(a, b):
    return jnp.dot(a, b, preferred_element_type=F32)


def _dot_nt(a, b):
    return lax.dot_general(a, b, (((1,), (1,)), ((), ())), preferred_element_type=F32)


def _dot_tn(a, b):
    return lax.dot_general(a, b, (((0,), (0,)), ((), ())), preferred_element_type=F32)


def _silu(x):
    return x * (1.0 / (1.0 + jnp.exp(-x)))


def _ada_kernel(c_ref, w_ref, b_ref, o_ref):
    s = _silu(c_ref[...]).astype(BF16)
    o_ref[0] = _dot(s, w_ref[0].astype(BF16)) + b_ref[0]


def ada_mod(cc, w_ada, b_ada, tn=512):
    n, d, nd = w_ada.shape
    return pl.pallas_call(
        _ada_kernel,
        grid=(n, nd // tn),
        in_specs=[pl.BlockSpec((8, d), lambda l, j: (0, 0)),
                  pl.BlockSpec((1, d, tn), lambda l, j: (l, 0, j)),
                  pl.BlockSpec((1, 1, tn), lambda l, j: (l, 0, j))],
        out_specs=pl.BlockSpec((1, 8, tn), lambda l, j: (l, 0, j)),
        out_shape=jax.ShapeDtypeStruct((n, 8, nd), F32),
        compiler_params=_params(("arbitrary", "arbitrary"), 3 * d * tn * 4 + (4 << 20)),
        name="ada_mod",
    )(cc, w_ada, b_ada.reshape(n, 1, nd))


def _normed(x_ref, g_ref, sh_ref, sc_ref):
    x = x_ref[...]
    y = x * lax.rsqrt(jnp.mean(x * x, axis=-1, keepdims=True) + EPS)
    return (y * g_ref[...]) * (1.0 + sc_ref[...]) + sh_ref[...]


def _norm_proj_kernel(*refs, blocks, layer, col0):
    ns = len(blocks)
    x_refs, g_ref = refs[:ns], refs[ns]
    mods = refs[ns + 1:ns + 1 + 2 * ns]
    w_hbm, xn_ref, p_ref, w_buf, sem = refs[ns + 1 + 2 * ns:]
    i = pl.program_id(0)

    @pl.when(i == 0)
    def _():
        cp = pltpu.make_async_copy(w_hbm.at[layer, pl.ds(col0, w_buf.shape[0]), :], w_buf, sem)
        cp.start()
        cp.wait()

    first = 0
    for s_, nb in enumerate(blocks):
        @pl.when((i >= first) & (i < first + nb))
        def _(s_=s_):
            hb = _normed(x_refs[s_], g_ref, mods[2 * s_], mods[2 * s_ + 1]).astype(BF16)
            xn_ref[...] = hb
            p_ref[...] = _dot_nt(hb, w_buf[...].astype(BF16))
        first += nb


def _lane_pick(v, lane, idx):
    return jnp.sum(jnp.where(lane == idx, v, 0.0), axis=-1, keepdims=True)


def _pack_bf16_pairs(x):
    xf = x.astype(BF16).astype(F32)
    half = xf.shape[1] // 2
    lo = lax.bitcast_convert_type(xf[:, :half], jnp.uint32)
    hi = lax.bitcast_convert_type(xf[:, half:], jnp.uint32)
    return hi | (lo >> 16)


def _unpack_bf16_pairs(p):
    lo = lax.bitcast_convert_type(p << 16, F32)
    hi = lax.bitcast_convert_type(p & jnp.uint32(0xFFFF0000), F32)
    return lo, hi


def _norm_route_kernel(x_ref, g_ref, sh_ref, sc_ref, w_ref, b_ref, xn_ref, route_ref):
    h = _normed(x_ref, g_ref, sh_ref, sc_ref)
    hb = h.astype(BF16)
    xn_ref[...] = _pack_bf16_pairs(h)
    logits = _dot(hb, w_ref[...].astype(BF16)) + b_ref[...]
    lane = lax.broadcasted_iota(jnp.int32, logits.shape, 1).astype(F32)
    far = float(4 * LANE)
    is_g = (lane >= MOE_EXPERTS) & (lane < MOE_EXPERTS + MOE_GROUPS)
    gl = jnp.where(is_g, logits, NEG)
    gmax = jnp.max(gl, axis=-1, keepdims=True)
    g_top = jnp.min(jnp.where(gl == gmax, lane, far), axis=-1, keepdims=True) - MOE_EXPERTS
    g_w = 1.0 / jnp.sum(jnp.where(is_g, jnp.exp(gl - gmax), 0.0), axis=-1, keepdims=True)
    in_grp = (lane >= g_top * MOE_EXPERTS_PER_GROUP) & (lane < (g_top + 1) * MOE_EXPERTS_PER_GROUP)
    el = jnp.where(in_grp, logits, NEG)
    v1 = jnp.max(el, axis=-1, keepdims=True)
    i1 = jnp.min(jnp.where(el == v1, lane, far), axis=-1, keepdims=True)
    el2 = jnp.where(lane == i1, NEG, el)
    v2 = jnp.max(el2, axis=-1, keepdims=True)
    i2 = jnp.min(jnp.where(el2 == v2, lane, far), axis=-1, keepdims=True)
    e2 = jnp.exp(v2 - v1)
    w1 = g_w / (1.0 + e2)
    w2 = g_w * e2 / (1.0 + e2)
    route_ref[...] = (jnp.where(lane == 0.0, i1, 0.0) + jnp.where(lane == 1.0, i2, 0.0)
                      + jnp.where(lane == 2.0, w1, 0.0) + jnp.where(lane == 3.0, w2, 0.0))


def _norm_route_streams_kernel(*refs, blocks):
    ns = len(blocks)
    x_refs, g_ref = refs[:ns], refs[ns]
    mods = refs[ns + 1:ns + 1 + 2 * ns]
    w_ref, b_ref, xn_ref, route_ref = refs[ns + 1 + 2 * ns:]
    i = pl.program_id(0)
    first = 0
    for s_, nb in enumerate(blocks):
        @pl.when((i >= first) & (i < first + nb))
        def _(s_=s_):
            _norm_route_kernel(x_refs[s_], g_ref, mods[2 * s_], mods[2 * s_ + 1], w_ref, b_ref, xn_ref, route_ref)
        first += nb


def _stream_specs(streams, tm, d):
    blocks = tuple(x.shape[0] // tm for x, _, _ in streams)
    ins, args, first = [], [], 0
    for (x, _, _), nb in zip(streams, blocks):
        ins.append(pl.BlockSpec((tm, d), lambda i, first=first, nb=nb: (jnp.clip(i - first, 0, nb - 1), 0)))
        args.append(x)
        first += nb
    return blocks, ins, args


def norm_proj(streams, g, w, layer, col0, ncols, tm=256):
    d = streams[0][0].shape[1]
    blocks, ins, args = _stream_specs(streams, tm, d)
    vec = pl.BlockSpec((1, d), lambda i: (0, 0))
    ins.append(vec)
    args.append(g.reshape(1, d))
    for _, sh, sc in streams:
        ins += [vec, vec]
        args += [sh.reshape(1, d), sc.reshape(1, d)]
    ins.append(pl.BlockSpec(memory_space=pl.ANY))
    args.append(w)
    rows = sum(blocks) * tm
    return pl.pallas_call(
        functools.partial(_norm_proj_kernel, blocks=blocks, layer=layer, col0=col0),
        grid=(sum(blocks),),
        in_specs=ins,
        out_specs=[pl.BlockSpec((tm, d), lambda i: (i, 0)), pl.BlockSpec((tm, ncols), lambda i: (i, 0))],
        out_shape=[jax.ShapeDtypeStruct((rows, d), BF16), jax.ShapeDtypeStruct((rows, ncols), F32)],
        scratch_shapes=[pltpu.VMEM((ncols, d), F32), pltpu.SemaphoreType.DMA(())],
        compiler_params=_params(("arbitrary",), (4 + 4 * len(streams)) * tm * d * 4 + 2 * d * LANE * 4),
        name="norm_proj",
    )(*args)


def norm_route(streams, g, w_small, b_small, tm=256):
    d = streams[0][0].shape[1]
    blocks, ins, args = _stream_specs(streams, tm, d)
    row = lambda i: (0, 0)
    vec = pl.BlockSpec((1, d), row)
    ins.append(vec)
    args.append(g.reshape(1, d))
    for _, sh, sc in streams:
        ins += [vec, vec]
        args += [sh.reshape(1, d), sc.reshape(1, d)]
    ins += [pl.BlockSpec((d, LANE), row), pl.BlockSpec((1, LANE), row)]
    args += [w_small, b_small]
    rows = sum(blocks) * tm
    return pl.pallas_call(
        functools.partial(_norm_route_streams_kernel, blocks=blocks),
        grid=(sum(blocks),),
        in_specs=ins,
        out_specs=[pl.BlockSpec((tm, d // 2), lambda i: (i, 0)), pl.BlockSpec((tm, LANE), lambda i: (i, 0))],
        out_shape=[jax.ShapeDtypeStruct((rows, d // 2), jnp.uint32), jax.ShapeDtypeStruct((rows, LANE), F32)],
        compiler_params=_params(("arbitrary",), (4 + 4 * len(streams)) * tm * d * 4 + 4 * d * LANE * 4),
        name="norm_route",
    )(*args)


def _final_norm_kernel(x_ref, g_ref, o_ref):
    x = x_ref[...]
    o_ref[...] = x * lax.rsqrt(jnp.mean(x * x, axis=-1, keepdims=True) + EPS) * g_ref[...]


def final_norm(x, g, tm=256):
    t, d = x.shape
    return pl.pallas_call(
        _final_norm_kernel,
        grid=(t // tm,),
        in_specs=[pl.BlockSpec((tm, d), lambda i: (i, 0)), pl.BlockSpec((1, d), lambda i: (0, 0))],
        out_specs=pl.BlockSpec((tm, d), lambda i: (i, 0)),
        out_shape=jax.ShapeDtypeStruct((t, d), F32),
        compiler_params=_params(("arbitrary",), 6 * tm * d * 4),
        name="final_norm",
    )(x, g.reshape(1, d))


def _mm_kernel(a_ref, w_ref, o_ref):
    o_ref[...] = _dot(a_ref[...], w_ref[...].astype(BF16)).astype(o_ref.dtype)


def _mm_wt_kernel(a_ref, wt_ref, o_ref):
    o_ref[...] = _dot_nt(a_ref[...], wt_ref[...].astype(BF16)).astype(o_ref.dtype)


def _mm_res_kernel(a_ref, w_ref, r_ref, g_ref, o_ref):
    o_ref[...] = r_ref[...] + g_ref[...] * _dot(a_ref[...], w_ref[...].astype(BF16))


def _mm_parts_res_kernel(*refs, widths):
    n = len(widths)
    w_ref, r_ref, g_ref, o_ref = refs[n:]
    acc, k0 = None, 0
    for a_ref, wd in zip(refs[:n], widths):
        part = _dot(a_ref[...].astype(BF16), w_ref[k0:k0 + wd, :].astype(BF16))
        acc = part if acc is None else acc + part
        k0 += wd
    o_ref[...] = r_ref[...] + g_ref[...] * acc


def _mm_head_pad_kernel(a_ref, w0_ref, w1_ref, w2_ref, o_ref):
    a = a_ref[...]
    acc = jnp.concatenate([_dot_nt(a, w_ref[...].astype(BF16)) for w_ref in (w0_ref, w1_ref, w2_ref)], axis=1)
    pad = jnp.zeros((a.shape[0], GLA_DKP - GLA_DK), o_ref.dtype)
    for h in range(GLA_HEADS):
        o_ref[:, h * GLA_DKP:h * GLA_DKP + GLA_DK] = acc[:, h * GLA_DK:(h + 1) * GLA_DK].astype(o_ref.dtype)
        o_ref[:, h * GLA_DKP + GLA_DK:(h + 1) * GLA_DKP] = pad


def gla_qk_proj(a, w_in_t, l, tm=1024):
    t, k = a.shape
    tm = min(tm, t)
    tn = GLA_KW // 3
    j0 = OFF_GQK // tn
    w_spec = lambda p: pl.BlockSpec((None, tn, k), lambda i, j: (l, j0 + 3 * j + p, 0))
    kwp = GLA_HEADS * GLA_DKP
    return pl.pallas_call(
        _mm_head_pad_kernel,
        grid=(t // tm, 2),
        in_specs=[pl.BlockSpec((tm, k), lambda i, j: (i, 0)), w_spec(0), w_spec(1), w_spec(2)],
        out_specs=pl.BlockSpec((tm, kwp), lambda i, j: (i, j)),
        out_shape=jax.ShapeDtypeStruct((t, 2 * kwp), BF16),
        compiler_params=_params(("arbitrary", "arbitrary"), 2 * tm * k * 2 + 7 * k * tn * 4 + 8 * tm * kwp * 4),
        name="in_gqk",
    )(a, w_in_t, w_in_t, w_in_t)


def matmul(a, w, w_lead, col0, ncols, tn, tm, out_dtype, res=None, gate=None, w_transposed=False, name="mm"):
    parts = a if isinstance(a, (tuple, list)) else (a,)
    t = parts[0].shape[0]
    k = sum(p.shape[1] for p in parts)
    tm = min(tm, t)
    nlead = len(w_lead)
    j0 = col0 // tn
    if w_transposed:
        w_spec = pl.BlockSpec((None,) * nlead + (tn, k), lambda i, j: tuple(w_lead) + (j + j0, 0))
    else:
        w_spec = pl.BlockSpec((None,) * nlead + (k, tn), lambda i, j: tuple(w_lead) + (0, j + j0))
    ins = [pl.BlockSpec((tm, p.shape[1]), lambda i, j: (i, 0)) for p in parts] + [w_spec]
    args = list(parts) + [w]
    wbytes = jnp.dtype(w.dtype).itemsize
    vmem = 2 * tm * k * 2 + 3 * k * tn * wbytes + 6 * tm * tn * 4
    if res is None:
        body = _mm_wt_kernel if w_transposed else _mm_kernel
    else:
        assert not w_transposed
        body = _mm_res_kernel if len(parts) == 1 else functools.partial(
            _mm_parts_res_kernel, widths=tuple(p.shape[1] for p in parts))
        ins += [pl.BlockSpec((tm, tn), lambda i, j: (i, j)), pl.BlockSpec((1, tn), lambda i, j: (0, j))]
        args += [res, gate.reshape(1, ncols)]
    return pl.pallas_call(
        body,
        grid=(t // tm, ncols // tn),
        in_specs=ins,
        out_specs=pl.BlockSpec((tm, tn), lambda i, j: (i, j)),
        out_shape=jax.ShapeDtypeStruct((t, ncols), out_dtype),
        compiler_params=_params(("arbitrary", "arbitrary"), vmem),
        name=name,
    )(*args)


NA_QROWS = 4
NA_SLAB = 12
NA_HEADS_PER_STEP = 4


def na_tables(rpb, rows):
    nblk = rows // NA_QROWS
    n_, h_ = rpb.shape[:2]
    qc = np.arange(GRID_W)[:, None]
    kc = np.arange(GRID_W)[None, :]
    c0 = np.clip(qc - NA_WIN_W // 2, 0, GRID_W - NA_WIN_W)
    ok_c = (kc >= c0) & (kc < c0 + NA_WIN_W)
    a = np.arange(NA_QROWS)[:, None]
    b = np.arange(NA_SLAB)[None, :]
    dr_l, ok_l = [], []
    for i in (0, 1, nblk - 1):
        base = int(np.clip(i * NA_QROWS - NA_WIN_H // 2, 0, rows - NA_SLAB))
        r = i * NA_QROWS + a
        r0 = np.clip(r - NA_WIN_H // 2, 0, rows - NA_WIN_H)
        krow = base + b
        ok_l.append((krow >= r0) & (krow < r0 + NA_WIN_H))
        dr_l.append(np.clip(krow - r + NA_WIN_H - 1, 0, 2 * NA_WIN_H - 2))
    dr = np.stack(dr_l)
    ok = np.stack(ok_l)[:, :, None, :, None] & ok_c[None, None, :, None, :]
    ok = jnp.asarray(ok.reshape(3, NA_QROWS * GRID_W, NA_SLAB * GRID_W))
    lo = GRID_W - NA_WIN_W
    padded = jnp.pad(rpb.astype(F32), ((0, 0), (0, 0), (0, 0), (lo, lo)))
    by_col = jnp.stack([padded[..., GRID_W - 1 - q:2 * GRID_W - 1 - q] for q in range(GRID_W)], axis=3)
    variants = []
    for v in range(3):
        rows_ = [jnp.concatenate([by_col[:, :, int(dr[v, a_, b_])] for b_ in range(NA_SLAB)], axis=-1)
                 for a_ in range(NA_QROWS)]
        variants.append(jnp.concatenate(rows_, axis=2))
    bias = jnp.stack(variants, axis=1)
    return jnp.where(ok[None, :, None], bias, NEG)


def _na_kernel(q_ref, k_ref, v_ref, kc_ref, vc_ref, t_ref, o_ref, *, rows):
    i = pl.program_id(1)
    base = jnp.clip(i * NA_QROWS - NA_WIN_H // 2, 0, rows - NA_SLAB) * GRID_W
    base = pl.multiple_of(base, NA_QROWS * GRID_W)
    n_keys = NA_SLAB * GRID_W
    scale = NA_HEAD_DIM ** -0.5
    for hh in range(NA_HEADS_PER_STEP):
        cols = slice(hh * NA_HEAD_DIM, (hh + 1) * NA_HEAD_DIM)
        q = q_ref[:, cols]
        k = k_ref[pl.ds(base, n_keys), cols]
        v = v_ref[pl.ds(base, n_keys), cols]
        s = _dot_nt(q, k) * scale + t_ref[hh]
        sc = _dot_nt(q, kc_ref[:, cols]) * scale
        m = jnp.maximum(jnp.max(s, axis=-1, keepdims=True), jnp.max(sc, axis=-1, keepdims=True))
        p = jnp.exp(s - m)
        pc = jnp.exp(sc - m)
        den = jnp.sum(p, axis=-1, keepdims=True) + jnp.sum(pc, axis=-1, keepdims=True)
        o = _dot(p.astype(BF16), v) + _dot(pc.astype(BF16), vc_ref[:, cols])
        o_ref[:, cols] = (o / den).astype(o_ref.dtype)


def neighbourhood_attention(u, s, tables, l):
    c = u.shape[0] - s
    rows = s // GRID_W
    nblk = rows // NA_QROWS
    tq = NA_QROWS * GRID_W
    hps = NA_HEADS_PER_STEP
    ng = NA_HEADS // hps
    hw = hps * NA_HEAD_DIM

    def variant(i):
        return jnp.where(i == 0, 0, jnp.where(i == nblk - 1, 2, 1))

    return pl.pallas_call(
        functools.partial(_na_kernel, rows=rows),
        grid=(ng, nblk),
        in_specs=[pl.BlockSpec((tq, hw), lambda h, i: (i, h)),
                  pl.BlockSpec((s, hw), lambda h, i: (0, ng + h)),
                  pl.BlockSpec((s, hw), lambda h, i: (0, 2 * ng + h)),
                  pl.BlockSpec((c, hw), lambda h, i: (s // c, ng + h)),
                  pl.BlockSpec((c, hw), lambda h, i: (s // c, 2 * ng + h)),
                  pl.BlockSpec((None, None, hps, tq, NA_SLAB * GRID_W), lambda h, i: (l, variant(i), h, 0, 0))],
        out_specs=pl.BlockSpec((tq, hw), lambda h, i: (i, h)),
        out_shape=jax.ShapeDtypeStruct((s, NA_WIDTH), BF16),
        compiler_params=_params(("arbitrary", "arbitrary"),
                                4 * s * hw * 2 + 2 * hps * tq * NA_SLAB * GRID_W * 4 + (12 << 20)),
        name="na_latent",
    )(u, u, u, u, u, tables)


def _full_attn_kernel(q_ref, k_ref, v_ref, o_ref):
    s = _dot_nt(q_ref[...], k_ref[...]) * NA_HEAD_DIM ** -0.5
    p = jnp.exp(s - jnp.max(s, axis=-1, keepdims=True))
    den = jnp.sum(p, axis=-1, keepdims=True)
    o_ref[...] = (_dot(p.astype(BF16), v_ref[...]) / den).astype(o_ref.dtype)


def full_attention(u, s):
    c = u.shape[0] - s
    b0 = s // c
    h_ = NA_HEADS
    hd = NA_HEAD_DIM
    return pl.pallas_call(
        _full_attn_kernel,
        grid=(h_,),
        in_specs=[pl.BlockSpec((c, hd), lambda h: (b0, h)),
                  pl.BlockSpec((c, hd), lambda h: (b0, h_ + h)),
                  pl.BlockSpec((c, hd), lambda h: (b0, 2 * h_ + h))],
        out_specs=pl.BlockSpec((c, hd), lambda h: (0, h)),
        out_shape=jax.ShapeDtypeStruct((c, NA_WIDTH), BF16),
        compiler_params=_params(("arbitrary",), 16 << 20),
        name="na_context",
    )(u, u, u)


def _dft_cs(n):
    idx = np.arange(n)
    ang = 2.0 * np.pi * ((idx[:, None] * idx[None, :]) % n) / n
    return np.cos(ang), np.sin(ang)


FN_ROWS = 8


def _fn1_kernel(w_ref, x_ref, tr_ref, ti_ref, z_ref, *, r):
    for s in range(FN_ROWS):
        y = _dot(w_ref[...], x_ref[:, s, :].astype(BF16))
        yr, yi = y[:r], y[r:]
        tr, ti = tr_ref[s], ti_ref[s]
        z_ref[pl.ds(0, r), s, :] = yr * tr - yi * ti
        z_ref[pl.ds(r, r), s, :] = yr * ti + yi * tr


def _fn2_kernel(zr_ref, zi_ref, kc_ref, ks_ref, c_ref, s_ref, w_ref, o_ref, *, scale):
    zr, zi = zr_ref[...].astype(BF16), zi_ref[...].astype(BF16)
    kc, ks = kc_ref[...], ks_ref[...]
    xr = (_dot(kc, zr) + _dot(ks, zi)).astype(BF16)
    xi = (_dot(kc, zi) - _dot(ks, zr)).astype(BF16)
    cw = GRID_W
    for g in range(FN_GROUPS):
        lo = g * FN_GROUP_DIM
        f = _dot(xr[:, lo:lo + FN_GROUP_DIM], c_ref[...]) + _dot(xi[:, lo:lo + FN_GROUP_DIM], s_ref[...])
        y = _dot((f * scale).astype(BF16), w_ref[g].astype(BF16))
        for kk in range(FN_ROWS):
            o_ref[:, kk, lo:lo + FN_GROUP_DIM] = y[kk * cw:(kk + 1) * cw]


def fourier_latent(u, l, w_fn):
    cw = GRID_W
    r = l // cw
    c_r, s_r = _dft_cs(r)
    w1 = jnp.asarray(np.concatenate([c_r, -s_r], axis=0), BF16)
    ang = 2.0 * np.pi * (np.arange(cw)[:, None] * np.arange(r)[None, :]) / l
    tr = jnp.asarray(np.cos(ang)[:, :, None], F32)
    ti = jnp.asarray(-np.sin(ang)[:, :, None], F32)
    z = pl.pallas_call(
        functools.partial(_fn1_kernel, r=r),
        grid=(cw // FN_ROWS,),
        in_specs=[pl.BlockSpec((2 * r, r), lambda j: (0, 0)),
                  pl.BlockSpec((r, FN_ROWS, FN_WIDTH), lambda j: (0, j, 0)),
                  pl.BlockSpec((FN_ROWS, r, 1), lambda j: (j, 0, 0)),
                  pl.BlockSpec((FN_ROWS, r, 1), lambda j: (j, 0, 0))],
        out_specs=pl.BlockSpec((2 * r, FN_ROWS, FN_WIDTH), lambda j: (0, j, 0)),
        out_shape=jax.ShapeDtypeStruct((2 * r, cw, FN_WIDTH), F32),
        compiler_params=_params(("arbitrary",), 40 << 20),
        name="fourier_stage1",
    )(w1, u.reshape(u.shape[0] // cw, cw, FN_WIDTH), tr, ti)
    z2 = z.reshape(2 * r * cw, FN_WIDTH)
    c_w, s_w = _dft_cs(cw)
    eye = np.eye(FN_ROWS)
    kc = jnp.asarray(np.kron(eye, c_w), BF16)
    ks = jnp.asarray(np.kron(eye, s_w), BF16)
    c_c, s_c = _dft_cs(FN_GROUP_DIM)
    nb = r // FN_ROWS
    tb = FN_ROWS * cw
    out = pl.pallas_call(
        functools.partial(_fn2_kernel, scale=float((l * FN_GROUP_DIM) ** -0.5)),
        grid=(nb,),
        in_specs=[pl.BlockSpec((tb, FN_WIDTH), lambda b: (b, 0)),
                  pl.BlockSpec((tb, FN_WIDTH), lambda b: (nb + b, 0)),
                  pl.BlockSpec((tb, tb), lambda b: (0, 0)),
                  pl.BlockSpec((tb, tb), lambda b: (0, 0)),
                  pl.BlockSpec((FN_GROUP_DIM, FN_GROUP_DIM), lambda b: (0, 0)),
                  pl.BlockSpec((FN_GROUP_DIM, FN_GROUP_DIM), lambda b: (0, 0)),
                  pl.BlockSpec((FN_GROUPS, FN_GROUP_DIM, FN_GROUP_DIM), lambda b: (0, 0, 0))],
        out_specs=pl.BlockSpec((cw, FN_ROWS, FN_WIDTH), lambda b: (0, b, 0)),
        out_shape=jax.ShapeDtypeStruct((cw, r, FN_WIDTH), F32),
        compiler_params=_params(("arbitrary",), 32 << 20),
        name="fourier_stage2",
    )(z2, z2, kc, ks, jnp.asarray(c_c, BF16), jnp.asarray(s_c, BF16), w_fn)
    return out.reshape(l, FN_WIDTH)


def _fn_ctx_kernel(u_ref, cl_ref, sl_ref, c_ref, s_ref, w_ref, o_ref, *, scale):
    u = u_ref[...].astype(BF16)
    gr = _dot(cl_ref[...], u).astype(BF16)
    gi = (-_dot(sl_ref[...], u)).astype(BF16)
    for g in range(FN_GROUPS):
        lo = g * FN_GROUP_DIM
        f = _dot(gr[:, lo:lo + FN_GROUP_DIM], c_ref[...]) + _dot(gi[:, lo:lo + FN_GROUP_DIM], s_ref[...])
        o_ref[:, lo:lo + FN_GROUP_DIM] = _dot((f * scale).astype(BF16), w_ref[g].astype(BF16)).astype(o_ref.dtype)


def fourier_context(u, s, w_fn):
    c = u.shape[0] - s
    c_l, s_l = _dft_cs(c)
    c_c, s_c = _dft_cs(FN_GROUP_DIM)
    gd = FN_GROUP_DIM
    whole = lambda shape: pl.BlockSpec(shape, lambda i: (0,) * len(shape))
    return pl.pallas_call(
        functools.partial(_fn_ctx_kernel, scale=float((c * FN_GROUP_DIM) ** -0.5)),
        grid=(1,),
        in_specs=[pl.BlockSpec((c, FN_WIDTH), lambda i: (s // c, 0)), whole((c, c)), whole((c, c)),
                  whole((gd, gd)), whole((gd, gd)), whole((FN_GROUPS, gd, gd))],
        out_specs=whole((c, FN_WIDTH)),
        out_shape=jax.ShapeDtypeStruct((c, FN_WIDTH), BF16),
        compiler_params=_params(("arbitrary",), 16 << 20),
        name="fourier_context",
    )(u, jnp.asarray(c_l, BF16), jnp.asarray(s_l, BF16), jnp.asarray(c_c, BF16), jnp.asarray(s_c, BF16), w_fn)


def _split_hi_lo(x):
    hi = x.astype(BF16)
    return hi, (x - hi.astype(F32)).astype(BF16)


def _gla_chunk(q_ref, k_ref, v_ref, lr_ref, cos_ref, sin_ref, w2_ref, bg_ref, perm_ref, st_ref, o_ref, rev):
    cs = GLA_CHUNK
    sub = GLA_SUB
    nsub = cs // sub
    z = _dot(lr_ref[...].astype(BF16), w2_ref[...].astype(BF16)) + bg_ref[...]
    la = -(jnp.maximum(-z, 0.0) + jnp.log(1.0 + jnp.exp(-jnp.abs(z)))) * (1.0 / GLA_GATE_TEMP)
    ri = lax.broadcasted_iota(jnp.int32, (cs, cs), 0)
    ci = lax.broadcasted_iota(jnp.int32, (cs, cs), 1)
    tri = jnp.where((ci >= ri) if rev else (ci <= ri), 1.0, 0.0).astype(BF16)
    la_hi, la_lo = _split_hi_lo(la)
    bcum = _dot(tri, la_hi) + _dot(tri, la_lo)
    edge = 0 if rev else cs - 1
    row_id = lax.broadcasted_iota(jnp.int32, (cs, GLA_DKP), 0)
    sub_r = lax.broadcasted_iota(jnp.int32, (sub, GLA_DKP), 0)
    lane_c = lax.broadcasted_iota(jnp.int32, (sub, cs), 1)
    cos, sin = cos_ref[...], sin_ref[...]
    perm = perm_ref[...]
    qscale = GLA_DK ** -0.5
    for h in range(GLA_HEADS):
        ks_ = slice(h * GLA_DKP, (h + 1) * GLA_DKP)
        vs_ = slice(h * GLA_DV, (h + 1) * GLA_DV)
        qb, kb = q_ref[:, ks_], k_ref[:, ks_]
        q = (qb.astype(F32) * cos + _dot(qb, perm) * sin) * qscale
        k = kb.astype(F32) * cos + _dot(kb, perm) * sin
        v = v_ref[:, vs_]
        b = bcum[:, ks_]
        b_edge = b[edge:edge + 1, :]
        st = st_ref[h]
        o = _dot_nt((q * jnp.exp(b)).astype(BF16), st.astype(BF16))
        slabs = []
        for blk in range(nsub):
            lo = blk * sub
            q_i = q[lo:lo + sub]
            b_i = b[lo:lo + sub]
            k_i = k[lo:lo + sub]
            acc = jnp.zeros((sub, cs), F32)
            if rev and blk < nsub - 1:
                ref_row = b[lo + sub:lo + sub + 1, :]
                outside = row_id >= lo + sub
            elif (not rev) and blk > 0:
                ref_row = b[lo - 1:lo, :]
                outside = row_id < lo
            else:
                ref_row = None
            if ref_row is not None:
                qe = q_i * jnp.exp(b_i - ref_row)
                ke = jnp.where(outside, k * jnp.exp(jnp.where(outside, ref_row - b, 0.0)), 0.0)
                acc = acc + _dot_nt(qe.astype(BF16), ke.astype(BF16))
            for j in range(sub):
                keep = (sub_r <= j) if rev else (sub_r >= j)
                d = jnp.exp(jnp.where(keep, b_i - b_i[j:j + 1, :], NEG))
                col = jnp.sum(q_i * d * k_i[j:j + 1, :], axis=-1, keepdims=True)
                acc = acc + jnp.where(lane_c == lo + j, col, 0.0)
            slabs.append(acc)
        attn = jnp.concatenate(slabs, axis=0)
        o = o + _dot(attn.astype(BF16), v)
        o_ref[:, vs_] = o
        kend = (k * jnp.exp(b_edge - b)).astype(BF16)
        st_ref[h] = st * jnp.exp(b_edge) + _dot_tn(v, kend)


def _gla_kernel(*refs, nchunks):
    fwd, bwd = refs[0:6], refs[6:12]
    w2f_ref, bgf_ref, w2b_ref, bgb_ref, perm_ref, s0_ref = refs[12:18]
    of_ref, ob_ref, sfin_ref, st_ref = refs[18:22]
    step = pl.program_id(0)

    @pl.when(step == 0)
    def _():
        st_ref[...] = s0_ref[...]

    _gla_chunk(*fwd, w2f_ref, bgf_ref, perm_ref, st_ref.at[0], of_ref, False)
    _gla_chunk(*bwd, w2b_ref, bgb_ref, perm_ref, st_ref.at[1], ob_ref, True)

    @pl.when(step == nchunks - 1)
    def _():
        sfin_ref[...] = st_ref[...]


def gla_scan(u_qk, u_vr, lr, cos, sin, w2p, bgp, perm, s0, row0, l):
    n = l // GLA_CHUNK
    c0 = row0 // GLA_CHUNK
    cs = GLA_CHUNK
    kw = GLA_HEADS * GLA_DKP
    rank2 = lr.shape[1]
    st_shape = (2, GLA_HEADS, GLA_DV, GLA_DKP)
    ins = []
    for ch in (lambda s: c0 + s, lambda s: c0 + n - 1 - s):
        ins += [pl.BlockSpec((cs, kw), lambda s, ch=ch: (ch(s), 0)),
                pl.BlockSpec((cs, kw), lambda s, ch=ch: (ch(s), 1)),
                pl.BlockSpec((cs, GLA_VW), lambda s, ch=ch: (ch(s), 0)),
                pl.BlockSpec((cs, rank2), lambda s, ch=ch: (ch(s), 0)),
                pl.BlockSpec((cs, GLA_DKP), lambda s, ch=ch: (ch(s), 0)),
                pl.BlockSpec((cs, GLA_DKP), lambda s, ch=ch: (ch(s), 0))]
    for dr in (0, 1):
        ins += [pl.BlockSpec((None, rank2, kw), lambda s, dr=dr: (dr, 0, 0)),
                pl.BlockSpec((None, 1, kw), lambda s, dr=dr: (dr, 0, 0))]
    ins += [pl.BlockSpec((GLA_DKP, GLA_DKP), lambda s: (0, 0)),
            pl.BlockSpec(st_shape, lambda s: (0, 0, 0, 0))]
    seq = (u_qk, u_qk, u_vr, lr, cos, sin)
    return pl.pallas_call(
        functools.partial(_gla_kernel, nchunks=n),
        grid=(n,),
        in_specs=ins,
        out_specs=[pl.BlockSpec((cs, GLA_VW), lambda s: (s, 0)),
                   pl.BlockSpec((cs, GLA_VW), lambda s: (n - 1 - s, 0)),
                   pl.BlockSpec(st_shape, lambda s: (0, 0, 0, 0))],
        out_shape=[jax.ShapeDtypeStruct((l, GLA_VW), F32), jax.ShapeDtypeStruct((l, GLA_VW), F32),
                   jax.ShapeDtypeStruct(st_shape, F32)],
        scratch_shapes=[pltpu.VMEM(st_shape, F32)],
        compiler_params=_params(("arbitrary",), 40 << 20),
        name="gla_scan",
    )(*seq, *seq, w2p, bgp, w2p, bgp, perm, s0)


def _gla_out_kernel(of_ref, ob_ref, r_ref, g_ref, o_ref):
    o = of_ref[...] + ob_ref[...]
    g = g_ref[...]
    r = r_ref[...].astype(F32)
    for h in range(GLA_HEADS):
        sl = slice(h * GLA_DV, (h + 1) * GLA_DV)
        oh = o[:, sl]
        on = oh * lax.rsqrt(jnp.mean(oh * oh, axis=-1, keepdims=True) + EPS) * g[:, sl]
        o_ref[:, sl] = (on * _silu(r[:, sl])).astype(o_ref.dtype)


def gla_output(o_f, o_b, u_vr, row0, g_gla, tm=256):
    l = o_f.shape[0]
    tm = min(tm, l)
    b0 = row0 // tm
    return pl.pallas_call(
        _gla_out_kernel,
        grid=(l // tm,),
        in_specs=[pl.BlockSpec((tm, GLA_VW), lambda i: (i, 0)),
                  pl.BlockSpec((tm, GLA_VW), lambda i: (i, 0)),
                  pl.BlockSpec((tm, GLA_VW), lambda i: (b0 + i, 1)),
                  pl.BlockSpec((1, GLA_VW), lambda i: (0, 0))],
        out_specs=pl.BlockSpec((tm, GLA_VW), lambda i: (i, 0)),
        out_shape=jax.ShapeDtypeStruct((l, GLA_VW), BF16),
        compiler_params=_params(("arbitrary",), 24 << 20),
        name="gla_output",
    )(o_f, o_b, u_vr, g_gla.reshape(1, GLA_VW))


def rope_tables(n_tokens):
    seg = GLA_DK // 2
    half = seg // 2
    inv = ROPE_THETA ** (-jnp.arange(half, dtype=F32) / half)
    pos = jnp.arange(n_tokens)
    ang_r = (pos // GRID_W).astype(F32)[:, None] * inv
    ang_c = (pos % GRID_W).astype(F32)[:, None] * inv
    pad1 = jnp.ones((n_tokens, GLA_DKP - GLA_DK), F32)
    pad0 = jnp.zeros((n_tokens, GLA_DKP - GLA_DK), F32)
    cos = jnp.concatenate([jnp.cos(ang_r)] * 2 + [jnp.cos(ang_c)] * 2 + [pad1], axis=1)
    sin = jnp.concatenate([jnp.sin(ang_r)] * 2 + [jnp.sin(ang_c)] * 2 + [pad0], axis=1)
    return cos, sin


def rope_perm():
    seg = GLA_DK // 2
    half = seg // 2
    p = np.zeros((GLA_DKP, GLA_DKP), np.float32)
    for s0 in (0, seg):
        for j in range(half):
            p[s0 + half + j, s0 + j] = -1.0
            p[s0 + j, s0 + half + j] = 1.0
    return jnp.asarray(p, BF16)


def _pad_heads(w):
    lead = w.shape[:-1]
    w = w.reshape(lead + (GLA_HEADS, GLA_DK))
    w = jnp.pad(w, [(0, 0)] * len(lead) + [(0, 0), (0, GLA_DKP - GLA_DK)])
    return w.reshape(lead + (GLA_HEADS * GLA_DKP,))


MOE_TR = 256


def moe_plan(route):
    t = route.shape[0]
    tr = MOE_TR
    n_a = 2 * t
    e = route[:, :2].astype(jnp.int32).reshape(-1)
    iota = jnp.arange(n_a, dtype=jnp.int32)
    e_sorted, order = lax.sort((e, iota), num_keys=1)
    counts = jnp.sum((e[:, None] == jnp.arange(MOE_EXPERTS, dtype=jnp.int32)[None]).astype(jnp.int32), axis=0)
    padded = ((counts + tr - 1) // tr) * tr
    ends = jnp.cumsum(padded)
    starts = ends - padded
    shift = starts - (jnp.cumsum(counts) - counts)
    _, pos = lax.sort((order, iota + shift[e_sorted]), num_keys=1)
    npad = n_a + MOE_EXPERTS * tr
    ntiles = npad // tr
    tile_e = jnp.minimum(jnp.searchsorted(ends, jnp.arange(ntiles, dtype=jnp.int32) * tr, side="right"),
                         MOE_EXPERTS - 1).astype(jnp.int32)
    row = jnp.arange(npad, dtype=jnp.int32)
    per_row = lambda v: jnp.repeat(v[tile_e], tr)
    valid = (row - per_row(starts)) < per_row(counts)
    src = jnp.where(valid, order[jnp.clip(row - per_row(shift), 0, n_a - 1)] // 2, 0)
    n_used = (ends[-1] // tr).reshape(1).astype(jnp.int32)
    return src.astype(jnp.int32), tile_e, n_used, pos.astype(jnp.int32)


def _row_gather_start(idx_ref, first, n, src_hbm, dst, sem):
    def body(j, carry):
        pltpu.make_async_copy(src_hbm.at[pl.ds(idx_ref[first + j], 1)], dst.at[pl.ds(j, 1)], sem).start()
        return carry
    lax.fori_loop(0, n, body, 0, unroll=8)


def _row_gather_wait(n, src_hbm, dst, sem):
    pltpu.make_async_copy(src_hbm.at[pl.ds(0, n)], dst, sem).wait()


def _moe_ffn_kernel(src_ref, te_ref, nu_ref, x_hbm, wgu_ref, wdn_ref, o_ref, buf, sem):
    i = pl.program_id(0)
    n_used = nu_ref[0]
    tr = MOE_TR
    slot = lax.rem(i, 2)

    @pl.when(i == 0)
    def _():
        _row_gather_start(src_ref, 0, tr, x_hbm, buf.at[0], sem.at[0])

    @pl.when(i < n_used)
    def _():
        @pl.when(i + 1 < n_used)
        def _():
            _row_gather_start(src_ref, (i + 1) * tr, tr, x_hbm, buf.at[1 - slot], sem.at[1 - slot])

        _row_gather_wait(tr, x_hbm, buf.at[slot], sem.at[slot])
        x_lo, x_hi = _unpack_bf16_pairs(buf[slot])
        half = x_lo.shape[1]
        gu = (_dot(x_lo.astype(BF16), wgu_ref[:half, :].astype(BF16))
              + _dot(x_hi.astype(BF16), wgu_ref[half:, :].astype(BF16)))
        hid = _silu(gu[:, :MOE_HIDDEN]) * gu[:, MOE_HIDDEN:]
        o_ref[...] = _pack_bf16_pairs(_dot(hid.astype(BF16), wdn_ref[...].astype(BF16)))

    @pl.when(i >= n_used)
    def _():
        o_ref[...] = jnp.zeros(o_ref.shape, o_ref.dtype)


def moe_ffn(xn, w_gu, w_dn, l, src, tile_e, n_used):
    t = xn.shape[0]
    d = 2 * xn.shape[1]
    tr = MOE_TR
    npad = src.shape[0]
    f = MOE_HIDDEN
    grid_spec = pltpu.PrefetchScalarGridSpec(
        num_scalar_prefetch=3,
        grid=(npad // tr,),
        in_specs=[pl.BlockSpec(memory_space=pl.ANY),
                  pl.BlockSpec((None, None, d, 2 * f), lambda i, s, te, nu: (l, te[i], 0, 0)),
                  pl.BlockSpec((None, None, f, d), lambda i, s, te, nu: (l, te[i], 0, 0))],
        out_specs=pl.BlockSpec((tr, d // 2), lambda i, s, te, nu: (i, 0)),
        scratch_shapes=[pltpu.VMEM((2, tr, d // 2), jnp.uint32), pltpu.SemaphoreType.DMA((2,))],
    )
    return pl.pallas_call(
        _moe_ffn_kernel,
        grid_spec=grid_spec,
        out_shape=jax.ShapeDtypeStruct((npad, d // 2), jnp.uint32),
        compiler_params=_params(("arbitrary",), 4 * tr * d * 4 + 2 * 3 * d * f * 4 + 6 * tr * d * 4),
        name="moe_ffn",
    )(src, tile_e, n_used, xn, w_gu, w_dn)


def _moe_combine_kernel(pos_ref, y_hbm, x_ref, route_ref, gate_ref, o_ref, buf, sem):
    i = pl.program_id(0)
    n = pl.num_programs(0)
    tm = x_ref.shape[0]
    slot = lax.rem(i, 2)

    def start(tile, s):
        _row_gather_start(pos_ref, 2 * tm * tile, 2 * tm, y_hbm, buf.at[s], sem.at[s])

    @pl.when(i == 0)
    def _():
        start(0, 0)

    @pl.when(i + 1 < n)
    def _():
        start(i + 1, 1 - slot)

    _row_gather_wait(2 * tm, y_hbm, buf.at[slot], sem.at[slot])
    route = route_ref[...]
    lane = lax.broadcasted_iota(jnp.int32, route.shape, 1)
    w1 = _lane_pick(route, lane, 2)
    w2 = _lane_pick(route, lane, 3)
    cur = buf.at[slot]
    half = cur.shape[1]
    cw = min(4 * LANE, half)
    for c0 in range(0, half, cw):
        first = _unpack_bf16_pairs(cur[:tm, c0:c0 + cw])
        second = _unpack_bf16_pairs(cur[tm:, c0:c0 + cw])
        for part in (0, 1):
            cols = slice(part * half + c0, part * half + c0 + cw)
            o_ref[:, cols] = x_ref[:, cols] + gate_ref[:, cols] * (w1 * first[part] + w2 * second[part])


def moe_combine(y, x, route, gate, pos, row0, tm=128):
    t, d = x.shape
    pos = pos[2 * row0:2 * (row0 + t)].reshape(t // tm, tm, 2).transpose(0, 2, 1).reshape(-1)
    blk0 = row0 // tm
    grid_spec = pltpu.PrefetchScalarGridSpec(
        num_scalar_prefetch=1,
        grid=(t // tm,),
        in_specs=[pl.BlockSpec(memory_space=pl.ANY),
                  pl.BlockSpec((tm, d), lambda i, p: (i, 0)),
                  pl.BlockSpec((tm, LANE), lambda i, p: (i + blk0, 0)),
                  pl.BlockSpec((1, d), lambda i, p: (0, 0))],
        out_specs=pl.BlockSpec((tm, d), lambda i, p: (i, 0)),
        scratch_shapes=[pltpu.VMEM((2, 2 * tm, d // 2), jnp.uint32), pltpu.SemaphoreType.DMA((2,))],
    )
    return pl.pallas_call(
        _moe_combine_kernel,
        grid_spec=grid_spec,
        out_shape=jax.ShapeDtypeStruct((t, d), F32),
        compiler_params=_params(("arbitrary",), 4 * tm * d * 4 + 8 * tm * d * 4),
        name="moe_combine",
    )(pos, y, x, route, gate.reshape(1, d))


def _layer(l, x, xc, mod, last, consts, w):
    (g_mix, w_in, rpb, w_fn, w_g2, b_g, g_gla, w_out, g_ffn, w_rg, b_rg, w_re, b_re, w_gu, w_dn) = w
    cos, sin, perm, na_tab = consts
    d = D_MODEL
    s_len, c_len = x.shape[0], xc.shape[0]
    m_x = mod[l, 0].reshape(N_MOD, d)
    m_c = mod[l, 1].reshape(N_MOD, d)

    xn, lr = norm_proj([(x, m_x[0], m_x[1]), (xc, m_c[0], m_c[1])], g_mix[l], w_in, l, OFF_LR, 2 * GLA_GATE_RANK)
    tm = (s_len + c_len) // 8
    u_na = matmul(xn, w_in, (l,), OFF_NA, 3 * NA_WIDTH, 512, tm, BF16, w_transposed=True, name="in_na")
    u_fn = matmul(xn, w_in, (l,), OFF_FN, FN_WIDTH, 512, tm, F32, w_transposed=True, name="in_fn")
    u_qk = gla_qk_proj(xn, w_in, l, tm)
    u_vr = matmul(xn, w_in, (l,), OFF_GVR, 2 * GLA_VW, 512, tm, BF16, w_transposed=True, name="in_gvr")

    r_ = GLA_GATE_RANK
    w2p = jnp.stack([jnp.pad(_pad_heads(w_g2[l, dr]), ((dr * r_, (1 - dr) * r_), (0, 0))) for dr in (0, 1)])
    bgp = _pad_heads(b_g[l]).reshape(2, 1, -1)
    zero = jnp.zeros((2, GLA_HEADS, GLA_DV, GLA_DKP), F32)
    *outs_c, s_c = gla_scan(u_qk, u_vr, lr, cos, sin, w2p, bgp, perm, zero, s_len, c_len)
    *outs_x, _ = gla_scan(u_qk, u_vr, lr, cos, sin, w2p, bgp, perm, s_c, 0, s_len)

    mix = (neighbourhood_attention(u_na, s_len, na_tab, l),
           fourier_latent(u_fn, s_len, w_fn[l]),
           gla_output(outs_x[0], outs_x[1], u_vr, 0, g_gla[l]))
    x = matmul(mix, w_out, (l,), 0, d, 512, 1024, F32, res=x, gate=m_x[2], name="out_proj")

    w_route = jnp.pad(jnp.concatenate([w_re[l], w_rg[l]], axis=1),
                      ((0, 0), (0, LANE - MOE_EXPERTS - MOE_GROUPS)))
    b_route = jnp.pad(jnp.concatenate([b_re[l], b_rg[l]]), (0, LANE - MOE_EXPERTS - MOE_GROUPS)).reshape(1, LANE)
    streams = [(x, m_x)]
    if not last:
        mix_c = (full_attention(u_na, s_len), fourier_context(u_fn, s_len, w_fn[l]),
                 gla_output(outs_c[0], outs_c[1], u_vr, s_len, g_gla[l]))
        xc = matmul(mix_c, w_out, (l,), 0, d, 512, 1024, F32, res=xc, gate=m_c[2], name="out_proj_ctx")
        streams.append((xc, m_c))

    yn, route = norm_route([(y, m[3], m[4]) for y, m in streams], g_ffn[l], w_route, b_route)
    src, tile_e, n_used, pos = moe_plan(route)
    y_exp = moe_ffn(yn, w_gu, w_dn, l, src, tile_e, n_used)
    outs, row0 = [], 0
    for y, m in streams:
        outs.append(moe_combine(y_exp, y, route, m[5], pos, row0))
        row0 += y.shape[0]
    return outs[0], (outs[1] if not last else xc)


def kernel(x, c, ctx, c_ctx, w_ada, b_ada, g_mix, w_in, rpb, w_fn, w_g2, b_g, g_gla, w_out,
           g_ffn, w_rg, b_rg, w_re, b_re, w_gu, w_dn, g_final):
    assert x.shape[0] == 1 and c.shape[0] == 1
    d = D_MODEL
    s_len = x.shape[1]
    c_len = ctx.shape[1]
    cc = jnp.concatenate([c, c_ctx[None], jnp.zeros((6, d), F32)], axis=0)
    mod = ada_mod(cc, w_ada, b_ada)
    cos, sin = rope_tables(s_len)
    cos = jnp.concatenate([cos, jnp.ones((c_len, GLA_DKP), F32)], axis=0)
    sin = jnp.concatenate([sin, jnp.zeros((c_len, GLA_DKP), F32)], axis=0)
    consts = (cos, sin, rope_perm(), na_tables(rpb, s_len // GRID_W))
    w_in = jnp.swapaxes(w_in, 1, 2)
    w = (g_mix, w_in, rpb, w_fn, w_g2, b_g, g_gla, w_out, g_ffn, w_rg, b_rg, w_re, b_re, w_gu, w_dn)
    xs, xc = x[0], ctx[0]
    for l in range(DEPTH):
        xs, xc = _layer(l, xs, xc, mod, l == DEPTH - 1, consts, w)
    return final_norm(xs, g_final)[None]
```

```python
import functools
import math

import numpy as np
import jax
import jax.numpy as jnp
from jax import lax
from jax.experimental import pallas as pl
from jax.experimental.pallas import tpu as pltpu

F32 = jnp.float32
BF16 = jnp.bfloat16

D_MODEL = 4096
DEPTH = 4
GRID_W = 64
EPS = 1e-6
N_MOD = 6

NA_HEADS = 12
NA_HEAD_DIM = 128
NA_WIN_H = 8
NA_WIN_W = 16
NA_WIDTH = NA_HEADS * NA_HEAD_DIM

FN_GROUPS = 8
FN_GROUP_DIM = 128
FN_WIDTH = FN_GROUPS * FN_GROUP_DIM

GLA_HEADS = 4
GLA_DK = 192
GLA_DKP = 256
GLA_DV = 384
GLA_KW = GLA_HEADS * GLA_DK
GLA_VW = GLA_HEADS * GLA_DV
GLA_GATE_RANK = 16
GLA_GATE_TEMP = 16.0
GLA_CHUNK = 64
GLA_SUB = 8
ROPE_THETA = 10000.0

MOE_GROUPS = 4
MOE_EXPERTS_PER_GROUP = 8
MOE_EXPERTS = MOE_GROUPS * MOE_EXPERTS_PER_GROUP
MOE_HIDDEN = 192

OFF_NA = 0
OFF_FN = 3 * NA_WIDTH
OFF_GQK = OFF_FN + FN_WIDTH
OFF_GVR = OFF_GQK + 2 * GLA_KW
OFF_LR = OFF_GVR + 2 * GLA_VW

LANE = 128
NEG = -1e30
V7X_VMEM_BUDGET = 56 * 1024 * 1024


def _params(sem, vmem_bytes):
    return pltpu.CompilerParams(dimension_semantics=sem,
                                vmem_limit_bytes=int(min(max(vmem_bytes, 16 << 20), V7X_VMEM_BUDGET)))


def _dot(a, b):
    return jnp.dot(a, b, preferred_element_type=F32)


def _dot_nt(a, b):
    return lax.dot_general(a, b, (((1,), (1,)), ((), ())), preferred_element_type=F32)


def _dot_tn(a, b):
    return lax.dot_general(a, b, (((0,), (0,)), ((), ())), preferred_element_type=F32)


def _silu(x):
    return x * (1.0 / (1.0 + jnp.exp(-x)))


def _ada_kernel(c_ref, w_ref, b_ref, o_ref):
    s = _silu(c_ref[...]).astype(BF16)
    o_ref[0] = _dot(s, w_ref[0].astype(BF16)) + b_ref[0]


def ada_mod(cc, w_ada, b_ada, tn=512):
    n, d, nd = w_ada.shape
    return pl.pallas_call(
        _ada_kernel,
        grid=(n, nd // tn),
        in_specs=[pl.BlockSpec((8, d), lambda l, j: (0, 0)),
                  pl.BlockSpec((1, d, tn), lambda l, j: (l, 0, j)),
                  pl.BlockSpec((1, 1, tn), lambda l, j: (l, 0, j))],
        out_specs=pl.BlockSpec((1, 8, tn), lambda l, j: (l, 0, j)),
        out_shape=jax.ShapeDtypeStruct((n, 8, nd), F32),
        compiler_params=_params(("arbitrary", "arbitrary"), 3 * d * tn * 4 + (4 << 20)),
        name="ada_mod",
    )(cc, w_ada, b_ada.reshape(n, 1, nd))


def _normed(x_ref, g_ref, sh_ref, sc_ref):
    x = x_ref[...]
    y = x * lax.rsqrt(jnp.mean(x * x, axis=-1, keepdims=True) + EPS)
    return (y * g_ref[...]) * (1.0 + sc_ref[...]) + sh_ref[...]


def _norm_proj_kernel(*refs, blocks, layer, col0):
    ns = len(blocks)
    x_refs, g_ref = refs[:ns], refs[ns]
    mods = refs[ns + 1:ns + 1 + 2 * ns]
    w_hbm, xn_ref, p_ref, w_buf, sem = refs[ns + 1 + 2 * ns:]
    i = pl.program_id(0)

    @pl.when(i == 0)
    def _():
        cp = pltpu.make_async_copy(w_hbm.at[layer, pl.ds(col0, w_buf.shape[0]), :], w_buf, sem)
        cp.start()
        cp.wait()

    first = 0
    for s_, nb in enumerate(blocks):
        @pl.when((i >= first) & (i < first + nb))
        def _(s_=s_):
            hb = _normed(x_refs[s_], g_ref, mods[2 * s_], mods[2 * s_ + 1]).astype(BF16)
            xn_ref[...] = hb
            p_ref[...] = _dot_nt(hb, w_buf[...].astype(BF16))
        first += nb


def _lane_pick(v, lane, idx):
    return jnp.sum(jnp.where(lane == idx, v, 0.0), axis=-1, keepdims=True)


def _pack_bf16_pairs(x):
    xf = x.astype(BF16).astype(F32)
    half = xf.shape[1] // 2
    lo = lax.bitcast_convert_type(xf[:, :half], jnp.uint32)
    hi = lax.bitcast_convert_type(xf[:, half:], jnp.uint32)
    return hi | (lo >> 16)


def _unpack_bf16_pairs(p):
    lo = lax.bitcast_convert_type(p << 16, F32)
    hi = lax.bitcast_convert_type(p & jnp.uint32(0xFFFF0000), F32)
    return lo, hi


def _norm_route_kernel(x_ref, g_ref, sh_ref, sc_ref, w_ref, b_ref, xn_ref, route_ref):
    h = _normed(x_ref, g_ref, sh_ref, sc_ref)
    hb = h.astype(BF16)
    xn_ref[...] = _pack_bf16_pairs(h)
    logits = _dot(hb, w_ref[...].astype(BF16)) + b_ref[...]
    lane = lax.broadcasted_iota(jnp.int32, logits.shape, 1).astype(F32)
    far = float(4 * LANE)
    is_g = (lane >= MOE_EXPERTS) & (lane < MOE_EXPERTS + MOE_GROUPS)
    gl = jnp.where(is_g, logits, NEG)
    gmax = jnp.max(gl, axis=-1, keepdims=True)
    g_top = jnp.min(jnp.where(gl == gmax, lane, far), axis=-1, keepdims=True) - MOE_EXPERTS
    g_w = 1.0 / jnp.sum(jnp.where(is_g, jnp.exp(gl - gmax), 0.0), axis=-1, keepdims=True)
    in_grp = (lane >= g_top * MOE_EXPERTS_PER_GROUP) & (lane < (g_top + 1) * MOE_EXPERTS_PER_GROUP)
    el = jnp.where(in_grp, logits, NEG)
    v1 = jnp.max(el, axis=-1, keepdims=True)
    i1 = jnp.min(jnp.where(el == v1, lane, far), axis=-1, keepdims=True)
    el2 = jnp.where(lane == i1, NEG, el)
    v2 = jnp.max(el2, axis=-1, keepdims=True)
    i2 = jnp.min(jnp.where(el2 == v2, lane, far), axis=-1, keepdims=True)
    e2 = jnp.exp(v2 - v1)
    w1 = g_w / (1.0 + e2)
    w2 = g_w * e2 / (1.0 + e2)
    route_ref[...] = (jnp.where(lane == 0.0, i1, 0.0) + jnp.where(lane == 1.0, i2, 0.0)
                      + jnp.where(lane == 2.0, w1, 0.0) + jnp.where(lane == 3.0, w2, 0.0))


def _norm_route_streams_kernel(*refs, blocks):
    ns = len(blocks)
    x_refs, g_ref = refs[:ns], refs[ns]
    mods = refs[ns + 1:ns + 1 + 2 * ns]
    w_ref, b_ref, xn_ref, route_ref = refs[ns + 1 + 2 * ns:]
    i = pl.program_id(0)
    first = 0
    for s_, nb in enumerate(blocks):
        @pl.when((i >= first) & (i < first + nb))
        def _(s_=s_):
            _norm_route_kernel(x_refs[s_], g_ref, mods[2 * s_], mods[2 * s_ + 1], w_ref, b_ref, xn_ref, route_ref)
        first += nb


def _stream_specs(streams, tm, d):
    blocks = tuple(x.shape[0] // tm for x, _, _ in streams)
    ins, args, first = [], [], 0
    for (x, _, _), nb in zip(streams, blocks):
        ins.append(pl.BlockSpec((tm, d), lambda i, first=first, nb=nb: (jnp.clip(i - first, 0, nb - 1), 0)))
        args.append(x)
        first += nb
    return blocks, ins, args


def norm_proj(streams, g, w, layer, col0, ncols, tm=256):
    d = streams[0][0].shape[1]
    blocks, ins, args = _stream_specs(streams, tm, d)
    vec = pl.BlockSpec((1, d), lambda i: (0, 0))
    ins.append(vec)
    args.append(g.reshape(1, d))
    for _, sh, sc in streams:
        ins += [vec, vec]
        args += [sh.reshape(1, d), sc.reshape(1, d)]
    ins.append(pl.BlockSpec(memory_space=pl.ANY))
    args.append(w)
    rows = sum(blocks) * tm
    return pl.pallas_call(
        functools.partial(_norm_proj_kernel, blocks=blocks, layer=layer, col0=col0),
        grid=(sum(blocks),),
        in_specs=ins,
        out_specs=[pl.BlockSpec((tm, d), lambda i: (i, 0)), pl.BlockSpec((tm, ncols), lambda i: (i, 0))],
        out_shape=[jax.ShapeDtypeStruct((rows, d), BF16), jax.ShapeDtypeStruct((rows, ncols), F32)],
        scratch_shapes=[pltpu.VMEM((ncols, d), F32), pltpu.SemaphoreType.DMA(())],
        compiler_params=_params(("arbitrary",), (4 + 4 * len(streams)) * tm * d * 4 + 2 * d * LANE * 4),
        name="norm_proj",
    )(*args)


def norm_route(streams, g, w_small, b_small, tm=256):
    d = streams[0][0].shape[1]
    blocks, ins, args = _stream_specs(streams, tm, d)
    row = lambda i: (0, 0)
    vec = pl.BlockSpec((1, d), row)
    ins.append(vec)
    args.append(g.reshape(1, d))
    for _, sh, sc in streams:
        ins += [vec, vec]
        args += [sh.reshape(1, d), sc.reshape(1, d)]
    ins += [pl.BlockSpec((d, LANE), row), pl.BlockSpec((1, LANE), row)]
    args += [w_small, b_small]
    rows = sum(blocks) * tm
    return pl.pallas_call(
        functools.partial(_norm_route_streams_kernel, blocks=blocks),
        grid=(sum(blocks),),
        in_specs=ins,
        out_specs=[pl.BlockSpec((tm, d // 2), lambda i: (i, 0)), pl.BlockSpec((tm, LANE), lambda i: (i, 0))],
        out_shape=[jax.ShapeDtypeStruct((rows, d // 2), jnp.uint32), jax.ShapeDtypeStruct((rows, LANE), F32)],
        compiler_params=_params(("arbitrary",), (4 + 4 * len(streams)) * tm * d * 4 + 4 * d * LANE * 4),
        name="norm_route",
    )(*args)


def _final_norm_kernel(x_ref, g_ref, o_ref):
    x = x_ref[...]
    o_ref[...] = x * lax.rsqrt(jnp.mean(x * x, axis=-1, keepdims=True) + EPS) * g_ref[...]


def final_norm(x, g, tm=256):
    t, d = x.shape
    return pl.pallas_call(
        _final_norm_kernel,
        grid=(t // tm,),
        in_specs=[pl.BlockSpec((tm, d), lambda i: (i, 0)), pl.BlockSpec((1, d), lambda i: (0, 0))],
        out_specs=pl.BlockSpec((tm, d), lambda i: (i, 0)),
        out_shape=jax.ShapeDtypeStruct((t, d), F32),
        compiler_params=_params(("arbitrary",), 6 * tm * d * 4),
        name="final_norm",
    )(x, g.reshape(1, d))


def _mm_kernel(a_ref, w_ref, o_ref):
    o_ref[...] = _dot(a_ref[...], w_ref[...].astype(BF16)).astype(o_ref.dtype)


def _mm_wt_kernel(a_ref, wt_ref, o_ref):
    o_ref[...] = _dot_nt(a_ref[...], wt_ref[...].astype(BF16)).astype(o_ref.dtype)


def _mm_res_kernel(a_ref, w_ref, r_ref, g_ref, o_ref):
    o_ref[...] = r_ref[...] + g_ref[...] * _dot(a_ref[...], w_ref[...].astype(BF16))


def _mm_parts_res_kernel(*refs, widths):
    n = len(widths)
    w_ref, r_ref, g_ref, o_ref = refs[n:]
    acc, k0 = None, 0
    for a_ref, wd in zip(refs[:n], widths):
        part = _dot(a_ref[...].astype(BF16), w_ref[k0:k0 + wd, :].astype(BF16))
        acc = part if acc is None else acc + part
        k0 += wd
    o_ref[...] = r_ref[...] + g_ref[...] * acc


def _mm_head_pad_kernel(a_ref, w0_ref, w1_ref, w2_ref, o_ref):
    a = a_ref[...]
    acc = jnp.concatenate([_dot_nt(a, w_ref[...].astype(BF16)) for w_ref in (w0_ref, w1_ref, w2_ref)], axis=1)
    pad = jnp.zeros((a.shape[0], GLA_DKP - GLA_DK), o_ref.dtype)
    for h in range(GLA_HEADS):
        o_ref[:, h * GLA_DKP:h * GLA_DKP + GLA_DK] = acc[:, h * GLA_DK:(h + 1) * GLA_DK].astype(o_ref.dtype)
        o_ref[:, h * GLA_DKP + GLA_DK:(h + 1) * GLA_DKP] = pad


def gla_qk_proj(a, w_in_t, l, tm=1024):
    t, k = a.shape
    tm = min(tm, t)
    tn = GLA_KW // 3
    j0 = OFF_GQK // tn
    w_spec = lambda p: pl.BlockSpec((None, tn, k), lambda i, j: (l, j0 + 3 * j + p, 0))
    kwp = GLA_HEADS * GLA_DKP
    return pl.pallas_call(
        _mm_head_pad_kernel,
        grid=(t // tm, 2),
        in_specs=[pl.BlockSpec((tm, k), lambda i, j: (i, 0)), w_spec(0), w_spec(1), w_spec(2)],
        out_specs=pl.BlockSpec((tm, kwp), lambda i, j: (i, j)),
        out_shape=jax.ShapeDtypeStruct((t, 2 * kwp), BF16),
        compiler_params=_params(("arbitrary", "arbitrary"), 2 * tm * k * 2 + 7 * k * tn * 4 + 8 * tm * kwp * 4),
        name="in_gqk",
    )(a, w_in_t, w_in_t, w_in_t)


def matmul(a, w, w_lead, col0, ncols, tn, tm, out_dtype, res=None, gate=None, w_transposed=False, name="mm"):
    parts = a if isinstance(a, (tuple, list)) else (a,)
    t = parts[0].shape[0]
    k = sum(p.shape[1] for p in parts)
    tm = min(tm, t)
    nlead = len(w_lead)
    j0 = col0 // tn
    if w_transposed:
        w_spec = pl.BlockSpec((None,) * nlead + (tn, k), lambda i, j: tuple(w_lead) + (j + j0, 0))
    else:
        w_spec = pl.BlockSpec((None,) * nlead + (k, tn), lambda i, j: tuple(w_lead) + (0, j + j0))
    ins = [pl.BlockSpec((tm, p.shape[1]), lambda i, j: (i, 0)) for p in parts] + [w_spec]
    args = list(parts) + [w]
    wbytes = jnp.dtype(w.dtype).itemsize
    vmem = 2 * tm * k * 2 + 3 * k * tn * wbytes + 6 * tm * tn * 4
    if res is None:
        body = _mm_wt_kernel if w_transposed else _mm_kernel
    else:
        assert not w_transposed
        body = _mm_res_kernel if len(parts) == 1 else functools.partial(
            _mm_parts_res_kernel, widths=tuple(p.shape[1] for p in parts))
        ins += [pl.BlockSpec((tm, tn), lambda i, j: (i, j)), pl.BlockSpec((1, tn), lambda i, j: (0, j))]
        args += [res, gate.reshape(1, ncols)]
    return pl.pallas_call(
        body,
        grid=(t // tm, ncols // tn),
        in_specs=ins,
        out_specs=pl.BlockSpec((tm, tn), lambda i, j: (i, j)),
        out_shape=jax.ShapeDtypeStruct((t, ncols), out_dtype),
        compiler_params=_params(("arbitrary", "arbitrary"), vmem),
        name=name,
    )(*args)


NA_QROWS = 4
NA_SLAB = 12
NA_HEADS_PER_STEP = 4


def na_tables(rpb, rows):
    nblk = rows // NA_QROWS
    n_, h_ = rpb.shape[:2]
    qc = np.arange(GRID_W)[:, None]
    kc = np.arange(GRID_W)[None, :]
    c0 = np.clip(qc - NA_WIN_W // 2, 0, GRID_W - NA_WIN_W)
    ok_c = (kc >= c0) & (kc < c0 + NA_WIN_W)
    a = np.arange(NA_QROWS)[:, None]
    b = np.arange(NA_SLAB)[None, :]
    dr_l, ok_l = [], []
    for i in (0, 1, nblk - 1):
        base = int(np.clip(i * NA_QROWS - NA_WIN_H // 2, 0, rows - NA_SLAB))
        r = i * NA_QROWS + a
        r0 = np.clip(r - NA_WIN_H // 2, 0, rows - NA_WIN_H)
        krow = base + b
        ok_l.append((krow >= r0) & (krow < r0 + NA_WIN_H))
        dr_l.append(np.clip(krow - r + NA_WIN_H - 1, 0, 2 * NA_WIN_H - 2))
    dr = np.stack(dr_l)
    ok = np.stack(ok_l)[:, :, None, :, None] & ok_c[None, None, :, None, :]
    ok = jnp.asarray(ok.reshape(3, NA_QROWS * GRID_W, NA_SLAB * GRID_W))
    lo = GRID_W - NA_WIN_W
    padded = jnp.pad(rpb.astype(F32), ((0, 0), (0, 0), (0, 0), (lo, lo)))
    by_col = jnp.stack([padded[..., GRID_W - 1 - q:2 * GRID_W - 1 - q] for q in range(GRID_W)], axis=3)
    variants = []
    for v in range(3):
        rows_ = [jnp.concatenate([by_col[:, :, int(dr[v, a_, b_])] for b_ in range(NA_SLAB)], axis=-1)
                 for a_ in range(NA_QROWS)]
        variants.append(jnp.concatenate(rows_, axis=2))
    bias = jnp.stack(variants, axis=1)
    return jnp.where(ok[None, :, None], bias, NEG)


def _na_kernel(q_ref, k_ref, v_ref, kc_ref, vc_ref, t_ref, o_ref, *, rows):
    i = pl.program_id(1)
    base = jnp.clip(i * NA_QROWS - NA_WIN_H // 2, 0, rows - NA_SLAB) * GRID_W
    base = pl.multiple_of(base, NA_QROWS * GRID_W)
    n_keys = NA_SLAB * GRID_W
    scale = NA_HEAD_DIM ** -0.5
    for hh in range(NA_HEADS_PER_STEP):
        cols = slice(hh * NA_HEAD_DIM, (hh + 1) * NA_HEAD_DIM)
        q = q_ref[:, cols]
        k = k_ref[pl.ds(base, n_keys), cols]
        v = v_ref[pl.ds(base, n_keys), cols]
        s = _dot_nt(q, k) * scale + t_ref[hh]
        sc = _dot_nt(q, kc_ref[:, cols]) * scale
        m = jnp.maximum(jnp.max(s, axis=-1, keepdims=True), jnp.max(sc, axis=-1, keepdims=True))
        p = jnp.exp(s - m)
        pc = jnp.exp(sc - m)
        den = jnp.sum(p, axis=-1, keepdims=True) + jnp.sum(pc, axis=-1, keepdims=True)
        o = _dot(p.astype(BF16), v) + _dot(pc.astype(BF16), vc_ref[:, cols])
        o_ref[:, cols] = (o / den).astype(o_ref.dtype)


def neighbourhood_attention(u, s, tables, l):
    c = u.shape[0] - s
    rows = s // GRID_W
    nblk = rows // NA_QROWS
    tq = NA_QROWS * GRID_W
    hps = NA_HEADS_PER_STEP
    ng = NA_HEADS // hps
    hw = hps * NA_HEAD_DIM

    def variant(i):
        return jnp.where(i == 0, 0, jnp.where(i == nblk - 1, 2, 1))

    return pl.pallas_call(
        functools.partial(_na_kernel, rows=rows),
        grid=(ng, nblk),
        in_specs=[pl.BlockSpec((tq, hw), lambda h, i: (i, h)),
                  pl.BlockSpec((s, hw), lambda h, i: (0, ng + h)),
                  pl.BlockSpec((s, hw), lambda h, i: (0, 2 * ng + h)),
                  pl.BlockSpec((c, hw), lambda h, i: (s // c, ng + h)),
                  pl.BlockSpec((c, hw), lambda h, i: (s // c, 2 * ng + h)),
                  pl.BlockSpec((None, None, hps, tq, NA_SLAB * GRID_W), lambda h, i: (l, variant(i), h, 0, 0))],
        out_specs=pl.BlockSpec((tq, hw), lambda h, i: (i, h)),
        out_shape=jax.ShapeDtypeStruct((s, NA_WIDTH), BF16),
        compiler_params=_params(("arbitrary", "arbitrary"),
                                4 * s * hw * 2 + 2 * hps * tq * NA_SLAB * GRID_W * 4 + (12 << 20)),
        name="na_latent",
    )(u, u, u, u, u, tables)


def _full_attn_kernel(q_ref, k_ref, v_ref, o_ref):
    s = _dot_nt(q_ref[...], k_ref[...]) * NA_HEAD_DIM ** -0.5
    p = jnp.exp(s - jnp.max(s, axis=-1, keepdims=True))
    den = jnp.sum(p, axis=-1, keepdims=True)
    o_ref[...] = (_dot(p.astype(BF16), v_ref[...]) / den).astype(o_ref.dtype)


def full_attention(u, s):
    c = u.shape[0] - s
    b0 = s // c
    h_ = NA_HEADS
    hd = NA_HEAD_DIM
    return pl.pallas_call(
        _full_attn_kernel,
        grid=(h_,),
        in_specs=[pl.BlockSpec((c, hd), lambda h: (b0, h)),
                  pl.BlockSpec((c, hd), lambda h: (b0, h_ + h)),
                  pl.BlockSpec((c, hd), lambda h: (b0, 2 * h_ + h))],
        out_specs=pl.BlockSpec((c, hd), lambda h: (0, h)),
        out_shape=jax.ShapeDtypeStruct((c, NA_WIDTH), BF16),
        compiler_params=_params(("arbitrary",), 16 << 20),
        name="na_context",
    )(u, u, u)


def _dft_cs(n):
    idx = np.arange(n)
    ang = 2.0 * np.pi * ((idx[:, None] * idx[None, :]) % n) / n
    return np.cos(ang), np.sin(ang)


FN_ROWS = 8


def _fn1_kernel(w_ref, x_ref, tr_ref, ti_ref, z_ref, *, r):
    for s in range(FN_ROWS):
        y = _dot(w_ref[...], x_ref[:, s, :].astype(BF16))
        yr, yi = y[:r], y[r:]
        tr, ti = tr_ref[s], ti_ref[s]
        z_ref[pl.ds(0, r), s, :] = yr * tr - yi * ti
        z_ref[pl.ds(r, r), s, :] = yr * ti + yi * tr


def _fn2_kernel(zr_ref, zi_ref, kc_ref, ks_ref, c_ref, s_ref, w_ref, o_ref, *, scale):
    zr, zi = zr_ref[...].astype(BF16), zi_ref[...].astype(BF16)
    kc, ks = kc_ref[...], ks_ref[...]
    xr = (_dot(kc, zr) + _dot(ks, zi)).astype(BF16)
    xi = (_dot(kc, zi) - _dot(ks, zr)).astype(BF16)
    cw = GRID_W
    for g in range(FN_GROUPS):
        lo = g * FN_GROUP_DIM
        f = _dot(xr[:, lo:lo + FN_GROUP_DIM], c_ref[...]) + _dot(xi[:, lo:lo + FN_GROUP_DIM], s_ref[...])
        y = _dot((f * scale).astype(BF16), w_ref[g].astype(BF16))
        for kk in range(FN_ROWS):
            o_ref[:, kk, lo:lo + FN_GROUP_DIM] = y[kk * cw:(kk + 1) * cw]


def fourier_latent(u, l, w_fn):
    cw = GRID_W
    r = l // cw
    c_r, s_r = _dft_cs(r)
    w1 = jnp.asarray(np.concatenate([c_r, -s_r], axis=0), BF16)
    ang = 2.0 * np.pi * (np.arange(cw)[:, None] * np.arange(r)[None, :]) / l
    tr = jnp.asarray(np.cos(ang)[:, :, None], F32)
    ti = jnp.asarray(-np.sin(ang)[:, :, None], F32)
    z = pl.pallas_call(
        functools.partial(_fn1_kernel, r=r),
        grid=(cw // FN_ROWS,),
        in_specs=[pl.BlockSpec((2 * r, r), lambda j: (0, 0)),
                  pl.BlockSpec((r, FN_ROWS, FN_WIDTH), lambda j: (0, j, 0)),
                  pl.BlockSpec((FN_ROWS, r, 1), lambda j: (j, 0, 0)),
                  pl.BlockSpec((FN_ROWS, r, 1), lambda j: (j, 0, 0))],
        out_specs=pl.BlockSpec((2 * r, FN_ROWS, FN_WIDTH), lambda j: (0, j, 0)),
        out_shape=jax.ShapeDtypeStruct((2 * r, cw, FN_WIDTH), F32),
        compiler_params=_params(("arbitrary",), 40 << 20),
        name="fourier_stage1",
    )(w1, u.reshape(u.shape[0] // cw, cw, FN_WIDTH), tr, ti)
    z2 = z.reshape(2 * r * cw, FN_WIDTH)
    c_w, s_w = _dft_cs(cw)
    eye = np.eye(FN_ROWS)
    kc = jnp.asarray(np.kron(eye, c_w), BF16)
    ks = jnp.asarray(np.kron(eye, s_w), BF16)
    c_c, s_c = _dft_cs(FN_GROUP_DIM)
    nb = r // FN_ROWS
    tb = FN_ROWS * cw
    out = pl.pallas_call(
        functools.partial(_fn2_kernel, scale=float((l * FN_GROUP_DIM) ** -0.5)),
        grid=(nb,),
        in_specs=[pl.BlockSpec((tb, FN_WIDTH), lambda b: (b, 0)),
                  pl.BlockSpec((tb, FN_WIDTH), lambda b: (nb + b, 0)),
                  pl.BlockSpec((tb, tb), lambda b: (0, 0)),
                  pl.BlockSpec((tb, tb), lambda b: (0, 0)),
                  pl.BlockSpec((FN_GROUP_DIM, FN_GROUP_DIM), lambda b: (0, 0)),
                  pl.BlockSpec((FN_GROUP_DIM, FN_GROUP_DIM), lambda b: (0, 0)),
                  pl.BlockSpec((FN_GROUPS, FN_GROUP_DIM, FN_GROUP_DIM), lambda b: (0, 0, 0))],
        out_specs=pl.BlockSpec((cw, FN_ROWS, FN_WIDTH), lambda b: (0, b, 0)),
        out_shape=jax.ShapeDtypeStruct((cw, r, FN_WIDTH), F32),
        compiler_params=_params(("arbitrary",), 32 << 20),
        name="fourier_stage2",
    )(z2, z2, kc, ks, jnp.asarray(c_c, BF16), jnp.asarray(s_c, BF16), w_fn)
    return out.reshape(l, FN_WIDTH)


def _fn_ctx_kernel(u_ref, cl_ref, sl_ref, c_ref, s_ref, w_ref, o_ref, *, scale):
    u = u_ref[...].astype(BF16)
    gr = _dot(cl_ref[...], u).astype(BF16)
    gi = (-_dot(sl_ref[...], u)).astype(BF16)
    for g in range(FN_GROUPS):
        lo = g * FN_GROUP_DIM
        f = _dot(gr[:, lo:lo + FN_GROUP_DIM], c_ref[...]) + _dot(gi[:, lo:lo + FN_GROUP_DIM], s_ref[...])
        o_ref[:, lo:lo + FN_GROUP_DIM] = _dot((f * scale).astype(BF16), w_ref[g].astype(BF16)).astype(o_ref.dtype)


def fourier_context(u, s, w_fn):
    c = u.shape[0] - s
    c_l, s_l = _dft_cs(c)
    c_c, s_c = _dft_cs(FN_GROUP_DIM)
    gd = FN_GROUP_DIM
    whole = lambda shape: pl.BlockSpec(shape, lambda i: (0,) * len(shape))
    return pl.pallas_call(
        functools.partial(_fn_ctx_kernel, scale=float((c * FN_GROUP_DIM) ** -0.5)),
        grid=(1,),
        in_specs=[pl.BlockSpec((c, FN_WIDTH), lambda i: (s // c, 0)), whole((c, c)), whole((c, c)),
                  whole((gd, gd)), whole((gd, gd)), whole((FN_GROUPS, gd, gd))],
        out_specs=whole((c, FN_WIDTH)),
        out_shape=jax.ShapeDtypeStruct((c, FN_WIDTH), BF16),
        compiler_params=_params(("arbitrary",), 16 << 20),
        name="fourier_context",
    )(u, jnp.asarray(c_l, BF16), jnp.asarray(s_l, BF16), jnp.asarray(c_c, BF16), jnp.asarray(s_c, BF16), w_fn)


def _split_hi_lo(x):
    hi = x.astype(BF16)
    return hi, (x - hi.astype(F32)).astype(BF16)


def _gla_chunk(q_ref, k_ref, v_ref, lr_ref, cos_ref, sin_ref, w2_ref, bg_ref, perm_ref, st_ref, o_ref, rev):
    cs = GLA_CHUNK
    sub = GLA_SUB
    nsub = cs // sub
    z = _dot(lr_ref[...].astype(BF16), w2_ref[...].astype(BF16)) + bg_ref[...]
    la = -(jnp.maximum(-z, 0.0) + jnp.log(1.0 + jnp.exp(-jnp.abs(z)))) * (1.0 / GLA_GATE_TEMP)
    ri = lax.broadcasted_iota(jnp.int32, (cs, cs), 0)
    ci = lax.broadcasted_iota(jnp.int32, (cs, cs), 1)
    tri = jnp.where((ci >= ri) if rev else (ci <= ri), 1.0, 0.0).astype(BF16)
    la_hi, la_lo = _split_hi_lo(la)
    bcum = _dot(tri, la_hi) + _dot(tri, la_lo)
    edge = 0 if rev else cs - 1
    row_id = lax.broadcasted_iota(jnp.int32, (cs, GLA_DKP), 0)
    sub_r = lax.broadcasted_iota(jnp.int32, (sub, GLA_DKP), 0)
    lane_c = lax.broadcasted_iota(jnp.int32, (sub, cs), 1)
    cos, sin = cos_ref[...], sin_ref[...]
    perm = perm_ref[...]
    qscale = GLA_DK ** -0.5
    for h in range(GLA_HEADS):
        ks_ = slice(h * GLA_DKP, (h + 1) * GLA_DKP)
        vs_ = slice(h * GLA_DV, (h + 1) * GLA_DV)
        qb, kb = q_ref[:, ks_], k_ref[:, ks_]
        q = (qb.astype(F32) * cos + _dot(qb, perm) * sin) * qscale
        k = kb.astype(F32) * cos + _dot(kb, perm) * sin
        v = v_ref[:, vs_]
        b = bcum[:, ks_]
        b_edge = b[edge:edge + 1, :]
        st = st_ref[h]
        o = _dot_nt((q * jnp.exp(b)).astype(BF16), st.astype(BF16))
        slabs = []
        for blk in range(nsub):
            lo = blk * sub
            q_i = q[lo:lo + sub]
            b_i = b[lo:lo + sub]
            k_i = k[lo:lo + sub]
            acc = jnp.zeros((sub, cs), F32)
            if rev and blk < nsub - 1:
                ref_row = b[lo + sub:lo + sub + 1, :]
                outside = row_id >= lo + sub
            elif (not rev) and blk > 0:
                ref_row = b[lo - 1:lo, :]
                outside = row_id < lo
            else:
                ref_row = None
            if ref_row is not None:
                qe = q_i * jnp.exp(b_i - ref_row)
                ke = jnp.where(outside, k * jnp.exp(jnp.where(outside, ref_row - b, 0.0)), 0.0)
                acc = acc + _dot_nt(qe.astype(BF16), ke.astype(BF16))
            for j in range(sub):
                keep = (sub_r <= j) if rev else (sub_r >= j)
                d = jnp.exp(jnp.where(keep, b_i - b_i[j:j + 1, :], NEG))
                col = jnp.sum(q_i * d * k_i[j:j + 1, :], axis=-1, keepdims=True)
                acc = acc + jnp.where(lane_c == lo + j, col, 0.0)
            slabs.append(acc)
        attn = jnp.concatenate(slabs, axis=0)
        o = o + _dot(attn.astype(BF16), v)
        o_ref[:, vs_] = o
        kend = (k * jnp.exp(b_edge - b)).astype(BF16)
        st_ref[h] = st * jnp.exp(b_edge) + _dot_tn(v, kend)


def _gla_kernel(*refs, nchunks):
    fwd, bwd = refs[0:6], refs[6:12]
    w2f_ref, bgf_ref, w2b_ref, bgb_ref, perm_ref, s0_ref = refs[12:18]
    of_ref, ob_ref, sfin_ref, st_ref = refs[18:22]
    step = pl.program_id(0)

    @pl.when(step == 0)
    def _():
        st_ref[...] = s0_ref[...]

    _gla_chunk(*fwd, w2f_ref, bgf_ref, perm_ref, st_ref.at[0], of_ref, False)
    _gla_chunk(*bwd, w2b_ref, bgb_ref, perm_ref, st_ref.at[1], ob_ref, True)

    @pl.when(step == nchunks - 1)
    def _():
        sfin_ref[...] = st_ref[...]


def gla_scan(u_qk, u_vr, lr, cos, sin, w2p, bgp, perm, s0, row0, l):
    n = l // GLA_CHUNK
    c0 = row0 // GLA_CHUNK
    cs = GLA_CHUNK
    kw = GLA_HEADS * GLA_DKP
    rank2 = lr.shape[1]
    st_shape = (2, GLA_HEADS, GLA_DV, GLA_DKP)
    ins = []
    for ch in (lambda s: c0 + s, lambda s: c0 + n - 1 - s):
        ins += [pl.BlockSpec((cs, kw), lambda s, ch=ch: (ch(s), 0)),
                pl.BlockSpec((cs, kw), lambda s, ch=ch: (ch(s), 1)),
                pl.BlockSpec((cs, GLA_VW), lambda s, ch=ch: (ch(s), 0)),
                pl.BlockSpec((cs, rank2), lambda s, ch=ch: (ch(s), 0)),
                pl.BlockSpec((cs, GLA_DKP), lambda s, ch=ch: (ch(s), 0)),
                pl.BlockSpec((cs, GLA_DKP), lambda s, ch=ch: (ch(s), 0))]
    for dr in (0, 1):
        ins += [pl.BlockSpec((None, rank2, kw), lambda s, dr=dr: (dr, 0, 0)),
                pl.BlockSpec((None, 1, kw), lambda s, dr=dr: (dr, 0, 0))]
    ins += [pl.BlockSpec((GLA_DKP, GLA_DKP), lambda s: (0, 0)),
            pl.BlockSpec(st_shape, lambda s: (0, 0, 0, 0))]
    seq = (u_qk, u_qk, u_vr, lr, cos, sin)
    return pl.pallas_call(
        functools.partial(_gla_kernel, nchunks=n),
        grid=(n,),
        in_specs=ins,
        out_specs=[pl.BlockSpec((cs, GLA_VW), lambda s: (s, 0)),
                   pl.BlockSpec((cs, GLA_VW), lambda s: (n - 1 - s, 0)),
                   pl.BlockSpec(st_shape, lambda s: (0, 0, 0, 0))],
        out_shape=[jax.ShapeDtypeStruct((l, GLA_VW), F32), jax.ShapeDtypeStruct((l, GLA_VW), F32),
                   jax.ShapeDtypeStruct(st_shape, F32)],
        scratch_shapes=[pltpu.VMEM(st_shape, F32)],
        compiler_params=_params(("arbitrary",), 40 << 20),
        name="gla_scan",
    )(*seq, *seq, w2p, bgp, w2p, bgp, perm, s0)


def _gla_out_kernel(of_ref, ob_ref, r_ref, g_ref, o_ref):
    o = of_ref[...] + ob_ref[...]
    g = g_ref[...]
    r = r_ref[...].astype(F32)
    for h in range(GLA_HEADS):
        sl = slice(h * GLA_DV, (h + 1) * GLA_DV)
        oh = o[:, sl]
        on = oh * lax.rsqrt(jnp.mean(oh * oh, axis=-1, keepdims=True) + EPS) * g[:, sl]
        o_ref[:, sl] = (on * _silu(r[:, sl])).astype(o_ref.dtype)


def gla_output(o_f, o_b, u_vr, row0, g_gla, tm=256):
    l = o_f.shape[0]
    tm = min(tm, l)
    b0 = row0 // tm
    return pl.pallas_call(
        _gla_out_kernel,
        grid=(l // tm,),
        in_specs=[pl.BlockSpec((tm, GLA_VW), lambda i: (i, 0)),
                  pl.BlockSpec((tm, GLA_VW), lambda i: (i, 0)),
                  pl.BlockSpec((tm, GLA_VW), lambda i: (b0 + i, 1)),
                  pl.BlockSpec((1, GLA_VW), lambda i: (0, 0))],
        out_specs=pl.BlockSpec((tm, GLA_VW), lambda i: (i, 0)),
        out_shape=jax.ShapeDtypeStruct((l, GLA_VW), BF16),
        compiler_params=_params(("arbitrary",), 24 << 20),
        name="gla_output",
    )(o_f, o_b, u_vr, g_gla.reshape(1, GLA_VW))


def rope_tables(n_tokens):
    seg = GLA_DK // 2
    half = seg // 2
    inv = ROPE_THETA ** (-jnp.arange(half, dtype=F32) / half)
    pos = jnp.arange(n_tokens)
    ang_r = (pos // GRID_W).astype(F32)[:, None] * inv
    ang_c = (pos % GRID_W).astype(F32)[:, None] * inv
    pad1 = jnp.ones((n_tokens, GLA_DKP - GLA_DK), F32)
    pad0 = jnp.zeros((n_tokens, GLA_DKP - GLA_DK), F32)
    cos = jnp.concatenate([jnp.cos(ang_r)] * 2 + [jnp.cos(ang_c)] * 2 + [pad1], axis=1)
    sin = jnp.concatenate([jnp.sin(ang_r)] * 2 + [jnp.sin(ang_c)] * 2 + [pad0], axis=1)
    return cos, sin


def rope_perm():
    seg = GLA_DK // 2
    half = seg // 2
    p = np.zeros((GLA_DKP, GLA_DKP), np.float32)
    for s0 in (0, seg):
        for j in range(half):
            p[s0 + half + j, s0 + j] = -1.0
            p[s0 + j, s0 + half + j] = 1.0
    return jnp.asarray(p, BF16)


def _pad_heads(w):
    lead = w.shape[:-1]
    w = w.reshape(lead + (GLA_HEADS, GLA_DK))
    w = jnp.pad(w, [(0, 0)] * len(lead) + [(0, 0), (0, GLA_DKP - GLA_DK)])
    return w.reshape(lead + (GLA_HEADS * GLA_DKP,))


MOE_TR = 256
MOE_GATHER_DEPTH = 3


def moe_plan(route):
    t = route.shape[0]
    tr = MOE_TR
    n_a = 2 * t
    e = route[:, :2].astype(jnp.int32).reshape(-1)
    iota = jnp.arange(n_a, dtype=jnp.int32)
    e_sorted, order = lax.sort((e, iota), num_keys=1)
    counts = jnp.sum((e[:, None] == jnp.arange(MOE_EXPERTS, dtype=jnp.int32)[None]).astype(jnp.int32), axis=0)
    padded = ((counts + tr - 1) // tr) * tr
    ends = jnp.cumsum(padded)
    starts = ends - padded
    shift = starts - (jnp.cumsum(counts) - counts)
    _, pos = lax.sort((order, iota + shift[e_sorted]), num_keys=1)
    npad = n_a + MOE_EXPERTS * tr
    ntiles = npad // tr
    tile_e = jnp.minimum(jnp.searchsorted(ends, jnp.arange(ntiles, dtype=jnp.int32) * tr, side="right"),
                         MOE_EXPERTS - 1).astype(jnp.int32)
    row = jnp.arange(npad, dtype=jnp.int32)
    per_row = lambda v: jnp.repeat(v[tile_e], tr)
    valid = (row - per_row(starts)) < per_row(counts)
    src = jnp.where(valid, order[jnp.clip(row - per_row(shift), 0, n_a - 1)] // 2, 0)
    n_used = (ends[-1] // tr).reshape(1).astype(jnp.int32)
    return src.astype(jnp.int32), tile_e, n_used, pos.astype(jnp.int32)


ROW_GROUP = 8


def _row_gather_start(idx_ref, first, src_hbm, dst, sem):
    def body(g, carry):
        for u in range(ROW_GROUP):
            pltpu.make_async_copy(src_hbm.at[pl.ds(idx_ref[first + g * ROW_GROUP + u], 1)],
                                  dst.at[g, pl.ds(u, 1)], sem).start()
        return carry
    lax.fori_loop(0, dst.shape[0], body, 0)


def _row_gather_wait(dst, sem):
    pltpu.make_async_copy(dst, dst, sem).wait()


def _rows(x):
    return x.reshape(x.shape[0] * ROW_GROUP, x.shape[2])


def _moe_ffn_kernel(src_ref, te_ref, nu_ref, x_hbm, wgu_ref, wdn_ref, o_ref, buf, sem):
    i = pl.program_id(0)
    n_used = nu_ref[0]
    tr = MOE_TR
    depth = buf.shape[0]
    slot = lax.rem(i, depth)

    @pl.when(i == 0)
    def _():
        for ahead in range(depth - 1):
            @pl.when(ahead < n_used)
            def _(ahead=ahead):
                _row_gather_start(src_ref, ahead * tr, x_hbm, buf.at[ahead], sem.at[ahead])

    @pl.when(i < n_used)
    def _():
        nxt = i + depth - 1

        @pl.when(nxt < n_used)
        def _():
            s_ = lax.rem(nxt, depth)
            _row_gather_start(src_ref, nxt * tr, x_hbm, buf.at[s_], sem.at[s_])

        _row_gather_wait(buf.at[slot], sem.at[slot])
        x_lo, x_hi = _unpack_bf16_pairs(_rows(buf[slot]))
        half = x_lo.shape[1]
        gu = (_dot(x_lo.astype(BF16), wgu_ref[:half, :].astype(BF16))
              + _dot(x_hi.astype(BF16), wgu_ref[half:, :].astype(BF16)))
        hid = _silu(gu[:, :MOE_HIDDEN]) * gu[:, MOE_HIDDEN:]
        o_ref[...] = _pack_bf16_pairs(_dot(hid.astype(BF16), wdn_ref[...].astype(BF16)))

    @pl.when(i >= n_used)
    def _():
        o_ref[...] = jnp.zeros(o_ref.shape, o_ref.dtype)


def moe_ffn(xn, w_gu, w_dn, l, src, tile_e, n_used):
    t = xn.shape[0]
    d = 2 * xn.shape[1]
    tr = MOE_TR
    npad = src.shape[0]
    f = MOE_HIDDEN
    grid_spec = pltpu.PrefetchScalarGridSpec(
        num_scalar_prefetch=3,
        grid=(npad // tr,),
        in_specs=[pl.BlockSpec(memory_space=pl.ANY),
                  pl.BlockSpec((None, None, d, 2 * f), lambda i, s, te, nu: (l, te[i], 0, 0)),
                  pl.BlockSpec((None, None, f, d), lambda i, s, te, nu: (l, te[i], 0, 0))],
        out_specs=pl.BlockSpec((tr, d // 2), lambda i, s, te, nu: (i, 0)),
        scratch_shapes=[pltpu.VMEM((MOE_GATHER_DEPTH, tr // ROW_GROUP, ROW_GROUP, d // 2), jnp.uint32),
                        pltpu.SemaphoreType.DMA((MOE_GATHER_DEPTH,))],
    )
    return pl.pallas_call(
        _moe_ffn_kernel,
        grid_spec=grid_spec,
        out_shape=jax.ShapeDtypeStruct((npad, d // 2), jnp.uint32),
        compiler_params=_params(("arbitrary",), 4 * tr * d * 4 + 2 * 3 * d * f * 4 + 6 * tr * d * 4),
        name="moe_ffn",
    )(src, tile_e, n_used, xn, w_gu, w_dn)


def _moe_combine_kernel(pos_ref, y_hbm, x_ref, route_ref, gate_ref, o_ref, buf, sem):
    i = pl.program_id(0)
    n = pl.num_programs(0)
    tm = x_ref.shape[0]
    slot = lax.rem(i, 2)

    def start(tile, s):
        _row_gather_start(pos_ref, 2 * tm * tile, y_hbm, buf.at[s], sem.at[s])

    @pl.when(i == 0)
    def _():
        start(0, 0)

    @pl.when(i + 1 < n)
    def _():
        start(i + 1, 1 - slot)

    _row_gather_wait(buf.at[slot], sem.at[slot])
    route = route_ref[...]
    lane = lax.broadcasted_iota(jnp.int32, route.shape, 1)
    w1 = _lane_pick(route, lane, 2)
    w2 = _lane_pick(route, lane, 3)
    cur = buf.at[slot]
    half = cur.shape[2]
    gm = tm // ROW_GROUP
    cw = min(4 * LANE, half)
    for c0 in range(0, half, cw):
        first = _unpack_bf16_pairs(_rows(cur[:gm, :, c0:c0 + cw]))
        second = _unpack_bf16_pairs(_rows(cur[gm:, :, c0:c0 + cw]))
        for part in (0, 1):
            cols = slice(part * half + c0, part * half + c0 + cw)
            o_ref[:, cols] = x_ref[:, cols] + gate_ref[:, cols] * (w1 * first[part] + w2 * second[part])


def moe_combine(y, x, route, gate, pos, row0, tm=128):
    t, d = x.shape
    pos = pos[2 * row0:2 * (row0 + t)].reshape(t // tm, tm, 2).transpose(0, 2, 1).reshape(-1)
    blk0 = row0 // tm
    grid_spec = pltpu.PrefetchScalarGridSpec(
        num_scalar_prefetch=1,
        grid=(t // tm,),
        in_specs=[pl.BlockSpec(memory_space=pl.ANY),
                  pl.BlockSpec((tm, d), lambda i, p: (i, 0)),
                  pl.BlockSpec((tm, LANE), lambda i, p: (i + blk0, 0)),
                  pl.BlockSpec((1, d), lambda i, p: (0, 0))],
        out_specs=pl.BlockSpec((tm, d), lambda i, p: (i, 0)),
        scratch_shapes=[pltpu.VMEM((2, 2 * tm // ROW_GROUP, ROW_GROUP, d // 2), jnp.uint32),
                        pltpu.SemaphoreType.DMA((2,))],
    )
    return pl.pallas_call(
        _moe_combine_kernel,
        grid_spec=grid_spec,
        out_shape=jax.ShapeDtypeStruct((t, d), F32),
        compiler_params=_params(("arbitrary",), 4 * tm * d * 4 + 8 * tm * d * 4),
        name="moe_combine",
    )(pos, y, x, route, gate.reshape(1, d))


def _layer(l, x, xc, mod, last, consts, w):
    (g_mix, w_in, rpb, w_fn, w_g2, b_g, g_gla, w_out, g_ffn, w_rg, b_rg, w_re, b_re, w_gu, w_dn) = w
    cos, sin, perm, na_tab = consts
    d = D_MODEL
    s_len, c_len = x.shape[0], xc.shape[0]
    m_x = mod[l, 0].reshape(N_MOD, d)
    m_c = mod[l, 1].reshape(N_MOD, d)

    xn, lr = norm_proj([(x, m_x[0], m_x[1]), (xc, m_c[0], m_c[1])], g_mix[l], w_in, l, OFF_LR, 2 * GLA_GATE_RANK)
    tm = (s_len + c_len) // 8
    u_na = matmul(xn, w_in, (l,), OFF_NA, 3 * NA_WIDTH, 512, tm, BF16, w_transposed=True, name="in_na")
    u_fn = matmul(xn, w_in, (l,), OFF_FN, FN_WIDTH, 512, tm, F32, w_transposed=True, name="in_fn")
    u_qk = gla_qk_proj(xn, w_in, l, tm)
    u_vr = matmul(xn, w_in, (l,), OFF_GVR, 2 * GLA_VW, 512, tm, BF16, w_transposed=True, name="in_gvr")

    r_ = GLA_GATE_RANK
    w2p = jnp.stack([jnp.pad(_pad_heads(w_g2[l, dr]), ((dr * r_, (1 - dr) * r_), (0, 0))) for dr in (0, 1)])
    bgp = _pad_heads(b_g[l]).reshape(2, 1, -1)
    zero = jnp.zeros((2, GLA_HEADS, GLA_DV, GLA_DKP), F32)
    *outs_c, s_c = gla_scan(u_qk, u_vr, lr, cos, sin, w2p, bgp, perm, zero, s_len, c_len)
    *outs_x, _ = gla_scan(u_qk, u_vr, lr, cos, sin, w2p, bgp, perm, s_c, 0, s_len)

    mix = (neighbourhood_attention(u_na, s_len, na_tab, l),
           fourier_latent(u_fn, s_len, w_fn[l]),
           gla_output(outs_x[0], outs_x[1], u_vr, 0, g_gla[l]))
    x = matmul(mix, w_out, (l,), 0, d, 512, 1024, F32, res=x, gate=m_x[2], name="out_proj")

    w_route = jnp.pad(jnp.concatenate([w_re[l], w_rg[l]], axis=1),
                      ((0, 0), (0, LANE - MOE_EXPERTS - MOE_GROUPS)))
    b_route = jnp.pad(jnp.concatenate([b_re[l], b_rg[l]]), (0, LANE - MOE_EXPERTS - MOE_GROUPS)).reshape(1, LANE)
    streams = [(x, m_x)]
    if not last:
        mix_c = (full_attention(u_na, s_len), fourier_context(u_fn, s_len, w_fn[l]),
                 gla_output(outs_c[0], outs_c[1], u_vr, s_len, g_gla[l]))
        xc = matmul(mix_c, w_out, (l,), 0, d, 512, 1024, F32, res=xc, gate=m_c[2], name="out_proj_ctx")
        streams.append((xc, m_c))

    yn, route = norm_route([(y, m[3], m[4]) for y, m in streams], g_ffn[l], w_route, b_route)
    src, tile_e, n_used, pos = moe_plan(route)
    y_exp = moe_ffn(yn, w_gu, w_dn, l, src, tile_e, n_used)
    outs, row0 = [], 0
    for y, m in streams:
        outs.append(moe_combine(y_exp, y, route, m[5], pos, row0))
        row0 += y.shape[0]
    return outs[0], (outs[1] if not last else xc)


def kernel(x, c, ctx, c_ctx, w_ada, b_ada, g_mix, w_in, rpb, w_fn, w_g2, b_g, g_gla, w_out,
           g_ffn, w_rg, b_rg, w_re, b_re, w_gu, w_dn, g_final):
    assert x.shape[0] == 1 and c.shape[0] == 1
    d = D_MODEL
    s_len = x.shape[1]
    c_len = ctx.shape[1]
    cc = jnp.concatenate([c, c_ctx[None], jnp.zeros((6, d), F32)], axis=0)
    mod = ada_mod(cc, w_ada, b_ada)
    cos, sin = rope_tables(s_len)
    cos = jnp.concatenate([cos, jnp.ones((c_len, GLA_DKP), F32)], axis=0)
    sin = jnp.concatenate([sin, jnp.zeros((c_len, GLA_DKP), F32)], axis=0)
    consts = (cos, sin, rope_perm(), na_tables(rpb, s_len // GRID_W))
    w_in = jnp.swapaxes(w_in, 1, 2)
    w = (g_mix, w_in, rpb, w_fn, w_g2, b_g, g_gla, w_out, g_ffn, w_rg, b_rg, w_re, b_re, w_gu, w_dn)
    xs, xc = x[0], ctx[0]
    for l in range(DEPTH):
        xs, xc = _layer(l, xs, xc, mod, l == DEPTH - 1, consts, w)
    return final_norm(xs, g_final)[None]
```

```python
import functools
import math

import numpy as np
import jax
import jax.numpy as jnp
from jax import lax
from jax.experimental import pallas as pl
from jax.experimental.pallas import tpu as pltpu

F32 = jnp.float32
BF16 = jnp.bfloat16

D_MODEL = 4096
DEPTH = 4
GRID_W = 64
EPS = 1e-6
N_MOD = 6

NA_HEADS = 12
NA_HEAD_DIM = 128
NA_WIN_H = 8
NA_WIN_W = 16
NA_WIDTH = NA_HEADS * NA_HEAD_DIM

FN_GROUPS = 8
FN_GROUP_DIM = 128
FN_WIDTH = FN_GROUPS * FN_GROUP_DIM

GLA_HEADS = 4
GLA_DK = 192
GLA_DKP = 256
GLA_DV = 384
GLA_KW = GLA_HEADS * GLA_DK
GLA_VW = GLA_HEADS * GLA_DV
GLA_GATE_RANK = 16
GLA_GATE_TEMP = 16.0
GLA_CHUNK = 64
GLA_SUB = 8
ROPE_THETA = 10000.0

MOE_GROUPS = 4
MOE_EXPERTS_PER_GROUP = 8
MOE_EXPERTS = MOE_GROUPS * MOE_EXPERTS_PER_GROUP
MOE_HIDDEN = 192

OFF_NA = 0
OFF_FN = 3 * NA_WIDTH
OFF_GQK = OFF_FN + FN_WIDTH
OFF_GVR = OFF_GQK + 2 * GLA_KW
OFF_LR = OFF_GVR + 2 * GLA_VW

LANE = 128
NEG = -1e30
V7X_VMEM_BUDGET = 56 * 1024 * 1024


def _params(sem, vmem_bytes):
    return pltpu.CompilerParams(dimension_semantics=sem,
                                vmem_limit_bytes=int(min(max(vmem_bytes, 16 << 20), V7X_VMEM_BUDGET)))


def _dot(a, b):
    return jnp.dot(a, b, preferred_element_type=F32)


def _dot_nt(a, b):
    return lax.dot_general(a, b, (((1,), (1,)), ((), ())), preferred_element_type=F32)


def _dot_tn(a, b):
    return lax.dot_general(a, b, (((0,), (0,)), ((), ())), preferred_element_type=F32)


def _silu(x):
    return x * (1.0 / (1.0 + jnp.exp(-x)))


def _ada_kernel(c_ref, w_ref, b_ref, o_ref):
    s = _silu(c_ref[...]).astype(BF16)
    o_ref[0] = _dot(s, w_ref[0].astype(BF16)) + b_ref[0]


def ada_mod(cc, w_ada, b_ada, tn=512):
    n, d, nd = w_ada.shape
    return pl.pallas_call(
        _ada_kernel,
        grid=(n, nd // tn),
        in_specs=[pl.BlockSpec((8, d), lambda l, j: (0, 0)),
                  pl.BlockSpec((1, d, tn), lambda l, j: (l, 0, j)),
                  pl.BlockSpec((1, 1, tn), lambda l, j: (l, 0, j))],
        out_specs=pl.BlockSpec((1, 8, tn), lambda l, j: (l, 0, j)),
        out_shape=jax.ShapeDtypeStruct((n, 8, nd), F32),
        compiler_params=_params(("arbitrary", "arbitrary"), 3 * d * tn * 4 + (4 << 20)),
        name="ada_mod",
    )(cc, w_ada, b_ada.reshape(n, 1, nd))


def _normed(x_ref, g_ref, sh_ref, sc_ref):
    x = x_ref[...]
    y = x * lax.rsqrt(jnp.mean(x * x, axis=-1, keepdims=True) + EPS)
    return (y * g_ref[...]) * (1.0 + sc_ref[...]) + sh_ref[...]


def _norm_proj_kernel(*refs, blocks, layer, col0):
    ns = len(blocks)
    x_refs, g_ref = refs[:ns], refs[ns]
    mods = refs[ns + 1:ns + 1 + 2 * ns]
    w_hbm, xn_ref, p_ref, w_buf, sem = refs[ns + 1 + 2 * ns:]
    i = pl.program_id(0)

    @pl.when(i == 0)
    def _():
        cp = pltpu.make_async_copy(w_hbm.at[layer, pl.ds(col0, w_buf.shape[0]), :], w_buf, sem)
        cp.start()
        cp.wait()

    first = 0
    for s_, nb in enumerate(blocks):
        @pl.when((i >= first) & (i < first + nb))
        def _(s_=s_):
            hb = _normed(x_refs[s_], g_ref, mods[2 * s_], mods[2 * s_ + 1]).astype(BF16)
            xn_ref[...] = hb
            p_ref[...] = _dot_nt(hb, w_buf[...].astype(BF16))
        first += nb


def _lane_pick(v, lane, idx):
    return jnp.sum(jnp.where(lane == idx, v, 0.0), axis=-1, keepdims=True)


def _pack_bf16_pairs(x):
    xf = x.astype(BF16).astype(F32)
    half = xf.shape[1] // 2
    lo = lax.bitcast_convert_type(xf[:, :half], jnp.uint32)
    hi = lax.bitcast_convert_type(xf[:, half:], jnp.uint32)
    return hi | (lo >> 16)


def _unpack_bf16_pairs(p):
    lo = lax.bitcast_convert_type(p << 16, F32)
    hi = lax.bitcast_convert_type(p & jnp.uint32(0xFFFF0000), F32)
    return lo, hi


def _norm_route_kernel(x_ref, g_ref, sh_ref, sc_ref, w_ref, b_ref, xn_ref, route_ref):
    h = _normed(x_ref, g_ref, sh_ref, sc_ref)
    hb = h.astype(BF16)
    xn_ref[...] = _pack_bf16_pairs(h)
    logits = _dot(hb, w_ref[...].astype(BF16)) + b_ref[...]
    lane = lax.broadcasted_iota(jnp.int32, logits.shape, 1).astype(F32)
    far = float(4 * LANE)
    is_g = (lane >= MOE_EXPERTS) & (lane < MOE_EXPERTS + MOE_GROUPS)
    gl = jnp.where(is_g, logits, NEG)
    gmax = jnp.max(gl, axis=-1, keepdims=True)
    g_top = jnp.min(jnp.where(gl == gmax, lane, far), axis=-1, keepdims=True) - MOE_EXPERTS
    g_w = 1.0 / jnp.sum(jnp.where(is_g, jnp.exp(gl - gmax), 0.0), axis=-1, keepdims=True)
    in_grp = (lane >= g_top * MOE_EXPERTS_PER_GROUP) & (lane < (g_top + 1) * MOE_EXPERTS_PER_GROUP)
    el = jnp.where(in_grp, logits, NEG)
    v1 = jnp.max(el, axis=-1, keepdims=True)
    i1 = jnp.min(jnp.where(el == v1, lane, far), axis=-1, keepdims=True)
    el2 = jnp.where(lane == i1, NEG, el)
    v2 = jnp.max(el2, axis=-1, keepdims=True)
    i2 = jnp.min(jnp.where(el2 == v2, lane, far), axis=-1, keepdims=True)
    e2 = jnp.exp(v2 - v1)
    w1 = g_w / (1.0 + e2)
    w2 = g_w * e2 / (1.0 + e2)
    route_ref[...] = (jnp.where(lane == 0.0, i1, 0.0) + jnp.where(lane == 1.0, i2, 0.0)
                      + jnp.where(lane == 2.0, w1, 0.0) + jnp.where(lane == 3.0, w2, 0.0))


def _norm_route_streams_kernel(*refs, blocks):
    ns = len(blocks)
    x_refs, g_ref = refs[:ns], refs[ns]
    mods = refs[ns + 1:ns + 1 + 2 * ns]
    w_ref, b_ref, xn_ref, route_ref = refs[ns + 1 + 2 * ns:]
    i = pl.program_id(0)
    first = 0
    for s_, nb in enumerate(blocks):
        @pl.when((i >= first) & (i < first + nb))
        def _(s_=s_):
            _norm_route_kernel(x_refs[s_], g_ref, mods[2 * s_], mods[2 * s_ + 1], w_ref, b_ref, xn_ref, route_ref)
        first += nb


def _stream_specs(streams, tm, d):
    blocks = tuple(x.shape[0] // tm for x, _, _ in streams)
    ins, args, first = [], [], 0
    for (x, _, _), nb in zip(streams, blocks):
        ins.append(pl.BlockSpec((tm, d), lambda i, first=first, nb=nb: (jnp.clip(i - first, 0, nb - 1), 0)))
        args.append(x)
        first += nb
    return blocks, ins, args


def norm_proj(streams, g, w, layer, col0, ncols, tm=256):
    d = streams[0][0].shape[1]
    blocks, ins, args = _stream_specs(streams, tm, d)
    vec = pl.BlockSpec((1, d), lambda i: (0, 0))
    ins.append(vec)
    args.append(g.reshape(1, d))
    for _, sh, sc in streams:
        ins += [vec, vec]
        args += [sh.reshape(1, d), sc.reshape(1, d)]
    ins.append(pl.BlockSpec(memory_space=pl.ANY))
    args.append(w)
    rows = sum(blocks) * tm
    return pl.pallas_call(
        functools.partial(_norm_proj_kernel, blocks=blocks, layer=layer, col0=col0),
        grid=(sum(blocks),),
        in_specs=ins,
        out_specs=[pl.BlockSpec((tm, d), lambda i: (i, 0)), pl.BlockSpec((tm, ncols), lambda i: (i, 0))],
        out_shape=[jax.ShapeDtypeStruct((rows, d), BF16), jax.ShapeDtypeStruct((rows, ncols), F32)],
        scratch_shapes=[pltpu.VMEM((ncols, d), F32), pltpu.SemaphoreType.DMA(())],
        compiler_params=_params(("arbitrary",), (4 + 4 * len(streams)) * tm * d * 4 + 2 * d * LANE * 4),
        name="norm_proj",
    )(*args)


def norm_route(streams, g, w_small, b_small, tm=256):
    d = streams[0][0].shape[1]
    blocks, ins, args = _stream_specs(streams, tm, d)
    row = lambda i: (0, 0)
    vec = pl.BlockSpec((1, d), row)
    ins.append(vec)
    args.append(g.reshape(1, d))
    for _, sh, sc in streams:
        ins += [vec, vec]
        args += [sh.reshape(1, d), sc.reshape(1, d)]
    ins += [pl.BlockSpec((d, LANE), row), pl.BlockSpec((1, LANE), row)]
    args += [w_small, b_small]
    rows = sum(blocks) * tm
    return pl.pallas_call(
        functools.partial(_norm_route_streams_kernel, blocks=blocks),
        grid=(sum(blocks),),
        in_specs=ins,
        out_specs=[pl.BlockSpec((tm, d // 2), lambda i: (i, 0)), pl.BlockSpec((tm, LANE), lambda i: (i, 0))],
        out_shape=[jax.ShapeDtypeStruct((rows, d // 2), jnp.uint32), jax.ShapeDtypeStruct((rows, LANE), F32)],
        compiler_params=_params(("arbitrary",), (4 + 4 * len(streams)) * tm * d * 4 + 4 * d * LANE * 4),
        name="norm_route",
    )(*args)


def _final_norm_kernel(x_ref, g_ref, o_ref):
    x = x_ref[...]
    o_ref[...] = x * lax.rsqrt(jnp.mean(x * x, axis=-1, keepdims=True) + EPS) * g_ref[...]


def final_norm(x, g, tm=256):
    t, d = x.shape
    return pl.pallas_call(
        _final_norm_kernel,
        grid=(t // tm,),
        in_specs=[pl.BlockSpec((tm, d), lambda i: (i, 0)), pl.BlockSpec((1, d), lambda i: (0, 0))],
        out_specs=pl.BlockSpec((tm, d), lambda i: (i, 0)),
        out_shape=jax.ShapeDtypeStruct((t, d), F32),
        compiler_params=_params(("arbitrary",), 6 * tm * d * 4),
        name="final_norm",
    )(x, g.reshape(1, d))


def _mm_kernel(a_ref, w_ref, o_ref):
    o_ref[...] = _dot(a_ref[...], w_ref[...].astype(BF16)).astype(o_ref.dtype)


def _mm_wt_kernel(a_ref, wt_ref, o_ref):
    o_ref[...] = _dot_nt(a_ref[...], wt_ref[...].astype(BF16)).astype(o_ref.dtype)


def _mm_res_kernel(a_ref, w_ref, r_ref, g_ref, o_ref):
    o_ref[...] = r_ref[...] + g_ref[...] * _dot(a_ref[...], w_ref[...].astype(BF16))


def _mm_parts_res_kernel(*refs, widths):
    n = len(widths)
    w_ref, r_ref, g_ref, o_ref = refs[n:]
    acc, k0 = None, 0
    for a_ref, wd in zip(refs[:n], widths):
        part = _dot(a_ref[...].astype(BF16), w_ref[k0:k0 + wd, :].astype(BF16))
        acc = part if acc is None else acc + part
        k0 += wd
    o_ref[...] = r_ref[...] + g_ref[...] * acc


def _mm_head_pad_kernel(a_ref, w0_ref, w1_ref, w2_ref, o_ref):
    a = a_ref[...]
    acc = jnp.concatenate([_dot_nt(a, w_ref[...].astype(BF16)) for w_ref in (w0_ref, w1_ref, w2_ref)], axis=1)
    pad = jnp.zeros((a.shape[0], GLA_DKP - GLA_DK), o_ref.dtype)
    for h in range(GLA_HEADS):
        o_ref[:, h * GLA_DKP:h * GLA_DKP + GLA_DK] = acc[:, h * GLA_DK:(h + 1) * GLA_DK].astype(o_ref.dtype)
        o_ref[:, h * GLA_DKP + GLA_DK:(h + 1) * GLA_DKP] = pad


def gla_qk_proj(a, w_in_t, l, tm=1024):
    t, k = a.shape
    tm = min(tm, t)
    tn = GLA_KW // 3
    j0 = OFF_GQK // tn
    w_spec = lambda p: pl.BlockSpec((None, tn, k), lambda i, j: (l, j0 + 3 * j + p, 0))
    kwp = GLA_HEADS * GLA_DKP
    return pl.pallas_call(
        _mm_head_pad_kernel,
        grid=(t // tm, 2),
        in_specs=[pl.BlockSpec((tm, k), lambda i, j: (i, 0)), w_spec(0), w_spec(1), w_spec(2)],
        out_specs=pl.BlockSpec((tm, kwp), lambda i, j: (i, j)),
        out_shape=jax.ShapeDtypeStruct((t, 2 * kwp), BF16),
        compiler_params=_params(("arbitrary", "arbitrary"), 2 * tm * k * 2 + 7 * k * tn * 4 + 8 * tm * kwp * 4),
        name="in_gqk",
    )(a, w_in_t, w_in_t, w_in_t)


def matmul(a, w, w_lead, col0, ncols, tn, tm, out_dtype, res=None, gate=None, w_transposed=False, name="mm"):
    parts = a if isinstance(a, (tuple, list)) else (a,)
    t = parts[0].shape[0]
    k = sum(p.shape[1] for p in parts)
    tm = min(tm, t)
    nlead = len(w_lead)
    j0 = col0 // tn
    if w_transposed:
        w_spec = pl.BlockSpec((None,) * nlead + (tn, k), lambda i, j: tuple(w_lead) + (j + j0, 0))
    else:
        w_spec = pl.BlockSpec((None,) * nlead + (k, tn), lambda i, j: tuple(w_lead) + (0, j + j0))
    ins = [pl.BlockSpec((tm, p.shape[1]), lambda i, j: (i, 0)) for p in parts] + [w_spec]
    args = list(parts) + [w]
    wbytes = jnp.dtype(w.dtype).itemsize
    vmem = 2 * tm * k * 2 + 3 * k * tn * wbytes + 6 * tm * tn * 4
    if res is None:
        body = _mm_wt_kernel if w_transposed else _mm_kernel
    else:
        assert not w_transposed
        body = _mm_res_kernel if len(parts) == 1 else functools.partial(
            _mm_parts_res_kernel, widths=tuple(p.shape[1] for p in parts))
        ins += [pl.BlockSpec((tm, tn), lambda i, j: (i, j)), pl.BlockSpec((1, tn), lambda i, j: (0, j))]
        args += [res, gate.reshape(1, ncols)]
    return pl.pallas_call(
        body,
        grid=(t // tm, ncols // tn),
        in_specs=ins,
        out_specs=pl.BlockSpec((tm, tn), lambda i, j: (i, j)),
        out_shape=jax.ShapeDtypeStruct((t, ncols), out_dtype),
        compiler_params=_params(("arbitrary", "arbitrary"), vmem),
        name=name,
    )(*args)


NA_QROWS = 4
NA_SLAB = 12
NA_HEADS_PER_STEP = 4


def na_tables(rpb, rows):
    nblk = rows // NA_QROWS
    n_, h_ = rpb.shape[:2]
    qc = np.arange(GRID_W)[:, None]
    kc = np.arange(GRID_W)[None, :]
    c0 = np.clip(qc - NA_WIN_W // 2, 0, GRID_W - NA_WIN_W)
    ok_c = (kc >= c0) & (kc < c0 + NA_WIN_W)
    a = np.arange(NA_QROWS)[:, None]
    b = np.arange(NA_SLAB)[None, :]
    dr_l, ok_l = [], []
    for i in (0, 1, nblk - 1):
        base = int(np.clip(i * NA_QROWS - NA_WIN_H // 2, 0, rows - NA_SLAB))
        r = i * NA_QROWS + a
        r0 = np.clip(r - NA_WIN_H // 2, 0, rows - NA_WIN_H)
        krow = base + b
        ok_l.append((krow >= r0) & (krow < r0 + NA_WIN_H))
        dr_l.append(np.clip(krow - r + NA_WIN_H - 1, 0, 2 * NA_WIN_H - 2))
    dr = np.stack(dr_l)
    ok = np.stack(ok_l)[:, :, None, :, None] & ok_c[None, None, :, None, :]
    ok = jnp.asarray(ok.reshape(3, NA_QROWS * GRID_W, NA_SLAB * GRID_W))
    lo = GRID_W - NA_WIN_W
    padded = jnp.pad(rpb.astype(F32), ((0, 0), (0, 0), (0, 0), (lo, lo)))
    by_col = jnp.stack([padded[..., GRID_W - 1 - q:2 * GRID_W - 1 - q] for q in range(GRID_W)], axis=3)
    variants = []
    for v in range(3):
        rows_ = [jnp.concatenate([by_col[:, :, int(dr[v, a_, b_])] for b_ in range(NA_SLAB)], axis=-1)
                 for a_ in range(NA_QROWS)]
        variants.append(jnp.concatenate(rows_, axis=2))
    bias = jnp.stack(variants, axis=1)
    return jnp.where(ok[None, :, None], bias, NEG)


def _na_kernel(q_ref, k_ref, v_ref, kc_ref, vc_ref, t_ref, o_ref, *, rows):
    i = pl.program_id(1)
    base = jnp.clip(i * NA_QROWS - NA_WIN_H // 2, 0, rows - NA_SLAB) * GRID_W
    base = pl.multiple_of(base, NA_QROWS * GRID_W)
    n_keys = NA_SLAB * GRID_W
    scale = NA_HEAD_DIM ** -0.5
    for hh in range(NA_HEADS_PER_STEP):
        cols = slice(hh * NA_HEAD_DIM, (hh + 1) * NA_HEAD_DIM)
        q = q_ref[:, cols]
        k = k_ref[pl.ds(base, n_keys), cols]
        v = v_ref[pl.ds(base, n_keys), cols]
        s = _dot_nt(q, k) * scale + t_ref[hh]
        sc = _dot_nt(q, kc_ref[:, cols]) * scale
        m = jnp.maximum(jnp.max(s, axis=-1, keepdims=True), jnp.max(sc, axis=-1, keepdims=True))
        p = jnp.exp(s - m)
        pc = jnp.exp(sc - m)
        den = jnp.sum(p, axis=-1, keepdims=True) + jnp.sum(pc, axis=-1, keepdims=True)
        o = _dot(p.astype(BF16), v) + _dot(pc.astype(BF16), vc_ref[:, cols])
        o_ref[:, cols] = (o / den).astype(o_ref.dtype)


def neighbourhood_attention(u, s, tables, l):
    c = u.shape[0] - s
    rows = s // GRID_W
    nblk = rows // NA_QROWS
    tq = NA_QROWS * GRID_W
    hps = NA_HEADS_PER_STEP
    ng = NA_HEADS // hps
    hw = hps * NA_HEAD_DIM

    def variant(i):
        return jnp.where(i == 0, 0, jnp.where(i == nblk - 1, 2, 1))

    return pl.pallas_call(
        functools.partial(_na_kernel, rows=rows),
        grid=(ng, nblk),
        in_specs=[pl.BlockSpec((tq, hw), lambda h, i: (i, h)),
                  pl.BlockSpec((s, hw), lambda h, i: (0, ng + h)),
                  pl.BlockSpec((s, hw), lambda h, i: (0, 2 * ng + h)),
                  pl.BlockSpec((c, hw), lambda h, i: (s // c, ng + h)),
                  pl.BlockSpec((c, hw), lambda h, i: (s // c, 2 * ng + h)),
                  pl.BlockSpec((None, None, hps, tq, NA_SLAB * GRID_W), lambda h, i: (l, variant(i), h, 0, 0))],
        out_specs=pl.BlockSpec((tq, hw), lambda h, i: (i, h)),
        out_shape=jax.ShapeDtypeStruct((s, NA_WIDTH), BF16),
        compiler_params=_params(("arbitrary", "arbitrary"),
                                4 * s * hw * 2 + 2 * hps * tq * NA_SLAB * GRID_W * 4 + (12 << 20)),
        name="na_latent",
    )(u, u, u, u, u, tables)


def _full_attn_kernel(q_ref, k_ref, v_ref, o_ref):
    s = _dot_nt(q_ref[...], k_ref[...]) * NA_HEAD_DIM ** -0.5
    p = jnp.exp(s - jnp.max(s, axis=-1, keepdims=True))
    den = jnp.sum(p, axis=-1, keepdims=True)
    o_ref[...] = (_dot(p.astype(BF16), v_ref[...]) / den).astype(o_ref.dtype)


def full_attention(u, s):
    c = u.shape[0] - s
    b0 = s // c
    h_ = NA_HEADS
    hd = NA_HEAD_DIM
    return pl.pallas_call(
        _full_attn_kernel,
        grid=(h_,),
        in_specs=[pl.BlockSpec((c, hd), lambda h: (b0, h)),
                  pl.BlockSpec((c, hd), lambda h: (b0, h_ + h)),
                  pl.BlockSpec((c, hd), lambda h: (b0, 2 * h_ + h))],
        out_specs=pl.BlockSpec((c, hd), lambda h: (0, h)),
        out_shape=jax.ShapeDtypeStruct((c, NA_WIDTH), BF16),
        compiler_params=_params(("arbitrary",), 16 << 20),
        name="na_context",
    )(u, u, u)


def _dft_cs(n):
    idx = np.arange(n)
    ang = 2.0 * np.pi * ((idx[:, None] * idx[None, :]) % n) / n
    return np.cos(ang), np.sin(ang)


FN_ROWS = 8


def _fn1_kernel(w_ref, x_ref, tr_ref, ti_ref, z_ref, *, r):
    for s in range(FN_ROWS):
        y = _dot(w_ref[...], x_ref[:, s, :].astype(BF16))
        yr, yi = y[:r], y[r:]
        tr, ti = tr_ref[s], ti_ref[s]
        z_ref[pl.ds(0, r), s, :] = yr * tr - yi * ti
        z_ref[pl.ds(r, r), s, :] = yr * ti + yi * tr


def _fn2_kernel(zr_ref, zi_ref, kc_ref, ks_ref, c_ref, s_ref, w_ref, o_ref, *, scale):
    zr, zi = zr_ref[...].astype(BF16), zi_ref[...].astype(BF16)
    kc, ks = kc_ref[...], ks_ref[...]
    xr = (_dot(kc, zr) + _dot(ks, zi)).astype(BF16)
    xi = (_dot(kc, zi) - _dot(ks, zr)).astype(BF16)
    cw = GRID_W
    for g in range(FN_GROUPS):
        lo = g * FN_GROUP_DIM
        f = _dot(xr[:, lo:lo + FN_GROUP_DIM], c_ref[...]) + _dot(xi[:, lo:lo + FN_GROUP_DIM], s_ref[...])
        y = _dot((f * scale).astype(BF16), w_ref[g].astype(BF16))
        for kk in range(FN_ROWS):
            o_ref[:, kk, lo:lo + FN_GROUP_DIM] = y[kk * cw:(kk + 1) * cw]


def fourier_latent(u, l, w_fn):
    cw = GRID_W
    r = l // cw
    c_r, s_r = _dft_cs(r)
    w1 = jnp.asarray(np.concatenate([c_r, -s_r], axis=0), BF16)
    ang = 2.0 * np.pi * (np.arange(cw)[:, None] * np.arange(r)[None, :]) / l
    tr = jnp.asarray(np.cos(ang)[:, :, None], F32)
    ti = jnp.asarray(-np.sin(ang)[:, :, None], F32)
    z = pl.pallas_call(
        functools.partial(_fn1_kernel, r=r),
        grid=(cw // FN_ROWS,),
        in_specs=[pl.BlockSpec((2 * r, r), lambda j: (0, 0)),
                  pl.BlockSpec((r, FN_ROWS, FN_WIDTH), lambda j: (0, j, 0)),
                  pl.BlockSpec((FN_ROWS, r, 1), lambda j: (j, 0, 0)),
                  pl.BlockSpec((FN_ROWS, r, 1), lambda j: (j, 0, 0))],
        out_specs=pl.BlockSpec((2 * r, FN_ROWS, FN_WIDTH), lambda j: (0, j, 0)),
        out_shape=jax.ShapeDtypeStruct((2 * r, cw, FN_WIDTH), F32),
        compiler_params=_params(("arbitrary",), 40 << 20),
        name="fourier_stage1",
    )(w1, u.reshape(u.shape[0] // cw, cw, FN_WIDTH), tr, ti)
    z2 = z.reshape(2 * r * cw, FN_WIDTH)
    c_w, s_w = _dft_cs(cw)
    eye = np.eye(FN_ROWS)
    kc = jnp.asarray(np.kron(eye, c_w), BF16)
    ks = jnp.asarray(np.kron(eye, s_w), BF16)
    c_c, s_c = _dft_cs(FN_GROUP_DIM)
    nb = r // FN_ROWS
    tb = FN_ROWS * cw
    out = pl.pallas_call(
        functools.partial(_fn2_kernel, scale=float((l * FN_GROUP_DIM) ** -0.5)),
        grid=(nb,),
        in_specs=[pl.BlockSpec((tb, FN_WIDTH), lambda b: (b, 0)),
                  pl.BlockSpec((tb, FN_WIDTH), lambda b: (nb + b, 0)),
                  pl.BlockSpec((tb, tb), lambda b: (0, 0)),
                  pl.BlockSpec((tb, tb), lambda b: (0, 0)),
                  pl.BlockSpec((FN_GROUP_DIM, FN_GROUP_DIM), lambda b: (0, 0)),
                  pl.BlockSpec((FN_GROUP_DIM, FN_GROUP_DIM), lambda b: (0, 0)),
                  pl.BlockSpec((FN_GROUPS, FN_GROUP_DIM, FN_GROUP_DIM), lambda b: (0, 0, 0))],
        out_specs=pl.BlockSpec((cw, FN_ROWS, FN_WIDTH), lambda b: (0, b, 0)),
        out_shape=jax.ShapeDtypeStruct((cw, r, FN_WIDTH), F32),
        compiler_params=_params(("arbitrary",), 32 << 20),
        name="fourier_stage2",
    )(z2, z2, kc, ks, jnp.asarray(c_c, BF16), jnp.asarray(s_c, BF16), w_fn)
    return out.reshape(l, FN_WIDTH)


def _fn_ctx_kernel(u_ref, cl_ref, sl_ref, c_ref, s_ref, w_ref, o_ref, *, scale):
    u = u_ref[...].astype(BF16)
    gr = _dot(cl_ref[...], u).astype(BF16)
    gi = (-_dot(sl_ref[...], u)).astype(BF16)
    for g in range(FN_GROUPS):
        lo = g * FN_GROUP_DIM
        f = _dot(gr[:, lo:lo + FN_GROUP_DIM], c_ref[...]) + _dot(gi[:, lo:lo + FN_GROUP_DIM], s_ref[...])
        o_ref[:, lo:lo + FN_GROUP_DIM] = _dot((f * scale).astype(BF16), w_ref[g].astype(BF16)).astype(o_ref.dtype)


def fourier_context(u, s, w_fn):
    c = u.shape[0] - s
    c_l, s_l = _dft_cs(c)
    c_c, s_c = _dft_cs(FN_GROUP_DIM)
    gd = FN_GROUP_DIM
    whole = lambda shape: pl.BlockSpec(shape, lambda i: (0,) * len(shape))
    return pl.pallas_call(
        functools.partial(_fn_ctx_kernel, scale=float((c * FN_GROUP_DIM) ** -0.5)),
        grid=(1,),
        in_specs=[pl.BlockSpec((c, FN_WIDTH), lambda i: (s // c, 0)), whole((c, c)), whole((c, c)),
                  whole((gd, gd)), whole((gd, gd)), whole((FN_GROUPS, gd, gd))],
        out_specs=whole((c, FN_WIDTH)),
        out_shape=jax.ShapeDtypeStruct((c, FN_WIDTH), BF16),
        compiler_params=_params(("arbitrary",), 16 << 20),
        name="fourier_context",
    )(u, jnp.asarray(c_l, BF16), jnp.asarray(s_l, BF16), jnp.asarray(c_c, BF16), jnp.asarray(s_c, BF16), w_fn)


def _split_hi_lo(x):
    hi = x.astype(BF16)
    return hi, (x - hi.astype(F32)).astype(BF16)


def _gla_chunk(q_ref, k_ref, v_ref, lr_ref, cos_ref, sin_ref, w2_ref, bg_ref, perm_ref, st_ref, o_ref, rev):
    cs = GLA_CHUNK
    sub = GLA_SUB
    nsub = cs // sub
    z = _dot(lr_ref[...].astype(BF16), w2_ref[...].astype(BF16)) + bg_ref[...]
    la = -(jnp.maximum(-z, 0.0) + jnp.log(1.0 + jnp.exp(-jnp.abs(z)))) * (1.0 / GLA_GATE_TEMP)
    ri = lax.broadcasted_iota(jnp.int32, (cs, cs), 0)
    ci = lax.broadcasted_iota(jnp.int32, (cs, cs), 1)
    tri = jnp.where((ci >= ri) if rev else (ci <= ri), 1.0, 0.0).astype(BF16)
    la_hi, la_lo = _split_hi_lo(la)
    bcum = _dot(tri, la_hi) + _dot(tri, la_lo)
    edge = 0 if rev else cs - 1
    row_id = lax.broadcasted_iota(jnp.int32, (cs, GLA_DKP), 0)
    sub_r = lax.broadcasted_iota(jnp.int32, (sub, GLA_DKP), 0)
    lane_c = lax.broadcasted_iota(jnp.int32, (sub, cs), 1)
    cos, sin = cos_ref[...], sin_ref[...]
    perm = perm_ref[...]
    qscale = GLA_DK ** -0.5
    for h in range(GLA_HEADS):
        ks_ = slice(h * GLA_DKP, (h + 1) * GLA_DKP)
        vs_ = slice(h * GLA_DV, (h + 1) * GLA_DV)
        qb, kb = q_ref[:, ks_], k_ref[:, ks_]
        q = (qb.astype(F32) * cos + _dot(qb, perm) * sin) * qscale
        k = kb.astype(F32) * cos + _dot(kb, perm) * sin
        v = v_ref[:, vs_]
        b = bcum[:, ks_]
        b_edge = b[edge:edge + 1, :]
        st = st_ref[h]
        o = _dot_nt((q * jnp.exp(b)).astype(BF16), st.astype(BF16))
        slabs = []
        for blk in range(nsub):
            lo = blk * sub
            q_i = q[lo:lo + sub]
            b_i = b[lo:lo + sub]
            k_i = k[lo:lo + sub]
            acc = jnp.zeros((sub, cs), F32)
            if rev and blk < nsub - 1:
                ref_row = b[lo + sub:lo + sub + 1, :]
                outside = row_id >= lo + sub
            elif (not rev) and blk > 0:
                ref_row = b[lo - 1:lo, :]
                outside = row_id < lo
            else:
                ref_row = None
            if ref_row is not None:
                qe = q_i * jnp.exp(b_i - ref_row)
                ke = jnp.where(outside, k * jnp.exp(jnp.where(outside, ref_row - b, 0.0)), 0.0)
                acc = acc + _dot_nt(qe.astype(BF16), ke.astype(BF16))
            for j in range(sub):
                keep = (sub_r <= j) if rev else (sub_r >= j)
                d = jnp.exp(jnp.where(keep, b_i - b_i[j:j + 1, :], NEG))
                col = jnp.sum(q_i * d * k_i[j:j + 1, :], axis=-1, keepdims=True)
                acc = acc + jnp.where(lane_c == lo + j, col, 0.0)
            slabs.append(acc)
        attn = jnp.concatenate(slabs, axis=0)
        o = o + _dot(attn.astype(BF16), v)
        o_ref[:, vs_] = o
        kend = (k * jnp.exp(b_edge - b)).astype(BF16)
        st_ref[h] = st * jnp.exp(b_edge) + _dot_tn(v, kend)


def _gla_kernel(*refs, nchunks):
    fwd, bwd = refs[0:6], refs[6:12]
    w2f_ref, bgf_ref, w2b_ref, bgb_ref, perm_ref, s0_ref = refs[12:18]
    of_ref, ob_ref, sfin_ref, st_ref = refs[18:22]
    step = pl.program_id(0)

    @pl.when(step == 0)
    def _():
        st_ref[...] = s0_ref[...]

    _gla_chunk(*fwd, w2f_ref, bgf_ref, perm_ref, st_ref.at[0], of_ref, False)
    _gla_chunk(*bwd, w2b_ref, bgb_ref, perm_ref, st_ref.at[1], ob_ref, True)

    @pl.when(step == nchunks - 1)
    def _():
        sfin_ref[...] = st_ref[...]


def gla_scan(u_qk, u_vr, lr, cos, sin, w2p, bgp, perm, s0, row0, l):
    n = l // GLA_CHUNK
    c0 = row0 // GLA_CHUNK
    cs = GLA_CHUNK
    kw = GLA_HEADS * GLA_DKP
    rank2 = lr.shape[1]
    st_shape = (2, GLA_HEADS, GLA_DV, GLA_DKP)
    ins = []
    for ch in (lambda s: c0 + s, lambda s: c0 + n - 1 - s):
        ins += [pl.BlockSpec((cs, kw), lambda s, ch=ch: (ch(s), 0)),
                pl.BlockSpec((cs, kw), lambda s, ch=ch: (ch(s), 1)),
                pl.BlockSpec((cs, GLA_VW), lambda s, ch=ch: (ch(s), 0)),
                pl.BlockSpec((cs, rank2), lambda s, ch=ch: (ch(s), 0)),
                pl.BlockSpec((cs, GLA_DKP), lambda s, ch=ch: (ch(s), 0)),
                pl.BlockSpec((cs, GLA_DKP), lambda s, ch=ch: (ch(s), 0))]
    for dr in (0, 1):
        ins += [pl.BlockSpec((None, rank2, kw), lambda s, dr=dr: (dr, 0, 0)),
                pl.BlockSpec((None, 1, kw), lambda s, dr=dr: (dr, 0, 0))]
    ins += [pl.BlockSpec((GLA_DKP, GLA_DKP), lambda s: (0, 0)),
            pl.BlockSpec(st_shape, lambda s: (0, 0, 0, 0))]
    seq = (u_qk, u_qk, u_vr, lr, cos, sin)
    return pl.pallas_call(
        functools.partial(_gla_kernel, nchunks=n),
        grid=(n,),
        in_specs=ins,
        out_specs=[pl.BlockSpec((cs, GLA_VW), lambda s: (s, 0)),
                   pl.BlockSpec((cs, GLA_VW), lambda s: (n - 1 - s, 0)),
                   pl.BlockSpec(st_shape, lambda s: (0, 0, 0, 0))],
        out_shape=[jax.ShapeDtypeStruct((l, GLA_VW), F32), jax.ShapeDtypeStruct((l, GLA_VW), F32),
                   jax.ShapeDtypeStruct(st_shape, F32)],
        scratch_shapes=[pltpu.VMEM(st_shape, F32)],
        compiler_params=_params(("arbitrary",), 40 << 20),
        name="gla_scan",
    )(*seq, *seq, w2p, bgp, w2p, bgp, perm, s0)


def _gla_out_kernel(of_ref, ob_ref, r_ref, g_ref, o_ref):
    o = of_ref[...] + ob_ref[...]
    g = g_ref[...]
    r = r_ref[...].astype(F32)
    for h in range(GLA_HEADS):
        sl = slice(h * GLA_DV, (h + 1) * GLA_DV)
        oh = o[:, sl]
        on = oh * lax.rsqrt(jnp.mean(oh * oh, axis=-1, keepdims=True) + EPS) * g[:, sl]
        o_ref[:, sl] = (on * _silu(r[:, sl])).astype(o_ref.dtype)


def gla_output(o_f, o_b, u_vr, row0, g_gla, tm=256):
    l = o_f.shape[0]
    tm = min(tm, l)
    b0 = row0 // tm
    return pl.pallas_call(
        _gla_out_kernel,
        grid=(l // tm,),
        in_specs=[pl.BlockSpec((tm, GLA_VW), lambda i: (i, 0)),
                  pl.BlockSpec((tm, GLA_VW), lambda i: (i, 0)),
                  pl.BlockSpec((tm, GLA_VW), lambda i: (b0 + i, 1)),
                  pl.BlockSpec((1, GLA_VW), lambda i: (0, 0))],
        out_specs=pl.BlockSpec((tm, GLA_VW), lambda i: (i, 0)),
        out_shape=jax.ShapeDtypeStruct((l, GLA_VW), BF16),
        compiler_params=_params(("arbitrary",), 24 << 20),
        name="gla_output",
    )(o_f, o_b, u_vr, g_gla.reshape(1, GLA_VW))


def rope_tables(n_tokens):
    seg = GLA_DK // 2
    half = seg // 2
    inv = ROPE_THETA ** (-jnp.arange(half, dtype=F32) / half)
    pos = jnp.arange(n_tokens)
    ang_r = (pos // GRID_W).astype(F32)[:, None] * inv
    ang_c = (pos % GRID_W).astype(F32)[:, None] * inv
    pad1 = jnp.ones((n_tokens, GLA_DKP - GLA_DK), F32)
    pad0 = jnp.zeros((n_tokens, GLA_DKP - GLA_DK), F32)
    cos = jnp.concatenate([jnp.cos(ang_r)] * 2 + [jnp.cos(ang_c)] * 2 + [pad1], axis=1)
    sin = jnp.concatenate([jnp.sin(ang_r)] * 2 + [jnp.sin(ang_c)] * 2 + [pad0], axis=1)
    return cos, sin


def rope_perm():
    seg = GLA_DK // 2
    half = seg // 2
    p = np.zeros((GLA_DKP, GLA_DKP), np.float32)
    for s0 in (0, seg):
        for j in range(half):
            p[s0 + half + j, s0 + j] = -1.0
            p[s0 + j, s0 + half + j] = 1.0
    return jnp.asarray(p, BF16)


def _pad_heads(w):
    lead = w.shape[:-1]
    w = w.reshape(lead + (GLA_HEADS, GLA_DK))
    w = jnp.pad(w, [(0, 0)] * len(lead) + [(0, 0), (0, GLA_DKP - GLA_DK)])
    return w.reshape(lead + (GLA_HEADS * GLA_DKP,))


MOE_TR = 256
MOE_GATHER_DEPTH = 4


def moe_plan(route):
    t = route.shape[0]
    tr = MOE_TR
    n_a = 2 * t
    e = route[:, :2].astype(jnp.int32).reshape(-1)
    iota = jnp.arange(n_a, dtype=jnp.int32)
    e_sorted, order = lax.sort((e, iota), num_keys=1)
    counts = jnp.sum((e[:, None] == jnp.arange(MOE_EXPERTS, dtype=jnp.int32)[None]).astype(jnp.int32), axis=0)
    padded = ((counts + tr - 1) // tr) * tr
    ends = jnp.cumsum(padded)
    starts = ends - padded
    shift = starts - (jnp.cumsum(counts) - counts)
    _, pos = lax.sort((order, iota + shift[e_sorted]), num_keys=1)
    npad = n_a + MOE_EXPERTS * tr
    ntiles = npad // tr
    tile_e = jnp.minimum(jnp.searchsorted(ends, jnp.arange(ntiles, dtype=jnp.int32) * tr, side="right"),
                         MOE_EXPERTS - 1).astype(jnp.int32)
    row = jnp.arange(npad, dtype=jnp.int32)
    per_row = lambda v: jnp.repeat(v[tile_e], tr)
    valid = (row - per_row(starts)) < per_row(counts)
    src = jnp.where(valid, order[jnp.clip(row - per_row(shift), 0, n_a - 1)] // 2, 0)
    n_used = (ends[-1] // tr).reshape(1).astype(jnp.int32)
    return src.astype(jnp.int32), tile_e, n_used, pos.astype(jnp.int32)


ROW_GROUP = 8


def _row_gather_start(idx_ref, first, src_hbm, dst, sem):
    def body(g, carry):
        for u in range(ROW_GROUP):
            pltpu.make_async_copy(src_hbm.at[pl.ds(idx_ref[first + g * ROW_GROUP + u], 1)],
                                  dst.at[g, pl.ds(u, 1)], sem).start()
        return carry
    lax.fori_loop(0, dst.shape[0], body, 0)


def _row_gather_wait(dst, sem):
    pltpu.make_async_copy(dst, dst, sem).wait()


def _rows(x):
    return x.reshape(x.shape[0] * ROW_GROUP, x.shape[2])


def _moe_ffn_kernel(src_ref, te_ref, nu_ref, x_hbm, wgu_ref, wdn_ref, o_ref, buf, sem):
    i = pl.program_id(0)
    n_used = nu_ref[0]
    tr = MOE_TR
    depth = buf.shape[0]
    slot = lax.rem(i, depth)

    @pl.when(i == 0)
    def _():
        for ahead in range(depth - 1):
            @pl.when(ahead < n_used)
            def _(ahead=ahead):
                _row_gather_start(src_ref, ahead * tr, x_hbm, buf.at[ahead], sem.at[ahead])

    @pl.when(i < n_used)
    def _():
        nxt = i + depth - 1

        @pl.when(nxt < n_used)
        def _():
            s_ = lax.rem(nxt, depth)
            _row_gather_start(src_ref, nxt * tr, x_hbm, buf.at[s_], sem.at[s_])

        _row_gather_wait(buf.at[slot], sem.at[slot])
        x_lo, x_hi = _unpack_bf16_pairs(_rows(buf[slot]))
        half = x_lo.shape[1]
        gu = (_dot(x_lo.astype(BF16), wgu_ref[:half, :].astype(BF16))
              + _dot(x_hi.astype(BF16), wgu_ref[half:, :].astype(BF16)))
        hid = _silu(gu[:, :MOE_HIDDEN]) * gu[:, MOE_HIDDEN:]
        o_ref[...] = _pack_bf16_pairs(_dot(hid.astype(BF16), wdn_ref[...].astype(BF16)))

    @pl.when(i >= n_used)
    def _():
        o_ref[...] = jnp.zeros(o_ref.shape, o_ref.dtype)


def moe_ffn(xn, w_gu, w_dn, l, src, tile_e, n_used):
    t = xn.shape[0]
    d = 2 * xn.shape[1]
    tr = MOE_TR
    npad = src.shape[0]
    f = MOE_HIDDEN
    grid_spec = pltpu.PrefetchScalarGridSpec(
        num_scalar_prefetch=3,
        grid=(npad // tr,),
        in_specs=[pl.BlockSpec(memory_space=pl.ANY),
                  pl.BlockSpec((None, None, d, 2 * f), lambda i, s, te, nu: (l, te[i], 0, 0)),
                  pl.BlockSpec((None, None, f, d), lambda i, s, te, nu: (l, te[i], 0, 0))],
        out_specs=pl.BlockSpec((tr, d // 2), lambda i, s, te, nu: (i, 0)),
        scratch_shapes=[pltpu.VMEM((MOE_GATHER_DEPTH, tr // ROW_GROUP, ROW_GROUP, d // 2), jnp.uint32),
                        pltpu.SemaphoreType.DMA((MOE_GATHER_DEPTH,))],
    )
    return pl.pallas_call(
        _moe_ffn_kernel,
        grid_spec=grid_spec,
        out_shape=jax.ShapeDtypeStruct((npad, d // 2), jnp.uint32),
        compiler_params=_params(("arbitrary",), 4 * tr * d * 4 + 2 * 3 * d * f * 4 + 6 * tr * d * 4),
        name="moe_ffn",
    )(src, tile_e, n_used, xn, w_gu, w_dn)


def _moe_combine_kernel(pos_ref, y_hbm, x_ref, route_ref, gate_ref, o_ref, buf, sem):
    i = pl.program_id(0)
    n = pl.num_programs(0)
    tm = x_ref.shape[0]
    depth = buf.shape[0]
    slot = lax.rem(i, depth)

    def start(tile):
        s = lax.rem(tile, depth)
        _row_gather_start(pos_ref, 2 * tm * tile, y_hbm, buf.at[s], sem.at[s])

    @pl.when(i == 0)
    def _():
        for ahead in range(depth - 1):
            @pl.when(ahead < n)
            def _(ahead=ahead):
                start(ahead)

    @pl.when(i + depth - 1 < n)
    def _():
        start(i + depth - 1)

    _row_gather_wait(buf.at[slot], sem.at[slot])
    route = route_ref[...]
    lane = lax.broadcasted_iota(jnp.int32, route.shape, 1)
    w1 = _lane_pick(route, lane, 2)
    w2 = _lane_pick(route, lane, 3)
    cur = buf.at[slot]
    half = cur.shape[2]
    gm = tm // ROW_GROUP
    cw = min(4 * LANE, half)
    for c0 in range(0, half, cw):
        first = _unpack_bf16_pairs(_rows(cur[:gm, :, c0:c0 + cw]))
        second = _unpack_bf16_pairs(_rows(cur[gm:, :, c0:c0 + cw]))
        for part in (0, 1):
            cols = slice(part * half + c0, part * half + c0 + cw)
            o_ref[:, cols] = x_ref[:, cols] + gate_ref[:, cols] * (w1 * first[part] + w2 * second[part])


def moe_combine(y, x, route, gate, pos, row0, tm=128):
    t, d = x.shape
    pos = pos[2 * row0:2 * (row0 + t)].reshape(t // tm, tm, 2).transpose(0, 2, 1).reshape(-1)
    blk0 = row0 // tm
    grid_spec = pltpu.PrefetchScalarGridSpec(
        num_scalar_prefetch=1,
        grid=(t // tm,),
        in_specs=[pl.BlockSpec(memory_space=pl.ANY),
                  pl.BlockSpec((tm, d), lambda i, p: (i, 0)),
                  pl.BlockSpec((tm, LANE), lambda i, p: (i + blk0, 0)),
                  pl.BlockSpec((1, d), lambda i, p: (0, 0))],
        out_specs=pl.BlockSpec((tm, d), lambda i, p: (i, 0)),
        scratch_shapes=[pltpu.VMEM((MOE_GATHER_DEPTH, 2 * tm // ROW_GROUP, ROW_GROUP, d // 2), jnp.uint32),
                        pltpu.SemaphoreType.DMA((MOE_GATHER_DEPTH,))],
    )
    return pl.pallas_call(
        _moe_combine_kernel,
        grid_spec=grid_spec,
        out_shape=jax.ShapeDtypeStruct((t, d), F32),
        compiler_params=_params(("arbitrary",), 4 * tm * d * 4 + 8 * tm * d * 4),
        name="moe_combine",
    )(pos, y, x, route, gate.reshape(1, d))


def _layer(l, x, xc, mod, last, consts, w):
    (g_mix, w_in, rpb, w_fn, w_g2, b_g, g_gla, w_out, g_ffn, w_rg, b_rg, w_re, b_re, w_gu, w_dn) = w
    cos, sin, perm, na_tab = consts
    d = D_MODEL
    s_len, c_len = x.shape[0], xc.shape[0]
    m_x = mod[l, 0].reshape(N_MOD, d)
    m_c = mod[l, 1].reshape(N_MOD, d)

    xn, lr = norm_proj([(x, m_x[0], m_x[1]), (xc, m_c[0], m_c[1])], g_mix[l], w_in, l, OFF_LR, 2 * GLA_GATE_RANK)
    tm = (s_len + c_len) // 8
    u_na = matmul(xn, w_in, (l,), OFF_NA, 3 * NA_WIDTH, 512, tm, BF16, w_transposed=True, name="in_na")
    u_fn = matmul(xn, w_in, (l,), OFF_FN, FN_WIDTH, 512, tm, F32, w_transposed=True, name="in_fn")
    u_qk = gla_qk_proj(xn, w_in, l, tm)
    u_vr = matmul(xn, w_in, (l,), OFF_GVR, 2 * GLA_VW, 512, tm, BF16, w_transposed=True, name="in_gvr")

    r_ = GLA_GATE_RANK
    w2p = jnp.stack([jnp.pad(_pad_heads(w_g2[l, dr]), ((dr * r_, (1 - dr) * r_), (0, 0))) for dr in (0, 1)])
    bgp = _pad_heads(b_g[l]).reshape(2, 1, -1)
    zero = jnp.zeros((2, GLA_HEADS, GLA_DV, GLA_DKP), F32)
    *outs_c, s_c = gla_scan(u_qk, u_vr, lr, cos, sin, w2p, bgp, perm, zero, s_len, c_len)
    *outs_x, _ = gla_scan(u_qk, u_vr, lr, cos, sin, w2p, bgp, perm, s_c, 0, s_len)

    mix = (neighbourhood_attention(u_na, s_len, na_tab, l),
           fourier_latent(u_fn, s_len, w_fn[l]),
           gla_output(outs_x[0], outs_x[1], u_vr, 0, g_gla[l]))
    x = matmul(mix, w_out, (l,), 0, d, 512, 1024, F32, res=x, gate=m_x[2], name="out_proj")

    w_route = jnp.pad(jnp.concatenate([w_re[l], w_rg[l]], axis=1),
                      ((0, 0), (0, LANE - MOE_EXPERTS - MOE_GROUPS)))
    b_route = jnp.pad(jnp.concatenate([b_re[l], b_rg[l]]), (0, LANE - MOE_EXPERTS - MOE_GROUPS)).reshape(1, LANE)
    streams = [(x, m_x)]
    if not last:
        mix_c = (full_attention(u_na, s_len), fourier_context(u_fn, s_len, w_fn[l]),
                 gla_output(outs_c[0], outs_c[1], u_vr, s_len, g_gla[l]))
        xc = matmul(mix_c, w_out, (l,), 0, d, 512, 1024, F32, res=xc, gate=m_c[2], name="out_proj_ctx")
        streams.append((xc, m_c))

    yn, route = norm_route([(y, m[3], m[4]) for y, m in streams], g_ffn[l], w_route, b_route)
    src, tile_e, n_used, pos = moe_plan(route)
    y_exp = moe_ffn(yn, w_gu, w_dn, l, src, tile_e, n_used)
    outs, row0 = [], 0
    for y, m in streams:
        outs.append(moe_combine(y_exp, y, route, m[5], pos, row0))
        row0 += y.shape[0]
    return outs[0], (outs[1] if not last else xc)


def kernel(x, c, ctx, c_ctx, w_ada, b_ada, g_mix, w_in, rpb, w_fn, w_g2, b_g, g_gla, w_out,
           g_ffn, w_rg, b_rg, w_re, b_re, w_gu, w_dn, g_final):
    assert x.shape[0] == 1 and c.shape[0] == 1
    d = D_MODEL
    s_len = x.shape[1]
    c_len = ctx.shape[1]
    cc = jnp.concatenate([c, c_ctx[None], jnp.zeros((6, d), F32)], axis=0)
    mod = ada_mod(cc, w_ada, b_ada)
    cos, sin = rope_tables(s_len)
    cos = jnp.concatenate([cos, jnp.ones((c_len, GLA_DKP), F32)], axis=0)
    sin = jnp.concatenate([sin, jnp.zeros((c_len, GLA_DKP), F32)], axis=0)
    consts = (cos, sin, rope_perm(), na_tables(rpb, s_len // GRID_W))
    w_in = jnp.swapaxes(w_in, 1, 2)
    w = (g_mix, w_in, rpb, w_fn, w_g2, b_g, g_gla, w_out, g_ffn, w_rg, b_rg, w_re, b_re, w_gu, w_dn)
    xs, xc = x[0], ctx[0]
    for l in range(DEPTH):
        xs, xc = _layer(l, xs, xc, mod, l == DEPTH - 1, consts, w)
    return final_norm(xs, g_final)[None]
```

```python
import functools

import numpy as np
import jax
import jax.numpy as jnp
from jax import lax
from jax.experimental import pallas as pl
from jax.experimental.pallas import tpu as pltpu

F32 = jnp.float32
BF16 = jnp.bfloat16

D_MODEL = 4096
DEPTH = 4
GRID_W = 64
EPS = 1e-6
N_MOD = 6

NA_HEADS = 12
NA_HEAD_DIM = 128
NA_WIN_H = 8
NA_WIN_W = 16
NA_WIDTH = NA_HEADS * NA_HEAD_DIM

FN_GROUPS = 8
FN_GROUP_DIM = 128
FN_WIDTH = FN_GROUPS * FN_GROUP_DIM

GLA_HEADS = 4
GLA_DK = 192
GLA_DKP = 256
GLA_DV = 384
GLA_KW = GLA_HEADS * GLA_DK
GLA_VW = GLA_HEADS * GLA_DV
GLA_GATE_RANK = 16
GLA_GATE_TEMP = 16.0
GLA_CHUNK = 64
GLA_SUB = 8
ROPE_THETA = 10000.0

MOE_GROUPS = 4
MOE_EXPERTS_PER_GROUP = 8
MOE_EXPERTS = MOE_GROUPS * MOE_EXPERTS_PER_GROUP
MOE_HIDDEN = 192

OFF_NA = 0
OFF_FN = 3 * NA_WIDTH
OFF_GQK = OFF_FN + FN_WIDTH
OFF_GVR = OFF_GQK + 2 * GLA_KW
OFF_LR = OFF_GVR + 2 * GLA_VW

LANE = 128
NEG = -1e30
V7X_VMEM_BUDGET = 56 * 1024 * 1024


def _params(sem, vmem_bytes):
    return pltpu.CompilerParams(dimension_semantics=sem,
                                vmem_limit_bytes=int(min(max(vmem_bytes, 16 << 20), V7X_VMEM_BUDGET)))


def _dot(a, b):
    return jnp.dot(a, b, preferred_element_type=F32)


def _dot_nt(a, b):
    return lax.dot_general(a, b, (((1,), (1,)), ((), ())), preferred_element_type=F32)


def _dot_tn(a, b):
    return lax.dot_general(a, b, (((0,), (0,)), ((), ())), preferred_element_type=F32)


def _silu(x):
    return x * (1.0 / (1.0 + jnp.exp(-x)))


def _ada_kernel(c_ref, w_ref, b_ref, o_ref):
    s = _silu(c_ref[...]).astype(BF16)
    o_ref[0] = _dot(s, w_ref[0].astype(BF16)) + b_ref[0]


def ada_mod(cc, w_ada, b_ada, tn=512):
    n, d, nd = w_ada.shape
    return pl.pallas_call(
        _ada_kernel,
        grid=(n, nd // tn),
        in_specs=[pl.BlockSpec((8, d), lambda l, j: (0, 0)),
                  pl.BlockSpec((1, d, tn), lambda l, j: (l, 0, j)),
                  pl.BlockSpec((1, 1, tn), lambda l, j: (l, 0, j))],
        out_specs=pl.BlockSpec((1, 8, tn), lambda l, j: (l, 0, j)),
        out_shape=jax.ShapeDtypeStruct((n, 8, nd), F32),
        compiler_params=_params(("arbitrary", "arbitrary"), 3 * d * tn * 4 + (4 << 20)),
        name="ada_mod",
    )(cc, w_ada, b_ada.reshape(n, 1, nd))


def _normed(x_ref, g_ref, sh_ref, sc_ref):
    x = x_ref[...]
    y = x * lax.rsqrt(jnp.mean(x * x, axis=-1, keepdims=True) + EPS)
    return (y * g_ref[...]) * (1.0 + sc_ref[...]) + sh_ref[...]


def _norm_proj_kernel(*refs, blocks, layer, col0):
    ns = len(blocks)
    x_refs, g_ref = refs[:ns], refs[ns]
    mods = refs[ns + 1:ns + 1 + 2 * ns]
    w_hbm, xn_ref, p_ref, w_buf, sem = refs[ns + 1 + 2 * ns:]
    i = pl.program_id(0)

    @pl.when(i == 0)
    def _():
        cp = pltpu.make_async_copy(w_hbm.at[layer, pl.ds(col0, w_buf.shape[0]), :], w_buf, sem)
        cp.start()
        cp.wait()

    first = 0
    for s_, nb in enumerate(blocks):
        @pl.when((i >= first) & (i < first + nb))
        def _(s_=s_):
            hb = _normed(x_refs[s_], g_ref, mods[2 * s_], mods[2 * s_ + 1]).astype(BF16)
            xn_ref[...] = hb
            p_ref[...] = _dot_nt(hb, w_buf[...].astype(BF16))
        first += nb


def _lane_pick(v, lane, idx):
    return jnp.sum(jnp.where(lane == idx, v, 0.0), axis=-1, keepdims=True)


def _pack_bf16_pairs(x):
    xf = x.astype(BF16).astype(F32)
    half = xf.shape[1] // 2
    lo = lax.bitcast_convert_type(xf[:, :half], jnp.uint32)
    hi = lax.bitcast_convert_type(xf[:, half:], jnp.uint32)
    return hi | (lo >> 16)


def _unpack_bf16_pairs(p):
    lo = lax.bitcast_convert_type(p << 16, F32)
    hi = lax.bitcast_convert_type(p & jnp.uint32(0xFFFF0000), F32)
    return lo, hi


def _norm_route_kernel(x_ref, g_ref, sh_ref, sc_ref, w_ref, b_ref, xn_ref, route_ref):
    h = _normed(x_ref, g_ref, sh_ref, sc_ref)
    hb = h.astype(BF16)
    xn_ref[...] = _pack_bf16_pairs(h)
    logits = _dot(hb, w_ref[...].astype(BF16)) + b_ref[...]
    lane = lax.broadcasted_iota(jnp.int32, logits.shape, 1).astype(F32)
    far = float(4 * LANE)
    is_g = (lane >= MOE_EXPERTS) & (lane < MOE_EXPERTS + MOE_GROUPS)
    gl = jnp.where(is_g, logits, NEG)
    gmax = jnp.max(gl, axis=-1, keepdims=True)
    g_top = jnp.min(jnp.where(gl == gmax, lane, far), axis=-1, keepdims=True) - MOE_EXPERTS
    g_w = 1.0 / jnp.sum(jnp.where(is_g, jnp.exp(gl - gmax), 0.0), axis=-1, keepdims=True)
    in_grp = (lane >= g_top * MOE_EXPERTS_PER_GROUP) & (lane < (g_top + 1) * MOE_EXPERTS_PER_GROUP)
    el = jnp.where(in_grp, logits, NEG)
    v1 = jnp.max(el, axis=-1, keepdims=True)
    i1 = jnp.min(jnp.where(el == v1, lane, far), axis=-1, keepdims=True)
    el2 = jnp.where(lane == i1, NEG, el)
    v2 = jnp.max(el2, axis=-1, keepdims=True)
    i2 = jnp.min(jnp.where(el2 == v2, lane, far), axis=-1, keepdims=True)
    e2 = jnp.exp(v2 - v1)
    w1 = g_w / (1.0 + e2)
    w2 = g_w * e2 / (1.0 + e2)
    route_ref[...] = (jnp.where(lane == 0.0, i1, 0.0) + jnp.where(lane == 1.0, i2, 0.0)
                      + jnp.where(lane == 2.0, w1, 0.0) + jnp.where(lane == 3.0, w2, 0.0))


def _norm_route_streams_kernel(*refs, blocks):
    ns = len(blocks)
    x_refs, g_ref = refs[:ns], refs[ns]
    mods = refs[ns + 1:ns + 1 + 2 * ns]
    w_ref, b_ref, xn_ref, route_ref = refs[ns + 1 + 2 * ns:]
    i = pl.program_id(0)
    first = 0
    for s_, nb in enumerate(blocks):
        @pl.when((i >= first) & (i < first + nb))
        def _(s_=s_):
            _norm_route_kernel(x_refs[s_], g_ref, mods[2 * s_], mods[2 * s_ + 1], w_ref, b_ref, xn_ref, route_ref)
        first += nb


def _stream_specs(streams, tm, d):
    blocks = tuple(x.shape[0] // tm for x, _, _ in streams)
    ins, args, first = [], [], 0
    for (x, _, _), nb in zip(streams, blocks):
        ins.append(pl.BlockSpec((tm, d), lambda i, first=first, nb=nb: (jnp.clip(i - first, 0, nb - 1), 0)))
        args.append(x)
        first += nb
    return blocks, ins, args


def norm_proj(streams, g, w, layer, col0, ncols, tm=256):
    d = streams[0][0].shape[1]
    blocks, ins, args = _stream_specs(streams, tm, d)
    vec = pl.BlockSpec((1, d), lambda i: (0, 0))
    ins.append(vec)
    args.append(g.reshape(1, d))
    for _, sh, sc in streams:
        ins += [vec, vec]
        args += [sh.reshape(1, d), sc.reshape(1, d)]
    ins.append(pl.BlockSpec(memory_space=pl.ANY))
    args.append(w)
    rows = sum(blocks) * tm
    return pl.pallas_call(
        functools.partial(_norm_proj_kernel, blocks=blocks, layer=layer, col0=col0),
        grid=(sum(blocks),),
        in_specs=ins,
        out_specs=[pl.BlockSpec((tm, d), lambda i: (i, 0)), pl.BlockSpec((tm, ncols), lambda i: (i, 0))],
        out_shape=[jax.ShapeDtypeStruct((rows, d), BF16), jax.ShapeDtypeStruct((rows, ncols), F32)],
        scratch_shapes=[pltpu.VMEM((ncols, d), F32), pltpu.SemaphoreType.DMA(())],
        compiler_params=_params(("arbitrary",), (4 + 4 * len(streams)) * tm * d * 4 + 2 * d * LANE * 4),
        name="norm_proj",
    )(*args)


def norm_route(streams, g, w_small, b_small, tm=256):
    d = streams[0][0].shape[1]
    blocks, ins, args = _stream_specs(streams, tm, d)
    row = lambda i: (0, 0)
    vec = pl.BlockSpec((1, d), row)
    ins.append(vec)
    args.append(g.reshape(1, d))
    for _, sh, sc in streams:
        ins += [vec, vec]
        args += [sh.reshape(1, d), sc.reshape(1, d)]
    ins += [pl.BlockSpec((d, LANE), row), pl.BlockSpec((1, LANE), row)]
    args += [w_small, b_small]
    rows = sum(blocks) * tm
    return pl.pallas_call(
        functools.partial(_norm_route_streams_kernel, blocks=blocks),
        grid=(sum(blocks),),
        in_specs=ins,
        out_specs=[pl.BlockSpec((tm, d // 2), lambda i: (i, 0)), pl.BlockSpec((tm, LANE), lambda i: (i, 0))],
        out_shape=[jax.ShapeDtypeStruct((rows, d // 2), jnp.uint32), jax.ShapeDtypeStruct((rows, LANE), F32)],
        compiler_params=_params(("arbitrary",), (4 + 4 * len(streams)) * tm * d * 4 + 4 * d * LANE * 4),
        name="norm_route",
    )(*args)


def _final_norm_kernel(x_ref, g_ref, o_ref):
    x = x_ref[...]
    o_ref[...] = x * lax.rsqrt(jnp.mean(x * x, axis=-1, keepdims=True) + EPS) * g_ref[...]


def final_norm(x, g, tm=256):
    t, d = x.shape
    return pl.pallas_call(
        _final_norm_kernel,
        grid=(t // tm,),
        in_specs=[pl.BlockSpec((tm, d), lambda i: (i, 0)), pl.BlockSpec((1, d), lambda i: (0, 0))],
        out_specs=pl.BlockSpec((tm, d), lambda i: (i, 0)),
        out_shape=jax.ShapeDtypeStruct((t, d), F32),
        compiler_params=_params(("arbitrary",), 6 * tm * d * 4),
        name="final_norm",
    )(x, g.reshape(1, d))


def _mm_wt_kernel(a_ref, wt_ref, o_ref):
    o_ref[...] = _dot_nt(a_ref[...], wt_ref[...].astype(BF16)).astype(o_ref.dtype)


def _mm_parts_res_kernel(*refs, widths):
    n = len(widths)
    w_ref, r_ref, g_ref, o_ref = refs[n:]
    acc, k0 = None, 0
    for a_ref, wd in zip(refs[:n], widths):
        part = _dot(a_ref[...].astype(BF16), w_ref[k0:k0 + wd, :].astype(BF16))
        acc = part if acc is None else acc + part
        k0 += wd
    o_ref[...] = r_ref[...] + g_ref[...] * acc


def _mm_head_pad_kernel(a_ref, w0_ref, w1_ref, w2_ref, o_ref):
    a = a_ref[...]
    acc = jnp.concatenate([_dot_nt(a, w_ref[...].astype(BF16)) for w_ref in (w0_ref, w1_ref, w2_ref)], axis=1)
    pad = jnp.zeros((a.shape[0], GLA_DKP - GLA_DK), o_ref.dtype)
    for h in range(GLA_HEADS):
        o_ref[:, h * GLA_DKP:h * GLA_DKP + GLA_DK] = acc[:, h * GLA_DK:(h + 1) * GLA_DK].astype(o_ref.dtype)
        o_ref[:, h * GLA_DKP + GLA_DK:(h + 1) * GLA_DKP] = pad


def gla_qk_proj(a, w_in_t, l, tm=1024):
    t, k = a.shape
    tm = min(tm, t)
    tn = GLA_KW // 3
    j0 = OFF_GQK // tn
    w_spec = lambda p: pl.BlockSpec((None, tn, k), lambda i, j: (l, j0 + 3 * j + p, 0))
    kwp = GLA_HEADS * GLA_DKP
    return pl.pallas_call(
        _mm_head_pad_kernel,
        grid=(t // tm, 2),
        in_specs=[pl.BlockSpec((tm, k), lambda i, j: (i, 0)), w_spec(0), w_spec(1), w_spec(2)],
        out_specs=pl.BlockSpec((tm, kwp), lambda i, j: (i, j)),
        out_shape=jax.ShapeDtypeStruct((t, 2 * kwp), BF16),
        compiler_params=_params(("arbitrary", "arbitrary"), 2 * tm * k * 2 + 7 * k * tn * 4 + 8 * tm * kwp * 4),
        name="in_gqk",
    )(a, w_in_t, w_in_t, w_in_t)


def matmul(a, w, w_lead, col0, ncols, tn, tm, out_dtype, res=None, gate=None, name="mm"):
    parts = a if isinstance(a, (tuple, list)) else (a,)
    t = parts[0].shape[0]
    k = sum(p.shape[1] for p in parts)
    tm = min(tm, t)
    nlead = len(w_lead)
    j0 = col0 // tn
    if res is None:
        w_spec = pl.BlockSpec((None,) * nlead + (tn, k), lambda i, j: tuple(w_lead) + (j + j0, 0))
    else:
        w_spec = pl.BlockSpec((None,) * nlead + (k, tn), lambda i, j: tuple(w_lead) + (0, j + j0))
    ins = [pl.BlockSpec((tm, p.shape[1]), lambda i, j: (i, 0)) for p in parts] + [w_spec]
    args = list(parts) + [w]
    wbytes = jnp.dtype(w.dtype).itemsize
    vmem = 2 * tm * k * 2 + 3 * k * tn * wbytes + 6 * tm * tn * 4
    if res is None:
        body = _mm_wt_kernel
    else:
        body = functools.partial(_mm_parts_res_kernel, widths=tuple(p.shape[1] for p in parts))
        ins += [pl.BlockSpec((tm, tn), lambda i, j: (i, j)), pl.BlockSpec((1, tn), lambda i, j: (0, j))]
        args += [res, gate.reshape(1, ncols)]
    return pl.pallas_call(
        body,
        grid=(t // tm, ncols // tn),
        in_specs=ins,
        out_specs=pl.BlockSpec((tm, tn), lambda i, j: (i, j)),
        out_shape=jax.ShapeDtypeStruct((t, ncols), out_dtype),
        compiler_params=_params(("arbitrary", "arbitrary"), vmem),
        name=name,
    )(*args)


NA_QROWS = 4
NA_SLAB = 12
NA_HEADS_PER_STEP = 4


def na_tables(rpb, rows):
    nblk = rows // NA_QROWS
    n_, h_ = rpb.shape[:2]
    qc = np.arange(GRID_W)[:, None]
    kc = np.arange(GRID_W)[None, :]
    c0 = np.clip(qc - NA_WIN_W // 2, 0, GRID_W - NA_WIN_W)
    ok_c = (kc >= c0) & (kc < c0 + NA_WIN_W)
    a = np.arange(NA_QROWS)[:, None]
    b = np.arange(NA_SLAB)[None, :]
    dr_l, ok_l = [], []
    for i in (0, 1, nblk - 1):
        base = int(np.clip(i * NA_QROWS - NA_WIN_H // 2, 0, rows - NA_SLAB))
        r = i * NA_QROWS + a
        r0 = np.clip(r - NA_WIN_H // 2, 0, rows - NA_WIN_H)
        krow = base + b
        ok_l.append((krow >= r0) & (krow < r0 + NA_WIN_H))
        dr_l.append(np.clip(krow - r + NA_WIN_H - 1, 0, 2 * NA_WIN_H - 2))
    dr = np.stack(dr_l)
    ok = np.stack(ok_l)[:, :, None, :, None] & ok_c[None, None, :, None, :]
    ok = jnp.asarray(ok.reshape(3, NA_QROWS * GRID_W, NA_SLAB * GRID_W))
    lo = GRID_W - NA_WIN_W
    padded = jnp.pad(rpb.astype(F32), ((0, 0), (0, 0), (0, 0), (lo, lo)))
    by_col = jnp.stack([padded[..., GRID_W - 1 - q:2 * GRID_W - 1 - q] for q in range(GRID_W)], axis=3)
    variants = []
    for v in range(3):
        rows_ = [jnp.concatenate([by_col[:, :, int(dr[v, a_, b_])] for b_ in range(NA_SLAB)], axis=-1)
                 for a_ in range(NA_QROWS)]
        variants.append(jnp.concatenate(rows_, axis=2))
    bias = jnp.stack(variants, axis=1)
    return jnp.where(ok[None, :, None], bias, NEG)


def _na_kernel(q_ref, k_ref, v_ref, kc_ref, vc_ref, t_ref, o_ref, *, rows):
    i = pl.program_id(1)
    base = jnp.clip(i * NA_QROWS - NA_WIN_H // 2, 0, rows - NA_SLAB) * GRID_W
    base = pl.multiple_of(base, NA_QROWS * GRID_W)
    n_keys = NA_SLAB * GRID_W
    scale = NA_HEAD_DIM ** -0.5
    for hh in range(NA_HEADS_PER_STEP):
        cols = slice(hh * NA_HEAD_DIM, (hh + 1) * NA_HEAD_DIM)
        q = q_ref[:, cols]
        k = k_ref[pl.ds(base, n_keys), cols]
        v = v_ref[pl.ds(base, n_keys), cols]
        s = _dot_nt(q, k) * scale + t_ref[hh]
        sc = _dot_nt(q, kc_ref[:, cols]) * scale
        m = jnp.maximum(jnp.max(s, axis=-1, keepdims=True), jnp.max(sc, axis=-1, keepdims=True))
        p = jnp.exp(s - m)
        pc = jnp.exp(sc - m)
        den = jnp.sum(p, axis=-1, keepdims=True) + jnp.sum(pc, axis=-1, keepdims=True)
        o = _dot(p.astype(BF16), v) + _dot(pc.astype(BF16), vc_ref[:, cols])
        o_ref[:, cols] = (o / den).astype(o_ref.dtype)


def neighbourhood_attention(u, s, tables, l):
    c = u.shape[0] - s
    rows = s // GRID_W
    nblk = rows // NA_QROWS
    tq = NA_QROWS * GRID_W
    hps = NA_HEADS_PER_STEP
    ng = NA_HEADS // hps
    hw = hps * NA_HEAD_DIM

    def variant(i):
        return jnp.where(i == 0, 0, jnp.where(i == nblk - 1, 2, 1))

    return pl.pallas_call(
        functools.partial(_na_kernel, rows=rows),
        grid=(ng, nblk),
        in_specs=[pl.BlockSpec((tq, hw), lambda h, i: (i, h)),
                  pl.BlockSpec((s, hw), lambda h, i: (0, ng + h)),
                  pl.BlockSpec((s, hw), lambda h, i: (0, 2 * ng + h)),
                  pl.BlockSpec((c, hw), lambda h, i: (s // c, ng + h)),
                  pl.BlockSpec((c, hw), lambda h, i: (s // c, 2 * ng + h)),
                  pl.BlockSpec((None, None, hps, tq, NA_SLAB * GRID_W), lambda h, i: (l, variant(i), h, 0, 0))],
        out_specs=pl.BlockSpec((tq, hw), lambda h, i: (i, h)),
        out_shape=jax.ShapeDtypeStruct((s, NA_WIDTH), BF16),
        compiler_params=_params(("arbitrary", "arbitrary"),
                                4 * s * hw * 2 + 2 * hps * tq * NA_SLAB * GRID_W * 4 + (12 << 20)),
        name="na_latent",
    )(u, u, u, u, u, tables)


def _full_attn_kernel(q_ref, k_ref, v_ref, o_ref):
    s = _dot_nt(q_ref[...], k_ref[...]) * NA_HEAD_DIM ** -0.5
    p = jnp.exp(s - jnp.max(s, axis=-1, keepdims=True))
    den = jnp.sum(p, axis=-1, keepdims=True)
    o_ref[...] = (_dot(p.astype(BF16), v_ref[...]) / den).astype(o_ref.dtype)


def full_attention(u, s):
    c = u.shape[0] - s
    b0 = s // c
    h_ = NA_HEADS
    hd = NA_HEAD_DIM
    return pl.pallas_call(
        _full_attn_kernel,
        grid=(h_,),
        in_specs=[pl.BlockSpec((c, hd), lambda h: (b0, h)),
                  pl.BlockSpec((c, hd), lambda h: (b0, h_ + h)),
                  pl.BlockSpec((c, hd), lambda h: (b0, 2 * h_ + h))],
        out_specs=pl.BlockSpec((c, hd), lambda h: (0, h)),
        out_shape=jax.ShapeDtypeStruct((c, NA_WIDTH), BF16),
        compiler_params=_params(("arbitrary",), 16 << 20),
        name="na_context",
    )(u, u, u)


def _dft_cs(n):
    idx = np.arange(n)
    ang = 2.0 * np.pi * ((idx[:, None] * idx[None, :]) % n) / n
    return np.cos(ang), np.sin(ang)


FN_ROWS = 8


def _fn1_kernel(w_ref, x_ref, tr_ref, ti_ref, z_ref, *, r):
    for s in range(FN_ROWS):
        y = _dot(w_ref[...], x_ref[:, s, :].astype(BF16))
        yr, yi = y[:r], y[r:]
        tr, ti = tr_ref[s], ti_ref[s]
        z_ref[pl.ds(0, r), s, :] = yr * tr - yi * ti
        z_ref[pl.ds(r, r), s, :] = yr * ti + yi * tr


def _fn2_kernel(zr_ref, zi_ref, kc_ref, ks_ref, c_ref, s_ref, w_ref, o_ref, *, scale):
    zr, zi = zr_ref[...].astype(BF16), zi_ref[...].astype(BF16)
    kc, ks = kc_ref[...], ks_ref[...]
    xr = (_dot(kc, zr) + _dot(ks, zi)).astype(BF16)
    xi = (_dot(kc, zi) - _dot(ks, zr)).astype(BF16)
    cw = GRID_W
    for g in range(FN_GROUPS):
        lo = g * FN_GROUP_DIM
        f = _dot(xr[:, lo:lo + FN_GROUP_DIM], c_ref[...]) + _dot(xi[:, lo:lo + FN_GROUP_DIM], s_ref[...])
        y = _dot((f * scale).astype(BF16), w_ref[g].astype(BF16))
        for kk in range(FN_ROWS):
            o_ref[:, kk, lo:lo + FN_GROUP_DIM] = y[kk * cw:(kk + 1) * cw]


def fourier_latent(u, l, w_fn):
    cw = GRID_W
    r = l // cw
    c_r, s_r = _dft_cs(r)
    w1 = jnp.asarray(np.concatenate([c_r, -s_r], axis=0), BF16)
    ang = 2.0 * np.pi * (np.arange(cw)[:, None] * np.arange(r)[None, :]) / l
    tr = jnp.asarray(np.cos(ang)[:, :, None], F32)
    ti = jnp.asarray(-np.sin(ang)[:, :, None], F32)
    z = pl.pallas_call(
        functools.partial(_fn1_kernel, r=r),
        grid=(cw // FN_ROWS,),
        in_specs=[pl.BlockSpec((2 * r, r), lambda j: (0, 0)),
                  pl.BlockSpec((r, FN_ROWS, FN_WIDTH), lambda j: (0, j, 0)),
                  pl.BlockSpec((FN_ROWS, r, 1), lambda j: (j, 0, 0)),
                  pl.BlockSpec((FN_ROWS, r, 1), lambda j: (j, 0, 0))],
        out_specs=pl.BlockSpec((2 * r, FN_ROWS, FN_WIDTH), lambda j: (0, j, 0)),
        out_shape=jax.ShapeDtypeStruct((2 * r, cw, FN_WIDTH), F32),
        compiler_params=_params(("arbitrary",), 40 << 20),
        name="fourier_stage1",
    )(w1, u.reshape(u.shape[0] // cw, cw, FN_WIDTH), tr, ti)
    z2 = z.reshape(2 * r * cw, FN_WIDTH)
    c_w, s_w = _dft_cs(cw)
    eye = np.eye(FN_ROWS)
    kc = jnp.asarray(np.kron(eye, c_w), BF16)
    ks = jnp.asarray(np.kron(eye, s_w), BF16)
    c_c, s_c = _dft_cs(FN_GROUP_DIM)
    nb = r // FN_ROWS
    tb = FN_ROWS * cw
    out = pl.pallas_call(
        functools.partial(_fn2_kernel, scale=float((l * FN_GROUP_DIM) ** -0.5)),
        grid=(nb,),
        in_specs=[pl.BlockSpec((tb, FN_WIDTH), lambda b: (b, 0)),
                  pl.BlockSpec((tb, FN_WIDTH), lambda b: (nb + b, 0)),
                  pl.BlockSpec((tb, tb), lambda b: (0, 0)),
                  pl.BlockSpec((tb, tb), lambda b: (0, 0)),
                  pl.BlockSpec((FN_GROUP_DIM, FN_GROUP_DIM), lambda b: (0, 0)),
                  pl.BlockSpec((FN_GROUP_DIM, FN_GROUP_DIM), lambda b: (0, 0)),
                  pl.BlockSpec((FN_GROUPS, FN_GROUP_DIM, FN_GROUP_DIM), lambda b: (0, 0, 0))],
        out_specs=pl.BlockSpec((cw, FN_ROWS, FN_WIDTH), lambda b: (0, b, 0)),
        out_shape=jax.ShapeDtypeStruct((cw, r, FN_WIDTH), F32),
        compiler_params=_params(("arbitrary",), 32 << 20),
        name="fourier_stage2",
    )(z2, z2, kc, ks, jnp.asarray(c_c, BF16), jnp.asarray(s_c, BF16), w_fn)
    return out.reshape(l, FN_WIDTH)


def _fn_ctx_kernel(u_ref, cl_ref, sl_ref, c_ref, s_ref, w_ref, o_ref, *, scale):
    u = u_ref[...].astype(BF16)
    gr = _dot(cl_ref[...], u).astype(BF16)
    gi = (-_dot(sl_ref[...], u)).astype(BF16)
    for g in range(FN_GROUPS):
        lo = g * FN_GROUP_DIM
        f = _dot(gr[:, lo:lo + FN_GROUP_DIM], c_ref[...]) + _dot(gi[:, lo:lo + FN_GROUP_DIM], s_ref[...])
        o_ref[:, lo:lo + FN_GROUP_DIM] = _dot((f * scale).astype(BF16), w_ref[g].astype(BF16)).astype(o_ref.dtype)


def fourier_context(u, s, w_fn):
    c = u.shape[0] - s
    c_l, s_l = _dft_cs(c)
    c_c, s_c = _dft_cs(FN_GROUP_DIM)
    gd = FN_GROUP_DIM
    whole = lambda shape: pl.BlockSpec(shape, lambda i: (0,) * len(shape))
    return pl.pallas_call(
        functools.partial(_fn_ctx_kernel, scale=float((c * FN_GROUP_DIM) ** -0.5)),
        grid=(1,),
        in_specs=[pl.BlockSpec((c, FN_WIDTH), lambda i: (s // c, 0)), whole((c, c)), whole((c, c)),
                  whole((gd, gd)), whole((gd, gd)), whole((FN_GROUPS, gd, gd))],
        out_specs=whole((c, FN_WIDTH)),
        out_shape=jax.ShapeDtypeStruct((c, FN_WIDTH), BF16),
        compiler_params=_params(("arbitrary",), 16 << 20),
        name="fourier_context",
    )(u, jnp.asarray(c_l, BF16), jnp.asarray(s_l, BF16), jnp.asarray(c_c, BF16), jnp.asarray(s_c, BF16), w_fn)


def _split_hi_lo(x):
    hi = x.astype(BF16)
    return hi, (x - hi.astype(F32)).astype(BF16)


def _gla_chunk(q_ref, k_ref, v_ref, lr_ref, cos_ref, sin_ref, w2_ref, bg_ref, perm_ref, st_ref, o_ref, rev):
    cs = GLA_CHUNK
    sub = GLA_SUB
    nsub = cs // sub
    z = _dot(lr_ref[...].astype(BF16), w2_ref[...].astype(BF16)) + bg_ref[...]
    la = -(jnp.maximum(-z, 0.0) + jnp.log(1.0 + jnp.exp(-jnp.abs(z)))) * (1.0 / GLA_GATE_TEMP)
    ri = lax.broadcasted_iota(jnp.int32, (cs, cs), 0)
    ci = lax.broadcasted_iota(jnp.int32, (cs, cs), 1)
    tri = jnp.where((ci >= ri) if rev else (ci <= ri), 1.0, 0.0).astype(BF16)
    la_hi, la_lo = _split_hi_lo(la)
    bcum = _dot(tri, la_hi) + _dot(tri, la_lo)
    edge = 0 if rev else cs - 1
    row_id = lax.broadcasted_iota(jnp.int32, (cs, GLA_DKP), 0)
    sub_r = lax.broadcasted_iota(jnp.int32, (sub, GLA_DKP), 0)
    lane_c = lax.broadcasted_iota(jnp.int32, (sub, cs), 1)
    cos, sin = cos_ref[...], sin_ref[...]
    perm = perm_ref[...]
    qscale = GLA_DK ** -0.5
    for h in range(GLA_HEADS):
        ks_ = slice(h * GLA_DKP, (h + 1) * GLA_DKP)
        vs_ = slice(h * GLA_DV, (h + 1) * GLA_DV)
        qb, kb = q_ref[:, ks_], k_ref[:, ks_]
        q = (qb.astype(F32) * cos + _dot(qb, perm) * sin) * qscale
        k = kb.astype(F32) * cos + _dot(kb, perm) * sin
        v = v_ref[:, vs_]
        b = bcum[:, ks_]
        b_edge = b[edge:edge + 1, :]
        st = st_ref[h]
        o = _dot_nt((q * jnp.exp(b)).astype(BF16), st.astype(BF16))
        slabs = []
        for blk in range(nsub):
            lo = blk * sub
            q_i = q[lo:lo + sub]
            b_i = b[lo:lo + sub]
            k_i = k[lo:lo + sub]
            acc = jnp.zeros((sub, cs), F32)
            if rev and blk < nsub - 1:
                ref_row = b[lo + sub:lo + sub + 1, :]
                outside = row_id >= lo + sub
            elif (not rev) and blk > 0:
                ref_row = b[lo - 1:lo, :]
                outside = row_id < lo
            else:
                ref_row = None
            if ref_row is not None:
                qe = q_i * jnp.exp(b_i - ref_row)
                ke = jnp.where(outside, k * jnp.exp(jnp.where(outside, ref_row - b, 0.0)), 0.0)
                acc = acc + _dot_nt(qe.astype(BF16), ke.astype(BF16))
            for j in range(sub):
                keep = (sub_r <= j) if rev else (sub_r >= j)
                d = jnp.exp(jnp.where(keep, b_i - b_i[j:j + 1, :], NEG))
                col = jnp.sum(q_i * d * k_i[j:j + 1, :], axis=-1, keepdims=True)
                acc = acc + jnp.where(lane_c == lo + j, col, 0.0)
            slabs.append(acc)
        attn = jnp.concatenate(slabs, axis=0)
        o = o + _dot(attn.astype(BF16), v)
        o_ref[:, vs_] = o
        kend = (k * jnp.exp(b_edge - b)).astype(BF16)
        st_ref[h] = st * jnp.exp(b_edge) + _dot_tn(v, kend)


def _gla_kernel(*refs, nchunks):
    fwd, bwd = refs[0:6], refs[6:12]
    w2f_ref, bgf_ref, w2b_ref, bgb_ref, perm_ref, s0_ref = refs[12:18]
    of_ref, ob_ref, sfin_ref, st_ref = refs[18:22]
    step = pl.program_id(0)

    @pl.when(step == 0)
    def _():
        st_ref[...] = s0_ref[...]

    _gla_chunk(*fwd, w2f_ref, bgf_ref, perm_ref, st_ref.at[0], of_ref, False)
    _gla_chunk(*bwd, w2b_ref, bgb_ref, perm_ref, st_ref.at[1], ob_ref, True)

    @pl.when(step == nchunks - 1)
    def _():
        sfin_ref[...] = st_ref[...]


def gla_scan(u_qk, u_vr, lr, cos, sin, w2p, bgp, perm, s0, row0, l):
    n = l // GLA_CHUNK
    c0 = row0 // GLA_CHUNK
    cs = GLA_CHUNK
    kw = GLA_HEADS * GLA_DKP
    rank2 = lr.shape[1]
    st_shape = (2, GLA_HEADS, GLA_DV, GLA_DKP)
    ins = []
    for ch in (lambda s: c0 + s, lambda s: c0 + n - 1 - s):
        ins += [pl.BlockSpec((cs, kw), lambda s, ch=ch: (ch(s), 0)),
                pl.BlockSpec((cs, kw), lambda s, ch=ch: (ch(s), 1)),
                pl.BlockSpec((cs, GLA_VW), lambda s, ch=ch: (ch(s), 0)),
                pl.BlockSpec((cs, rank2), lambda s, ch=ch: (ch(s), 0)),
                pl.BlockSpec((cs, GLA_DKP), lambda s, ch=ch: (ch(s), 0)),
                pl.BlockSpec((cs, GLA_DKP), lambda s, ch=ch: (ch(s), 0))]
    for dr in (0, 1):
        ins += [pl.BlockSpec((None, rank2, kw), lambda s, dr=dr: (dr, 0, 0)),
                pl.BlockSpec((None, 1, kw), lambda s, dr=dr: (dr, 0, 0))]
    ins += [pl.BlockSpec((GLA_DKP, GLA_DKP), lambda s: (0, 0)),
            pl.BlockSpec(st_shape, lambda s: (0, 0, 0, 0))]
    seq = (u_qk, u_qk, u_vr, lr, cos, sin)
    return pl.pallas_call(
        functools.partial(_gla_kernel, nchunks=n),
        grid=(n,),
        in_specs=ins,
        out_specs=[pl.BlockSpec((cs, GLA_VW), lambda s: (s, 0)),
                   pl.BlockSpec((cs, GLA_VW), lambda s: (n - 1 - s, 0)),
                   pl.BlockSpec(st_shape, lambda s: (0, 0, 0, 0))],
        out_shape=[jax.ShapeDtypeStruct((l, GLA_VW), F32), jax.ShapeDtypeStruct((l, GLA_VW), F32),
                   jax.ShapeDtypeStruct(st_shape, F32)],
        scratch_shapes=[pltpu.VMEM(st_shape, F32)],
        compiler_params=_params(("arbitrary",), 40 << 20),
        name="gla_scan",
    )(*seq, *seq, w2p, bgp, w2p, bgp, perm, s0)


def _gla_out_kernel(of_ref, ob_ref, r_ref, g_ref, o_ref):
    o = of_ref[...] + ob_ref[...]
    g = g_ref[...]
    r = r_ref[...].astype(F32)
    for h in range(GLA_HEADS):
        sl = slice(h * GLA_DV, (h + 1) * GLA_DV)
        oh = o[:, sl]
        on = oh * lax.rsqrt(jnp.mean(oh * oh, axis=-1, keepdims=True) + EPS) * g[:, sl]
        o_ref[:, sl] = (on * _silu(r[:, sl])).astype(o_ref.dtype)


def gla_output(o_f, o_b, u_vr, row0, g_gla, tm=256):
    l = o_f.shape[0]
    tm = min(tm, l)
    b0 = row0 // tm
    return pl.pallas_call(
        _gla_out_kernel,
        grid=(l // tm,),
        in_specs=[pl.BlockSpec((tm, GLA_VW), lambda i: (i, 0)),
                  pl.BlockSpec((tm, GLA_VW), lambda i: (i, 0)),
                  pl.BlockSpec((tm, GLA_VW), lambda i: (b0 + i, 1)),
                  pl.BlockSpec((1, GLA_VW), lambda i: (0, 0))],
        out_specs=pl.BlockSpec((tm, GLA_VW), lambda i: (i, 0)),
        out_shape=jax.ShapeDtypeStruct((l, GLA_VW), BF16),
        compiler_params=_params(("arbitrary",), 24 << 20),
        name="gla_output",
    )(o_f, o_b, u_vr, g_gla.reshape(1, GLA_VW))


def rope_tables(n_tokens):
    seg = GLA_DK // 2
    half = seg // 2
    inv = ROPE_THETA ** (-jnp.arange(half, dtype=F32) / half)
    pos = jnp.arange(n_tokens)
    ang_r = (pos // GRID_W).astype(F32)[:, None] * inv
    ang_c = (pos % GRID_W).astype(F32)[:, None] * inv
    pad1 = jnp.ones((n_tokens, GLA_DKP - GLA_DK), F32)
    pad0 = jnp.zeros((n_tokens, GLA_DKP - GLA_DK), F32)
    cos = jnp.concatenate([jnp.cos(ang_r)] * 2 + [jnp.cos(ang_c)] * 2 + [pad1], axis=1)
    sin = jnp.concatenate([jnp.sin(ang_r)] * 2 + [jnp.sin(ang_c)] * 2 + [pad0], axis=1)
    return cos, sin


def rope_perm():
    seg = GLA_DK // 2
    half = seg // 2
    p = np.zeros((GLA_DKP, GLA_DKP), np.float32)
    for s0 in (0, seg):
        for j in range(half):
            p[s0 + half + j, s0 + j] = -1.0
            p[s0 + j, s0 + half + j] = 1.0
    return jnp.asarray(p, BF16)


def _pad_heads(w):
    lead = w.shape[:-1]
    w = w.reshape(lead + (GLA_HEADS, GLA_DK))
    w = jnp.pad(w, [(0, 0)] * len(lead) + [(0, 0), (0, GLA_DKP - GLA_DK)])
    return w.reshape(lead + (GLA_HEADS * GLA_DKP,))


MOE_TR = 256
MOE_GATHER_DEPTH = 4


def moe_plan(route):
    t = route.shape[0]
    tr = MOE_TR
    n_a = 2 * t
    e = route[:, :2].astype(jnp.int32).reshape(-1)
    iota = jnp.arange(n_a, dtype=jnp.int32)
    e_sorted, order = lax.sort((e, iota), num_keys=1)
    counts = jnp.sum((e[:, None] == jnp.arange(MOE_EXPERTS, dtype=jnp.int32)[None]).astype(jnp.int32), axis=0)
    padded = ((counts + tr - 1) // tr) * tr
    ends = jnp.cumsum(padded)
    starts = ends - padded
    shift = starts - (jnp.cumsum(counts) - counts)
    _, pos = lax.sort((order, iota + shift[e_sorted]), num_keys=1)
    npad = n_a + MOE_EXPERTS * tr
    ntiles = npad // tr
    tile_e = jnp.minimum(jnp.searchsorted(ends, jnp.arange(ntiles, dtype=jnp.int32) * tr, side="right"),
                         MOE_EXPERTS - 1).astype(jnp.int32)
    row = jnp.arange(npad, dtype=jnp.int32)
    per_row = lambda v: jnp.repeat(v[tile_e], tr)
    valid = (row - per_row(starts)) < per_row(counts)
    src = jnp.where(valid, order[jnp.clip(row - per_row(shift), 0, n_a - 1)] // 2, 0)
    n_used = (ends[-1] // tr).reshape(1).astype(jnp.int32)
    return src.astype(jnp.int32), tile_e, n_used, pos.astype(jnp.int32)


ROW_GROUP = 8


def _row_gather_start(idx_ref, first, src_hbm, dst, sem):
    def body(g, carry):
        for u in range(ROW_GROUP):
            pltpu.make_async_copy(src_hbm.at[pl.ds(idx_ref[first + g * ROW_GROUP + u], 1)],
                                  dst.at[g, pl.ds(u, 1)], sem).start()
        return carry
    lax.fori_loop(0, dst.shape[0], body, 0)


def _row_gather_wait(dst, sem):
    pltpu.make_async_copy(dst, dst, sem).wait()


def _rows(x):
    return x.reshape(x.shape[0] * ROW_GROUP, x.shape[2])


def _moe_ffn_kernel(src_ref, te_ref, nu_ref, x_hbm, wgu_ref, wdn_ref, o_ref, buf, sem):
    i = pl.program_id(0)
    n_used = nu_ref[0]
    tr = MOE_TR
    depth = buf.shape[0]
    slot = lax.rem(i, depth)

    @pl.when(i == 0)
    def _():
        for ahead in range(depth - 1):
            @pl.when(ahead < n_used)
            def _(ahead=ahead):
                _row_gather_start(src_ref, ahead * tr, x_hbm, buf.at[ahead], sem.at[ahead])

    @pl.when(i < n_used)
    def _():
        nxt = i + depth - 1

        @pl.when(nxt < n_used)
        def _():
            s_ = lax.rem(nxt, depth)
            _row_gather_start(src_ref, nxt * tr, x_hbm, buf.at[s_], sem.at[s_])

        _row_gather_wait(buf.at[slot], sem.at[slot])
        x_lo, x_hi = _unpack_bf16_pairs(_rows(buf[slot]))
        half = x_lo.shape[1]
        gu = (_dot(x_lo.astype(BF16), wgu_ref[:half, :].astype(BF16))
              + _dot(x_hi.astype(BF16), wgu_ref[half:, :].astype(BF16)))
        hid = _silu(gu[:, :MOE_HIDDEN]) * gu[:, MOE_HIDDEN:]
        o_ref[...] = _pack_bf16_pairs(_dot(hid.astype(BF16), wdn_ref[...].astype(BF16)))

    @pl.when(i >= n_used)
    def _():
        o_ref[...] = jnp.zeros(o_ref.shape, o_ref.dtype)


def moe_ffn(xn, w_gu, w_dn, l, src, tile_e, n_used):
    t = xn.shape[0]
    d = 2 * xn.shape[1]
    tr = MOE_TR
    npad = src.shape[0]
    f = MOE_HIDDEN
    grid_spec = pltpu.PrefetchScalarGridSpec(
        num_scalar_prefetch=3,
        grid=(npad // tr,),
        in_specs=[pl.BlockSpec(memory_space=pl.ANY),
                  pl.BlockSpec((None, None, d, 2 * f), lambda i, s, te, nu: (l, te[i], 0, 0)),
                  pl.BlockSpec((None, None, f, d), lambda i, s, te, nu: (l, te[i], 0, 0))],
        out_specs=pl.BlockSpec((tr, d // 2), lambda i, s, te, nu: (i, 0)),
        scratch_shapes=[pltpu.VMEM((MOE_GATHER_DEPTH, tr // ROW_GROUP, ROW_GROUP, d // 2), jnp.uint32),
                        pltpu.SemaphoreType.DMA((MOE_GATHER_DEPTH,))],
    )
    return pl.pallas_call(
        _moe_ffn_kernel,
        grid_spec=grid_spec,
        out_shape=jax.ShapeDtypeStruct((npad, d // 2), jnp.uint32),
        compiler_params=_params(("arbitrary",), 4 * tr * d * 4 + 2 * 3 * d * f * 4 + 6 * tr * d * 4),
        name="moe_ffn",
    )(src, tile_e, n_used, xn, w_gu, w_dn)


def _moe_combine_kernel(pos_ref, y_hbm, x_ref, route_ref, gate_ref, o_ref, buf, sem):
    i = pl.program_id(0)
    n = pl.num_programs(0)
    tm = x_ref.shape[0]
    depth = buf.shape[0]
    slot = lax.rem(i, depth)

    def start(tile):
        s = lax.rem(tile, depth)
        _row_gather_start(pos_ref, 2 * tm * tile, y_hbm, buf.at[s], sem.at[s])

    @pl.when(i == 0)
    def _():
        for ahead in range(depth - 1):
            @pl.when(ahead < n)
            def _(ahead=ahead):
                start(ahead)

    @pl.when(i + depth - 1 < n)
    def _():
        start(i + depth - 1)

    _row_gather_wait(buf.at[slot], sem.at[slot])
    route = route_ref[...]
    lane = lax.broadcasted_iota(jnp.int32, route.shape, 1)
    w1 = _lane_pick(route, lane, 2)
    w2 = _lane_pick(route, lane, 3)
    cur = buf.at[slot]
    half = cur.shape[2]
    gm = tm // ROW_GROUP
    cw = min(4 * LANE, half)
    for c0 in range(0, half, cw):
        first = _unpack_bf16_pairs(_rows(cur[:gm, :, c0:c0 + cw]))
        second = _unpack_bf16_pairs(_rows(cur[gm:, :, c0:c0 + cw]))
        for part in (0, 1):
            cols = slice(part * half + c0, part * half + c0 + cw)
            o_ref[:, cols] = x_ref[:, cols] + gate_ref[:, cols] * (w1 * first[part] + w2 * second[part])


def moe_combine(y, x, route, gate, pos, row0, tm=128):
    t, d = x.shape
    pos = pos[2 * row0:2 * (row0 + t)].reshape(t // tm, tm, 2).transpose(0, 2, 1).reshape(-1)
    blk0 = row0 // tm
    grid_spec = pltpu.PrefetchScalarGridSpec(
        num_scalar_prefetch=1,
        grid=(t // tm,),
        in_specs=[pl.BlockSpec(memory_space=pl.ANY),
                  pl.BlockSpec((tm, d), lambda i, p: (i, 0)),
                  pl.BlockSpec((tm, LANE), lambda i, p: (i + blk0, 0)),
                  pl.BlockSpec((1, d), lambda i, p: (0, 0))],
        out_specs=pl.BlockSpec((tm, d), lambda i, p: (i, 0)),
        scratch_shapes=[pltpu.VMEM((MOE_GATHER_DEPTH, 2 * tm // ROW_GROUP, ROW_GROUP, d // 2), jnp.uint32),
                        pltpu.SemaphoreType.DMA((MOE_GATHER_DEPTH,))],
    )
    return pl.pallas_call(
        _moe_combine_kernel,
        grid_spec=grid_spec,
        out_shape=jax.ShapeDtypeStruct((t, d), F32),
        compiler_params=_params(("arbitrary",), 4 * tm * d * 4 + 8 * tm * d * 4),
        name="moe_combine",
    )(pos, y, x, route, gate.reshape(1, d))


def _layer(l, x, xc, mod, last, consts, w):
    (g_mix, w_in, rpb, w_fn, w_g2, b_g, g_gla, w_out, g_ffn, w_rg, b_rg, w_re, b_re, w_gu, w_dn) = w
    cos, sin, perm, na_tab = consts
    d = D_MODEL
    s_len, c_len = x.shape[0], xc.shape[0]
    m_x = mod[l, 0].reshape(N_MOD, d)
    m_c = mod[l, 1].reshape(N_MOD, d)

    xn, lr = norm_proj([(x, m_x[0], m_x[1]), (xc, m_c[0], m_c[1])], g_mix[l], w_in, l, OFF_LR, 2 * GLA_GATE_RANK)
    tm = (s_len + c_len) // 8
    u_na = matmul(xn, w_in, (l,), OFF_NA, 3 * NA_WIDTH, 512, tm, BF16, name="in_na")
    u_fn = matmul(xn, w_in, (l,), OFF_FN, FN_WIDTH, 512, tm, F32, name="in_fn")
    u_qk = gla_qk_proj(xn, w_in, l, tm)
    u_vr = matmul(xn, w_in, (l,), OFF_GVR, 2 * GLA_VW, 512, tm, BF16, name="in_gvr")

    r_ = GLA_GATE_RANK
    w2p = jnp.stack([jnp.pad(_pad_heads(w_g2[l, dr]), ((dr * r_, (1 - dr) * r_), (0, 0))) for dr in (0, 1)])
    bgp = _pad_heads(b_g[l]).reshape(2, 1, -1)
    zero = jnp.zeros((2, GLA_HEADS, GLA_DV, GLA_DKP), F32)
    *outs_c, s_c = gla_scan(u_qk, u_vr, lr, cos, sin, w2p, bgp, perm, zero, s_len, c_len)
    *outs_x, _ = gla_scan(u_qk, u_vr, lr, cos, sin, w2p, bgp, perm, s_c, 0, s_len)

    mix = (neighbourhood_attention(u_na, s_len, na_tab, l),
           fourier_latent(u_fn, s_len, w_fn[l]),
           gla_output(outs_x[0], outs_x[1], u_vr, 0, g_gla[l]))
    x = matmul(mix, w_out, (l,), 0, d, 512, 1024, F32, res=x, gate=m_x[2], name="out_proj")

    w_route = jnp.pad(jnp.concatenate([w_re[l], w_rg[l]], axis=1),
                      ((0, 0), (0, LANE - MOE_EXPERTS - MOE_GROUPS)))
    b_route = jnp.pad(jnp.concatenate([b_re[l], b_rg[l]]), (0, LANE - MOE_EXPERTS - MOE_GROUPS)).reshape(1, LANE)
    streams = [(x, m_x)]
    if not last:
        mix_c = (full_attention(u_na, s_len), fourier_context(u_fn, s_len, w_fn[l]),
                 gla_output(outs_c[0], outs_c[1], u_vr, s_len, g_gla[l]))
        xc = matmul(mix_c, w_out, (l,), 0, d, 512, 1024, F32, res=xc, gate=m_c[2], name="out_proj_ctx")
        streams.append((xc, m_c))

    yn, route = norm_route([(y, m[3], m[4]) for y, m in streams], g_ffn[l], w_route, b_route)
    src, tile_e, n_used, pos = moe_plan(route)
    y_exp = moe_ffn(yn, w_gu, w_dn, l, src, tile_e, n_used)
    outs, row0 = [], 0
    for y, m in streams:
        outs.append(moe_combine(y_exp, y, route, m[5], pos, row0))
        row0 += y.shape[0]
    return outs[0], (outs[1] if not last else xc)


def kernel(x, c, ctx, c_ctx, w_ada, b_ada, g_mix, w_in, rpb, w_fn, w_g2, b_g, g_gla, w_out,
           g_ffn, w_rg, b_rg, w_re, b_re, w_gu, w_dn, g_final):
    assert x.shape[0] == 1 and c.shape[0] == 1
    d = D_MODEL
    s_len = x.shape[1]
    c_len = ctx.shape[1]
    cc = jnp.concatenate([c, c_ctx[None], jnp.zeros((6, d), F32)], axis=0)
    mod = ada_mod(cc, w_ada, b_ada)
    cos, sin = rope_tables(s_len)
    cos = jnp.concatenate([cos, jnp.ones((c_len, GLA_DKP), F32)], axis=0)
    sin = jnp.concatenate([sin, jnp.zeros((c_len, GLA_DKP), F32)], axis=0)
    consts = (cos, sin, rope_perm(), na_tables(rpb, s_len // GRID_W))
    w_in = jnp.swapaxes(w_in, 1, 2)
    w = (g_mix, w_in, rpb, w_fn, w_g2, b_g, g_gla, w_out, g_ffn, w_rg, b_rg, w_re, b_re, w_gu, w_dn)
    xs, xc = x[0], ctx[0]
    for l in range(DEPTH):
        xs, xc = _layer(l, xs, xc, mod, l == DEPTH - 1, consts, w)
    return final_norm(xs, g_final)[None]
```

```python
import functools

import numpy as np
import jax
import jax.numpy as jnp
from jax import lax
from jax.experimental import pallas as pl
from jax.experimental.pallas import tpu as pltpu

F32 = jnp.float32
BF16 = jnp.bfloat16

D_MODEL = 4096
DEPTH = 4
GRID_W = 64
EPS = 1e-6
N_MOD = 6

NA_HEADS = 12
NA_HEAD_DIM = 128
NA_WIN_H = 8
NA_WIN_W = 16
NA_WIDTH = NA_HEADS * NA_HEAD_DIM

FN_GROUPS = 8
FN_GROUP_DIM = 128
FN_WIDTH = FN_GROUPS * FN_GROUP_DIM

GLA_HEADS = 4
GLA_DK = 192
GLA_DKP = 256
GLA_DV = 384
GLA_KW = GLA_HEADS * GLA_DK
GLA_VW = GLA_HEADS * GLA_DV
GLA_GATE_RANK = 16
GLA_GATE_TEMP = 16.0
GLA_CHUNK = 64
GLA_SUB = 8
ROPE_THETA = 10000.0

MOE_GROUPS = 4
MOE_EXPERTS_PER_GROUP = 8
MOE_EXPERTS = MOE_GROUPS * MOE_EXPERTS_PER_GROUP
MOE_HIDDEN = 192

OFF_NA = 0
OFF_FN = 3 * NA_WIDTH
OFF_GQK = OFF_FN + FN_WIDTH
OFF_GVR = OFF_GQK + 2 * GLA_KW
OFF_LR = OFF_GVR + 2 * GLA_VW

LANE = 128
NEG = -1e30
V7X_VMEM_BUDGET = 56 * 1024 * 1024


def _params(sem, vmem_bytes):
    return pltpu.CompilerParams(dimension_semantics=sem,
                                vmem_limit_bytes=int(min(max(vmem_bytes, 16 << 20), V7X_VMEM_BUDGET)))


def _dot(a, b):
    return jnp.dot(a, b, preferred_element_type=F32)


def _dot_nt(a, b):
    return lax.dot_general(a, b, (((1,), (1,)), ((), ())), preferred_element_type=F32)


def _dot_tn(a, b):
    return lax.dot_general(a, b, (((0,), (0,)), ((), ())), preferred_element_type=F32)


def _silu(x):
    return x * (1.0 / (1.0 + jnp.exp(-x)))


def _ada_kernel(c_ref, w_ref, b_ref, o_ref):
    s = _silu(c_ref[...]).astype(BF16)
    o_ref[0] = _dot(s, w_ref[0].astype(BF16)) + b_ref[0]


def ada_mod(cc, w_ada, b_ada, tn=512):
    n, d, nd = w_ada.shape
    return pl.pallas_call(
        _ada_kernel,
        grid=(n, nd // tn),
        in_specs=[pl.BlockSpec((8, d), lambda l, j: (0, 0)),
                  pl.BlockSpec((1, d, tn), lambda l, j: (l, 0, j)),
                  pl.BlockSpec((1, 1, tn), lambda l, j: (l, 0, j))],
        out_specs=pl.BlockSpec((1, 8, tn), lambda l, j: (l, 0, j)),
        out_shape=jax.ShapeDtypeStruct((n, 8, nd), F32),
        compiler_params=_params(("arbitrary", "arbitrary"), 3 * d * tn * 4 + (4 << 20)),
        name="ada_mod",
    )(cc, w_ada, b_ada.reshape(n, 1, nd))


def _normed(x_ref, g_ref, sh_ref, sc_ref):
    x = x_ref[...]
    y = x * lax.rsqrt(jnp.mean(x * x, axis=-1, keepdims=True) + EPS)
    return (y * g_ref[...]) * (1.0 + sc_ref[...]) + sh_ref[...]


def _norm_proj_kernel(*refs, blocks, layer, col0):
    ns = len(blocks)
    x_refs, g_ref = refs[:ns], refs[ns]
    mods = refs[ns + 1:ns + 1 + 2 * ns]
    w_hbm, xn_ref, p_ref, w_buf, sem = refs[ns + 1 + 2 * ns:]
    i = pl.program_id(0)

    @pl.when(i == 0)
    def _():
        cp = pltpu.make_async_copy(w_hbm.at[layer, pl.ds(col0, w_buf.shape[0]), :], w_buf, sem)
        cp.start()
        cp.wait()

    first = 0
    for s_, nb in enumerate(blocks):
        @pl.when((i >= first) & (i < first + nb))
        def _(s_=s_):
            hb = _normed(x_refs[s_], g_ref, mods[2 * s_], mods[2 * s_ + 1]).astype(BF16)
            xn_ref[...] = hb
            p_ref[...] = _dot_nt(hb, w_buf[...].astype(BF16))
        first += nb


def _lane_pick(v, lane, idx):
    return jnp.sum(jnp.where(lane == idx, v, 0.0), axis=-1, keepdims=True)


def _pack_bf16_pairs(x):
    xf = x.astype(BF16).astype(F32)
    half = xf.shape[1] // 2
    lo = lax.bitcast_convert_type(xf[:, :half], jnp.uint32)
    hi = lax.bitcast_convert_type(xf[:, half:], jnp.uint32)
    return hi | (lo >> 16)


def _unpack_bf16_pairs(p):
    lo = lax.bitcast_convert_type(p << 16, F32)
    hi = lax.bitcast_convert_type(p & jnp.uint32(0xFFFF0000), F32)
    return lo, hi


def _norm_route_kernel(x_ref, g_ref, sh_ref, sc_ref, w_ref, b_ref, xn_ref, route_ref):
    h = _normed(x_ref, g_ref, sh_ref, sc_ref)
    hb = h.astype(BF16)
    xn_ref[...] = _pack_bf16_pairs(h)
    logits = _dot(hb, w_ref[...].astype(BF16)) + b_ref[...]
    lane = lax.broadcasted_iota(jnp.int32, logits.shape, 1).astype(F32)
    far = float(4 * LANE)
    is_g = (lane >= MOE_EXPERTS) & (lane < MOE_EXPERTS + MOE_GROUPS)
    gl = jnp.where(is_g, logits, NEG)
    gmax = jnp.max(gl, axis=-1, keepdims=True)
    g_top = jnp.min(jnp.where(gl == gmax, lane, far), axis=-1, keepdims=True) - MOE_EXPERTS
    g_w = 1.0 / jnp.sum(jnp.where(is_g, jnp.exp(gl - gmax), 0.0), axis=-1, keepdims=True)
    in_grp = (lane >= g_top * MOE_EXPERTS_PER_GROUP) & (lane < (g_top + 1) * MOE_EXPERTS_PER_GROUP)
    el = jnp.where(in_grp, logits, NEG)
    v1 = jnp.max(el, axis=-1, keepdims=True)
    i1 = jnp.min(jnp.where(el == v1, lane, far), axis=-1, keepdims=True)
    el2 = jnp.where(lane == i1, NEG, el)
    v2 = jnp.max(el2, axis=-1, keepdims=True)
    i2 = jnp.min(jnp.where(el2 == v2, lane, far), axis=-1, keepdims=True)
    e2 = jnp.exp(v2 - v1)
    w1 = g_w / (1.0 + e2)
    w2 = g_w * e2 / (1.0 + e2)
    route_ref[...] = (jnp.where(lane == 0.0, i1, 0.0) + jnp.where(lane == 1.0, i2, 0.0)
                      + jnp.where(lane == 2.0, w1, 0.0) + jnp.where(lane == 3.0, w2, 0.0))


def _norm_route_streams_kernel(*refs, blocks):
    ns = len(blocks)
    x_refs, g_ref = refs[:ns], refs[ns]
    mods = refs[ns + 1:ns + 1 + 2 * ns]
    w_ref, b_ref, xn_ref, route_ref = refs[ns + 1 + 2 * ns:]
    i = pl.program_id(0)
    first = 0
    for s_, nb in enumerate(blocks):
        @pl.when((i >= first) & (i < first + nb))
        def _(s_=s_):
            _norm_route_kernel(x_refs[s_], g_ref, mods[2 * s_], mods[2 * s_ + 1], w_ref, b_ref, xn_ref, route_ref)
        first += nb


def _stream_specs(streams, tm, d):
    blocks = tuple(x.shape[0] // tm for x, _, _ in streams)
    ins, args, first = [], [], 0
    for (x, _, _), nb in zip(streams, blocks):
        ins.append(pl.BlockSpec((tm, d), lambda i, first=first, nb=nb: (jnp.clip(i - first, 0, nb - 1), 0)))
        args.append(x)
        first += nb
    return blocks, ins, args


def norm_proj(streams, g, w, layer, col0, ncols, tm=256):
    d = streams[0][0].shape[1]
    blocks, ins, args = _stream_specs(streams, tm, d)
    vec = pl.BlockSpec((1, d), lambda i: (0, 0))
    ins.append(vec)
    args.append(g.reshape(1, d))
    for _, sh, sc in streams:
        ins += [vec, vec]
        args += [sh.reshape(1, d), sc.reshape(1, d)]
    ins.append(pl.BlockSpec(memory_space=pl.ANY))
    args.append(w)
    rows = sum(blocks) * tm
    return pl.pallas_call(
        functools.partial(_norm_proj_kernel, blocks=blocks, layer=layer, col0=col0),
        grid=(sum(blocks),),
        in_specs=ins,
        out_specs=[pl.BlockSpec((tm, d), lambda i: (i, 0)), pl.BlockSpec((tm, ncols), lambda i: (i, 0))],
        out_shape=[jax.ShapeDtypeStruct((rows, d), BF16), jax.ShapeDtypeStruct((rows, ncols), F32)],
        scratch_shapes=[pltpu.VMEM((ncols, d), F32), pltpu.SemaphoreType.DMA(())],
        compiler_params=_params(("arbitrary",), (4 + 4 * len(streams)) * tm * d * 4 + 2 * d * LANE * 4),
        name="norm_proj",
    )(*args)


def norm_route(streams, g, w_small, b_small, tm=256):
    d = streams[0][0].shape[1]
    blocks, ins, args = _stream_specs(streams, tm, d)
    row = lambda i: (0, 0)
    vec = pl.BlockSpec((1, d), row)
    ins.append(vec)
    args.append(g.reshape(1, d))
    for _, sh, sc in streams:
        ins += [vec, vec]
        args += [sh.reshape(1, d), sc.reshape(1, d)]
    ins += [pl.BlockSpec((d, LANE), row), pl.BlockSpec((1, LANE), row)]
    args += [w_small, b_small]
    rows = sum(blocks) * tm
    return pl.pallas_call(
        functools.partial(_norm_route_streams_kernel, blocks=blocks),
        grid=(sum(blocks),),
        in_specs=ins,
        out_specs=[pl.BlockSpec((tm, d // 2), lambda i: (i, 0)), pl.BlockSpec((tm, LANE), lambda i: (i, 0))],
        out_shape=[jax.ShapeDtypeStruct((rows, d // 2), jnp.uint32), jax.ShapeDtypeStruct((rows, LANE), F32)],
        compiler_params=_params(("arbitrary",), (4 + 4 * len(streams)) * tm * d * 4 + 4 * d * LANE * 4),
        name="norm_route",
    )(*args)


def _final_norm_kernel(x_ref, g_ref, o_ref):
    x = x_ref[...]
    o_ref[...] = x * lax.rsqrt(jnp.mean(x * x, axis=-1, keepdims=True) + EPS) * g_ref[...]


def final_norm(x, g, tm=256):
    t, d = x.shape
    return pl.pallas_call(
        _final_norm_kernel,
        grid=(t // tm,),
        in_specs=[pl.BlockSpec((tm, d), lambda i: (i, 0)), pl.BlockSpec((1, d), lambda i: (0, 0))],
        out_specs=pl.BlockSpec((tm, d), lambda i: (i, 0)),
        out_shape=jax.ShapeDtypeStruct((t, d), F32),
        compiler_params=_params(("arbitrary",), 6 * tm * d * 4),
        name="final_norm",
    )(x, g.reshape(1, d))


def _mm_wt_kernel(a_ref, wt_ref, o_ref):
    o_ref[...] = _dot_nt(a_ref[...], wt_ref[...].astype(BF16)).astype(o_ref.dtype)


def _mm_parts_res_kernel(*refs, widths):
    n = len(widths)
    w_ref, r_ref, g_ref, o_ref = refs[n:]
    acc, k0 = None, 0
    for a_ref, wd in zip(refs[:n], widths):
        part = _dot(a_ref[...].astype(BF16), w_ref[k0:k0 + wd, :].astype(BF16))
        acc = part if acc is None else acc + part
        k0 += wd
    o_ref[...] = r_ref[...] + g_ref[...] * acc


def _mm_head_pad_kernel(a_ref, w0_ref, w1_ref, w2_ref, o_ref):
    a = a_ref[...]
    acc = jnp.concatenate([_dot_nt(a, w_ref[...].astype(BF16)) for w_ref in (w0_ref, w1_ref, w2_ref)], axis=1)
    pad = jnp.zeros((a.shape[0], GLA_DKP - GLA_DK), o_ref.dtype)
    for h in range(GLA_HEADS):
        o_ref[:, h * GLA_DKP:h * GLA_DKP + GLA_DK] = acc[:, h * GLA_DK:(h + 1) * GLA_DK].astype(o_ref.dtype)
        o_ref[:, h * GLA_DKP + GLA_DK:(h + 1) * GLA_DKP] = pad


def gla_qk_proj(a, w_in_t, l, tm=1024):
    t, k = a.shape
    tm = min(tm, t)
    tn = GLA_KW // 3
    j0 = OFF_GQK // tn
    w_spec = lambda p: pl.BlockSpec((None, tn, k), lambda i, j: (l, j0 + 3 * j + p, 0))
    kwp = GLA_HEADS * GLA_DKP
    return pl.pallas_call(
        _mm_head_pad_kernel,
        grid=(t // tm, 2),
        in_specs=[pl.BlockSpec((tm, k), lambda i, j: (i, 0)), w_spec(0), w_spec(1), w_spec(2)],
        out_specs=pl.BlockSpec((tm, kwp), lambda i, j: (i, j)),
        out_shape=jax.ShapeDtypeStruct((t, 2 * kwp), BF16),
        compiler_params=_params(("arbitrary", "arbitrary"), 2 * tm * k * 2 + 7 * k * tn * 4 + 8 * tm * kwp * 4),
        name="in_gqk",
    )(a, w_in_t, w_in_t, w_in_t)


def matmul(a, w, w_lead, col0, ncols, tn, tm, out_dtype, res=None, gate=None, name="mm"):
    parts = a if isinstance(a, (tuple, list)) else (a,)
    t = parts[0].shape[0]
    k = sum(p.shape[1] for p in parts)
    tm = min(tm, t)
    nlead = len(w_lead)
    j0 = col0 // tn
    if res is None:
        w_spec = pl.BlockSpec((None,) * nlead + (tn, k), lambda i, j: tuple(w_lead) + (j + j0, 0))
    else:
        w_spec = pl.BlockSpec((None,) * nlead + (k, tn), lambda i, j: tuple(w_lead) + (0, j + j0))
    ins = [pl.BlockSpec((tm, p.shape[1]), lambda i, j: (i, 0)) for p in parts] + [w_spec]
    args = list(parts) + [w]
    wbytes = jnp.dtype(w.dtype).itemsize
    vmem = 2 * tm * k * 2 + 3 * k * tn * wbytes + 6 * tm * tn * 4
    if res is None:
        body = _mm_wt_kernel
    else:
        body = functools.partial(_mm_parts_res_kernel, widths=tuple(p.shape[1] for p in parts))
        ins += [pl.BlockSpec((tm, tn), lambda i, j: (i, j)), pl.BlockSpec((1, tn), lambda i, j: (0, j))]
        args += [res, gate.reshape(1, ncols)]
    return pl.pallas_call(
        body,
        grid=(t // tm, ncols // tn),
        in_specs=ins,
        out_specs=pl.BlockSpec((tm, tn), lambda i, j: (i, j)),
        out_shape=jax.ShapeDtypeStruct((t, ncols), out_dtype),
        compiler_params=_params(("arbitrary", "arbitrary"), vmem),
        name=name,
    )(*args)


NA_QROWS = 4
NA_SLAB = 12
NA_HEADS_PER_STEP = 4


def na_tables(rpb, rows):
    nblk = rows // NA_QROWS
    n_, h_ = rpb.shape[:2]
    qc = np.arange(GRID_W)[:, None]
    kc = np.arange(GRID_W)[None, :]
    c0 = np.clip(qc - NA_WIN_W // 2, 0, GRID_W - NA_WIN_W)
    ok_c = (kc >= c0) & (kc < c0 + NA_WIN_W)
    a = np.arange(NA_QROWS)[:, None]
    b = np.arange(NA_SLAB)[None, :]
    dr_l, ok_l = [], []
    for i in (0, 1, nblk - 1):
        base = int(np.clip(i * NA_QROWS - NA_WIN_H // 2, 0, rows - NA_SLAB))
        r = i * NA_QROWS + a
        r0 = np.clip(r - NA_WIN_H // 2, 0, rows - NA_WIN_H)
        krow = base + b
        ok_l.append((krow >= r0) & (krow < r0 + NA_WIN_H))
        dr_l.append(np.clip(krow - r + NA_WIN_H - 1, 0, 2 * NA_WIN_H - 2))
    dr = np.stack(dr_l)
    ok = np.stack(ok_l)[:, :, None, :, None] & ok_c[None, None, :, None, :]
    ok = jnp.asarray(ok.reshape(3, NA_QROWS * GRID_W, NA_SLAB * GRID_W))
    lo = GRID_W - NA_WIN_W
    padded = jnp.pad(rpb.astype(F32), ((0, 0), (0, 0), (0, 0), (lo, lo)))
    by_col = jnp.stack([padded[..., GRID_W - 1 - q:2 * GRID_W - 1 - q] for q in range(GRID_W)], axis=3)
    variants = []
    for v in range(3):
        rows_ = [jnp.concatenate([by_col[:, :, int(dr[v, a_, b_])] for b_ in range(NA_SLAB)], axis=-1)
                 for a_ in range(NA_QROWS)]
        variants.append(jnp.concatenate(rows_, axis=2))
    bias = jnp.stack(variants, axis=1)
    return jnp.where(ok[None, :, None], bias, NEG)


def _na_kernel(q_ref, k_ref, v_ref, kc_ref, vc_ref, t_ref, o_ref, *, rows):
    i = pl.program_id(1)
    base = jnp.clip(i * NA_QROWS - NA_WIN_H // 2, 0, rows - NA_SLAB) * GRID_W
    base = pl.multiple_of(base, NA_QROWS * GRID_W)
    n_keys = NA_SLAB * GRID_W
    scale = NA_HEAD_DIM ** -0.5
    for hh in range(NA_HEADS_PER_STEP):
        cols = slice(hh * NA_HEAD_DIM, (hh + 1) * NA_HEAD_DIM)
        q = q_ref[:, cols]
        k = k_ref[pl.ds(base, n_keys), cols]
        v = v_ref[pl.ds(base, n_keys), cols]
        s = _dot_nt(q, k) * scale + t_ref[hh]
        sc = _dot_nt(q, kc_ref[:, cols]) * scale
        m = jnp.maximum(jnp.max(s, axis=-1, keepdims=True), jnp.max(sc, axis=-1, keepdims=True))
        p = jnp.exp(s - m)
        pc = jnp.exp(sc - m)
        den = jnp.sum(p, axis=-1, keepdims=True) + jnp.sum(pc, axis=-1, keepdims=True)
        o = _dot(p.astype(BF16), v) + _dot(pc.astype(BF16), vc_ref[:, cols])
        o_ref[:, cols] = (o / den).astype(o_ref.dtype)


def neighbourhood_attention(u, s, tables, l):
    c = u.shape[0] - s
    rows = s // GRID_W
    nblk = rows // NA_QROWS
    tq = NA_QROWS * GRID_W
    hps = NA_HEADS_PER_STEP
    ng = NA_HEADS // hps
    hw = hps * NA_HEAD_DIM

    def variant(i):
        return jnp.where(i == 0, 0, jnp.where(i == nblk - 1, 2, 1))

    return pl.pallas_call(
        functools.partial(_na_kernel, rows=rows),
        grid=(ng, nblk),
        in_specs=[pl.BlockSpec((tq, hw), lambda h, i: (i, h)),
                  pl.BlockSpec((s, hw), lambda h, i: (0, ng + h)),
                  pl.BlockSpec((s, hw), lambda h, i: (0, 2 * ng + h)),
                  pl.BlockSpec((c, hw), lambda h, i: (s // c, ng + h)),
                  pl.BlockSpec((c, hw), lambda h, i: (s // c, 2 * ng + h)),
                  pl.BlockSpec((None, None, hps, tq, NA_SLAB * GRID_W), lambda h, i: (l, variant(i), h, 0, 0))],
        out_specs=pl.BlockSpec((tq, hw), lambda h, i: (i, h)),
        out_shape=jax.ShapeDtypeStruct((s, NA_WIDTH), BF16),
        compiler_params=_params(("arbitrary", "arbitrary"),
                                4 * s * hw * 2 + 2 * hps * tq * NA_SLAB * GRID_W * 4 + (12 << 20)),
        name="na_latent",
    )(u, u, u, u, u, tables)


def _full_attn_kernel(q_ref, k_ref, v_ref, o_ref):
    s = _dot_nt(q_ref[...], k_ref[...]) * NA_HEAD_DIM ** -0.5
    p = jnp.exp(s - jnp.max(s, axis=-1, keepdims=True))
    den = jnp.sum(p, axis=-1, keepdims=True)
    o_ref[...] = (_dot(p.astype(BF16), v_ref[...]) / den).astype(o_ref.dtype)


def full_attention(u, s):
    c = u.shape[0] - s
    b0 = s // c
    h_ = NA_HEADS
    hd = NA_HEAD_DIM
    return pl.pallas_call(
        _full_attn_kernel,
        grid=(h_,),
        in_specs=[pl.BlockSpec((c, hd), lambda h: (b0, h)),
                  pl.BlockSpec((c, hd), lambda h: (b0, h_ + h)),
                  pl.BlockSpec((c, hd), lambda h: (b0, 2 * h_ + h))],
        out_specs=pl.BlockSpec((c, hd), lambda h: (0, h)),
        out_shape=jax.ShapeDtypeStruct((c, NA_WIDTH), BF16),
        compiler_params=_params(("arbitrary",), 16 << 20),
        name="na_context",
    )(u, u, u)


def _dft_cs(n):
    idx = np.arange(n)
    ang = 2.0 * np.pi * ((idx[:, None] * idx[None, :]) % n) / n
    return np.cos(ang), np.sin(ang)


FN_ROWS = 8


def _fn1_kernel(w_ref, x_ref, tr_ref, ti_ref, z_ref, *, r):
    for s in range(FN_ROWS):
        y = _dot(w_ref[...], x_ref[:, s, :].astype(BF16))
        yr, yi = y[:r], y[r:]
        tr, ti = tr_ref[s], ti_ref[s]
        z_ref[pl.ds(0, r), s, :] = yr * tr - yi * ti
        z_ref[pl.ds(r, r), s, :] = yr * ti + yi * tr


def _fn2_kernel(zr_ref, zi_ref, kc_ref, ks_ref, c_ref, s_ref, w_ref, o_ref, *, scale):
    zr, zi = zr_ref[...].astype(BF16), zi_ref[...].astype(BF16)
    kc, ks = kc_ref[...], ks_ref[...]
    xr = (_dot(kc, zr) + _dot(ks, zi)).astype(BF16)
    xi = (_dot(kc, zi) - _dot(ks, zr)).astype(BF16)
    cw = GRID_W
    for g in range(FN_GROUPS):
        lo = g * FN_GROUP_DIM
        f = _dot(xr[:, lo:lo + FN_GROUP_DIM], c_ref[...]) + _dot(xi[:, lo:lo + FN_GROUP_DIM], s_ref[...])
        y = _dot((f * scale).astype(BF16), w_ref[g].astype(BF16))
        for kk in range(FN_ROWS):
            o_ref[:, kk, lo:lo + FN_GROUP_DIM] = y[kk * cw:(kk + 1) * cw]


def fourier_latent(u, l, w_fn):
    cw = GRID_W
    r = l // cw
    c_r, s_r = _dft_cs(r)
    w1 = jnp.asarray(np.concatenate([c_r, -s_r], axis=0), BF16)
    ang = 2.0 * np.pi * (np.arange(cw)[:, None] * np.arange(r)[None, :]) / l
    tr = jnp.asarray(np.cos(ang)[:, :, None], F32)
    ti = jnp.asarray(-np.sin(ang)[:, :, None], F32)
    z = pl.pallas_call(
        functools.partial(_fn1_kernel, r=r),
        grid=(cw // FN_ROWS,),
        in_specs=[pl.BlockSpec((2 * r, r), lambda j: (0, 0)),
                  pl.BlockSpec((r, FN_ROWS, FN_WIDTH), lambda j: (0, j, 0)),
                  pl.BlockSpec((FN_ROWS, r, 1), lambda j: (j, 0, 0)),
                  pl.BlockSpec((FN_ROWS, r, 1), lambda j: (j, 0, 0))],
        out_specs=pl.BlockSpec((2 * r, FN_ROWS, FN_WIDTH), lambda j: (0, j, 0)),
        out_shape=jax.ShapeDtypeStruct((2 * r, cw, FN_WIDTH), F32),
        compiler_params=_params(("arbitrary",), 40 << 20),
        name="fourier_stage1",
    )(w1, u.reshape(u.shape[0] // cw, cw, FN_WIDTH), tr, ti)
    z2 = z.reshape(2 * r * cw, FN_WIDTH)
    c_w, s_w = _dft_cs(cw)
    eye = np.eye(FN_ROWS)
    kc = jnp.asarray(np.kron(eye, c_w), BF16)
    ks = jnp.asarray(np.kron(eye, s_w), BF16)
    c_c, s_c = _dft_cs(FN_GROUP_DIM)
    nb = r // FN_ROWS
    tb = FN_ROWS * cw
    out = pl.pallas_call(
        functools.partial(_fn2_kernel, scale=float((l * FN_GROUP_DIM) ** -0.5)),
        grid=(nb,),
        in_specs=[pl.BlockSpec((tb, FN_WIDTH), lambda b: (b, 0)),
                  pl.BlockSpec((tb, FN_WIDTH), lambda b: (nb + b, 0)),
                  pl.BlockSpec((tb, tb), lambda b: (0, 0)),
                  pl.BlockSpec((tb, tb), lambda b: (0, 0)),
                  pl.BlockSpec((FN_GROUP_DIM, FN_GROUP_DIM), lambda b: (0, 0)),
                  pl.BlockSpec((FN_GROUP_DIM, FN_GROUP_DIM), lambda b: (0, 0)),
                  pl.BlockSpec((FN_GROUPS, FN_GROUP_DIM, FN_GROUP_DIM), lambda b: (0, 0, 0))],
        out_specs=pl.BlockSpec((cw, FN_ROWS, FN_WIDTH), lambda b: (0, b, 0)),
        out_shape=jax.ShapeDtypeStruct((cw, r, FN_WIDTH), F32),
        compiler_params=_params(("arbitrary",), 32 << 20),
        name="fourier_stage2",
    )(z2, z2, kc, ks, jnp.asarray(c_c, BF16), jnp.asarray(s_c, BF16), w_fn)
    return out.reshape(l, FN_WIDTH)


def _fn_ctx_kernel(u_ref, cl_ref, sl_ref, c_ref, s_ref, w_ref, o_ref, *, scale):
    u = u_ref[...].astype(BF16)
    gr = _dot(cl_ref[...], u).astype(BF16)
    gi = (-_dot(sl_ref[...], u)).astype(BF16)
    for g in range(FN_GROUPS):
        lo = g * FN_GROUP_DIM
        f = _dot(gr[:, lo:lo + FN_GROUP_DIM], c_ref[...]) + _dot(gi[:, lo:lo + FN_GROUP_DIM], s_ref[...])
        o_ref[:, lo:lo + FN_GROUP_DIM] = _dot((f * scale).astype(BF16), w_ref[g].astype(BF16)).astype(o_ref.dtype)


def fourier_context(u, s, w_fn):
    c = u.shape[0] - s
    c_l, s_l = _dft_cs(c)
    c_c, s_c = _dft_cs(FN_GROUP_DIM)
    gd = FN_GROUP_DIM
    whole = lambda shape: pl.BlockSpec(shape, lambda i: (0,) * len(shape))
    return pl.pallas_call(
        functools.partial(_fn_ctx_kernel, scale=float((c * FN_GROUP_DIM) ** -0.5)),
        grid=(1,),
        in_specs=[pl.BlockSpec((c, FN_WIDTH), lambda i: (s // c, 0)), whole((c, c)), whole((c, c)),
                  whole((gd, gd)), whole((gd, gd)), whole((FN_GROUPS, gd, gd))],
        out_specs=whole((c, FN_WIDTH)),
        out_shape=jax.ShapeDtypeStruct((c, FN_WIDTH), BF16),
        compiler_params=_params(("arbitrary",), 16 << 20),
        name="fourier_context",
    )(u, jnp.asarray(c_l, BF16), jnp.asarray(s_l, BF16), jnp.asarray(c_c, BF16), jnp.asarray(s_c, BF16), w_fn)


def _split_hi_lo(x):
    hi = x.astype(BF16)
    return hi, (x - hi.astype(F32)).astype(BF16)


def _gla_chunk(q_ref, k_ref, v_ref, lr_ref, cos_ref, sin_ref, w2_ref, bg_ref, perm_ref, st_ref, o_ref, rev):
    cs = GLA_CHUNK
    sub = GLA_SUB
    nsub = cs // sub
    z = _dot(lr_ref[...].astype(BF16), w2_ref[...].astype(BF16)) + bg_ref[...]
    la = -(jnp.maximum(-z, 0.0) + jnp.log(1.0 + jnp.exp(-jnp.abs(z)))) * (1.0 / GLA_GATE_TEMP)
    ri = lax.broadcasted_iota(jnp.int32, (cs, cs), 0)
    ci = lax.broadcasted_iota(jnp.int32, (cs, cs), 1)
    tri = jnp.where((ci >= ri) if rev else (ci <= ri), 1.0, 0.0).astype(BF16)
    la_hi, la_lo = _split_hi_lo(la)
    bcum = _dot(tri, la_hi) + _dot(tri, la_lo)
    edge = 0 if rev else cs - 1
    row_id = lax.broadcasted_iota(jnp.int32, (cs, GLA_DKP), 0)
    sub_r = lax.broadcasted_iota(jnp.int32, (sub, GLA_DKP), 0)
    lane_c = lax.broadcasted_iota(jnp.int32, (sub, cs), 1)
    cos, sin = cos_ref[...], sin_ref[...]
    perm = perm_ref[...]
    qscale = GLA_DK ** -0.5
    for h in range(GLA_HEADS):
        ks_ = slice(h * GLA_DKP, (h + 1) * GLA_DKP)
        vs_ = slice(h * GLA_DV, (h + 1) * GLA_DV)
        qb, kb = q_ref[:, ks_], k_ref[:, ks_]
        q = (qb.astype(F32) * cos + _dot(qb, perm) * sin) * qscale
        k = kb.astype(F32) * cos + _dot(kb, perm) * sin
        v = v_ref[:, vs_]
        b = bcum[:, ks_]
        b_edge = b[edge:edge + 1, :]
        st = st_ref[h]
        o = _dot_nt((q * jnp.exp(b)).astype(BF16), st.astype(BF16))
        slabs = []
        for blk in range(nsub):
            lo = blk * sub
            q_i = q[lo:lo + sub]
            b_i = b[lo:lo + sub]
            k_i = k[lo:lo + sub]
            acc = jnp.zeros((sub, cs), F32)
            if rev and blk < nsub - 1:
                ref_row = b[lo + sub:lo + sub + 1, :]
                outside = row_id >= lo + sub
            elif (not rev) and blk > 0:
                ref_row = b[lo - 1:lo, :]
                outside = row_id < lo
            else:
                ref_row = None
            if ref_row is not None:
                qe = q_i * jnp.exp(b_i - ref_row)
                ke = jnp.where(outside, k * jnp.exp(jnp.where(outside, ref_row - b, 0.0)), 0.0)
                acc = acc + _dot_nt(qe.astype(BF16), ke.astype(BF16))
            for j in range(sub):
                keep = (sub_r <= j) if rev else (sub_r >= j)
                d = jnp.exp(jnp.where(keep, b_i - b_i[j:j + 1, :], NEG))
                col = jnp.sum(q_i * d * k_i[j:j + 1, :], axis=-1, keepdims=True)
                acc = acc + jnp.where(lane_c == lo + j, col, 0.0)
            slabs.append(acc)
        attn = jnp.concatenate(slabs, axis=0)
        o = o + _dot(attn.astype(BF16), v)
        o_ref[:, vs_] = o
        kend = (k * jnp.exp(b_edge - b)).astype(BF16)
        st_ref[h] = st * jnp.exp(b_edge) + _dot_tn(v, kend)


def _gla_kernel(*refs, nchunks):
    fwd, bwd = refs[0:6], refs[6:12]
    w2f_ref, bgf_ref, w2b_ref, bgb_ref, perm_ref, s0_ref = refs[12:18]
    of_ref, ob_ref, sfin_ref, st_ref = refs[18:22]
    step = pl.program_id(0)

    @pl.when(step == 0)
    def _():
        st_ref[...] = s0_ref[...]

    _gla_chunk(*fwd, w2f_ref, bgf_ref, perm_ref, st_ref.at[0], of_ref, False)
    _gla_chunk(*bwd, w2b_ref, bgb_ref, perm_ref, st_ref.at[1], ob_ref, True)

    @pl.when(step == nchunks - 1)
    def _():
        sfin_ref[...] = st_ref[...]


def gla_scan(u_qk, u_vr, lr, cos, sin, w2p, bgp, perm, s0, row0, l):
    n = l // GLA_CHUNK
    c0 = row0 // GLA_CHUNK
    cs = GLA_CHUNK
    kw = GLA_HEADS * GLA_DKP
    rank2 = lr.shape[1]
    st_shape = (2, GLA_HEADS, GLA_DV, GLA_DKP)
    ins = []
    for ch in (lambda s: c0 + s, lambda s: c0 + n - 1 - s):
        ins += [pl.BlockSpec((cs, kw), lambda s, ch=ch: (ch(s), 0)),
                pl.BlockSpec((cs, kw), lambda s, ch=ch: (ch(s), 1)),
                pl.BlockSpec((cs, GLA_VW), lambda s, ch=ch: (ch(s), 0)),
                pl.BlockSpec((cs, rank2), lambda s, ch=ch: (ch(s), 0)),
                pl.BlockSpec((cs, GLA_DKP), lambda s, ch=ch: (ch(s), 0)),
                pl.BlockSpec((cs, GLA_DKP), lambda s, ch=ch: (ch(s), 0))]
    for dr in (0, 1):
        ins += [pl.BlockSpec((None, rank2, kw), lambda s, dr=dr: (dr, 0, 0)),
                pl.BlockSpec((None, 1, kw), lambda s, dr=dr: (dr, 0, 0))]
    ins += [pl.BlockSpec((GLA_DKP, GLA_DKP), lambda s: (0, 0)),
            pl.BlockSpec(st_shape, lambda s: (0, 0, 0, 0))]
    seq = (u_qk, u_qk, u_vr, lr, cos, sin)
    return pl.pallas_call(
        functools.partial(_gla_kernel, nchunks=n),
        grid=(n,),
        in_specs=ins,
        out_specs=[pl.BlockSpec((cs, GLA_VW), lambda s: (s, 0)),
                   pl.BlockSpec((cs, GLA_VW), lambda s: (n - 1 - s, 0)),
                   pl.BlockSpec(st_shape, lambda s: (0, 0, 0, 0))],
        out_shape=[jax.ShapeDtypeStruct((l, GLA_VW), F32), jax.ShapeDtypeStruct((l, GLA_VW), F32),
                   jax.ShapeDtypeStruct(st_shape, F32)],
        scratch_shapes=[pltpu.VMEM(st_shape, F32)],
        compiler_params=_params(("arbitrary",), 40 << 20),
        name="gla_scan",
    )(*seq, *seq, w2p, bgp, w2p, bgp, perm, s0)


def _gla_out_kernel(of_ref, ob_ref, r_ref, g_ref, o_ref):
    o = of_ref[...] + ob_ref[...]
    g = g_ref[...]
    r = r_ref[...].astype(F32)
    for h in range(GLA_HEADS):
        sl = slice(h * GLA_DV, (h + 1) * GLA_DV)
        oh = o[:, sl]
        on = oh * lax.rsqrt(jnp.mean(oh * oh, axis=-1, keepdims=True) + EPS) * g[:, sl]
        o_ref[:, sl] = (on * _silu(r[:, sl])).astype(o_ref.dtype)


def gla_output(o_f, o_b, u_vr, row0, g_gla, tm=256):
    l = o_f.shape[0]
    tm = min(tm, l)
    b0 = row0 // tm
    return pl.pallas_call(
        _gla_out_kernel,
        grid=(l // tm,),
        in_specs=[pl.BlockSpec((tm, GLA_VW), lambda i: (i, 0)),
                  pl.BlockSpec((tm, GLA_VW), lambda i: (i, 0)),
                  pl.BlockSpec((tm, GLA_VW), lambda i: (b0 + i, 1)),
                  pl.BlockSpec((1, GLA_VW), lambda i: (0, 0))],
        out_specs=pl.BlockSpec((tm, GLA_VW), lambda i: (i, 0)),
        out_shape=jax.ShapeDtypeStruct((l, GLA_VW), BF16),
        compiler_params=_params(("arbitrary",), 24 << 20),
        name="gla_output",
    )(o_f, o_b, u_vr, g_gla.reshape(1, GLA_VW))


def rope_tables(n_tokens):
    seg = GLA_DK // 2
    half = seg // 2
    inv = ROPE_THETA ** (-jnp.arange(half, dtype=F32) / half)
    pos = jnp.arange(n_tokens)
    ang_r = (pos // GRID_W).astype(F32)[:, None] * inv
    ang_c = (pos % GRID_W).astype(F32)[:, None] * inv
    pad1 = jnp.ones((n_tokens, GLA_DKP - GLA_DK), F32)
    pad0 = jnp.zeros((n_tokens, GLA_DKP - GLA_DK), F32)
    cos = jnp.concatenate([jnp.cos(ang_r)] * 2 + [jnp.cos(ang_c)] * 2 + [pad1], axis=1)
    sin = jnp.concatenate([jnp.sin(ang_r)] * 2 + [jnp.sin(ang_c)] * 2 + [pad0], axis=1)
    return cos, sin


def rope_perm():
    seg = GLA_DK // 2
    half = seg // 2
    p = np.zeros((GLA_DKP, GLA_DKP), np.float32)
    for s0 in (0, seg):
        for j in range(half):
            p[s0 + half + j, s0 + j] = -1.0
            p[s0 + j, s0 + half + j] = 1.0
    return jnp.asarray(p, BF16)


def _pad_heads(w):
    lead = w.shape[:-1]
    w = w.reshape(lead + (GLA_HEADS, GLA_DK))
    w = jnp.pad(w, [(0, 0)] * len(lead) + [(0, 0), (0, GLA_DKP - GLA_DK)])
    return w.reshape(lead + (GLA_HEADS * GLA_DKP,))


MOE_TR = 256
MOE_GATHER_DEPTH = 4


def moe_plan(route):
    t = route.shape[0]
    tr = MOE_TR
    n_a = 2 * t
    e = route[:, :2].astype(jnp.int32).reshape(-1)
    iota = jnp.arange(n_a, dtype=jnp.int32)
    e_sorted, order = lax.sort((e, iota), num_keys=1)
    counts = jnp.sum((e[:, None] == jnp.arange(MOE_EXPERTS, dtype=jnp.int32)[None]).astype(jnp.int32), axis=0)
    padded = ((counts + tr - 1) // tr) * tr
    ends = jnp.cumsum(padded)
    starts = ends - padded
    shift = starts - (jnp.cumsum(counts) - counts)
    _, pos = lax.sort((order, iota + shift[e_sorted]), num_keys=1)
    npad = n_a + MOE_EXPERTS * tr
    ntiles = npad // tr
    tile_e = jnp.minimum(jnp.searchsorted(ends, jnp.arange(ntiles, dtype=jnp.int32) * tr, side="right"),
                         MOE_EXPERTS - 1).astype(jnp.int32)
    row = jnp.arange(npad, dtype=jnp.int32)
    per_row = lambda v: jnp.repeat(v[tile_e], tr)
    valid = (row - per_row(starts)) < per_row(counts)
    src = jnp.where(valid, order[jnp.clip(row - per_row(shift), 0, n_a - 1)] // 2, 0)
    n_used = (ends[-1] // tr).reshape(1).astype(jnp.int32)
    return src.astype(jnp.int32), tile_e, n_used, pos.astype(jnp.int32)


ROW_GROUP = 8


def _row_gather_start(idx_ref, first, src_hbm, dst, sem):
    def body(g, carry):
        for u in range(ROW_GROUP):
            pltpu.make_async_copy(src_hbm.at[pl.ds(idx_ref[first + g * ROW_GROUP + u], 1)],
                                  dst.at[g, pl.ds(u, 1)], sem).start(priority=u % 2)
        return carry
    lax.fori_loop(0, dst.shape[0], body, 0)


def _row_gather_wait(dst, sem):
    pltpu.make_async_copy(dst, dst, sem).wait()


def _rows(x):
    return x.reshape(x.shape[0] * ROW_GROUP, x.shape[2])


def _moe_ffn_kernel(src_ref, te_ref, nu_ref, x_hbm, wgu_ref, wdn_ref, o_ref, buf, sem):
    i = pl.program_id(0)
    n_used = nu_ref[0]
    tr = MOE_TR
    depth = buf.shape[0]
    slot = lax.rem(i, depth)

    @pl.when(i == 0)
    def _():
        for ahead in range(depth - 1):
            @pl.when(ahead < n_used)
            def _(ahead=ahead):
                _row_gather_start(src_ref, ahead * tr, x_hbm, buf.at[ahead], sem.at[ahead])

    @pl.when(i < n_used)
    def _():
        nxt = i + depth - 1

        @pl.when(nxt < n_used)
        def _():
            s_ = lax.rem(nxt, depth)
            _row_gather_start(src_ref, nxt * tr, x_hbm, buf.at[s_], sem.at[s_])

        _row_gather_wait(buf.at[slot], sem.at[slot])
        x_lo, x_hi = _unpack_bf16_pairs(_rows(buf[slot]))
        half = x_lo.shape[1]
        gu = (_dot(x_lo.astype(BF16), wgu_ref[:half, :].astype(BF16))
              + _dot(x_hi.astype(BF16), wgu_ref[half:, :].astype(BF16)))
        hid = _silu(gu[:, :MOE_HIDDEN]) * gu[:, MOE_HIDDEN:]
        o_ref[...] = _pack_bf16_pairs(_dot(hid.astype(BF16), wdn_ref[...].astype(BF16)))

    @pl.when(i >= n_used)
    def _():
        o_ref[...] = jnp.zeros(o_ref.shape, o_ref.dtype)


def moe_ffn(xn, w_gu, w_dn, l, src, tile_e, n_used):
    t = xn.shape[0]
    d = 2 * xn.shape[1]
    tr = MOE_TR
    npad = src.shape[0]
    f = MOE_HIDDEN
    grid_spec = pltpu.PrefetchScalarGridSpec(
        num_scalar_prefetch=3,
        grid=(npad // tr,),
        in_specs=[pl.BlockSpec(memory_space=pl.ANY),
                  pl.BlockSpec((None, None, d, 2 * f), lambda i, s, te, nu: (l, te[i], 0, 0)),
                  pl.BlockSpec((None, None, f, d), lambda i, s, te, nu: (l, te[i], 0, 0))],
        out_specs=pl.BlockSpec((tr, d // 2), lambda i, s, te, nu: (i, 0)),
        scratch_shapes=[pltpu.VMEM((MOE_GATHER_DEPTH, tr // ROW_GROUP, ROW_GROUP, d // 2), jnp.uint32),
                        pltpu.SemaphoreType.DMA((MOE_GATHER_DEPTH,))],
    )
    return pl.pallas_call(
        _moe_ffn_kernel,
        grid_spec=grid_spec,
        out_shape=jax.ShapeDtypeStruct((npad, d // 2), jnp.uint32),
        compiler_params=_params(("arbitrary",), 4 * tr * d * 4 + 2 * 3 * d * f * 4 + 6 * tr * d * 4),
        name="moe_ffn",
    )(src, tile_e, n_used, xn, w_gu, w_dn)


def _moe_combine_kernel(pos_ref, y_hbm, x_ref, route_ref, gate_ref, o_ref, buf, sem):
    i = pl.program_id(0)
    n = pl.num_programs(0)
    tm = x_ref.shape[0]
    depth = buf.shape[0]
    slot = lax.rem(i, depth)

    def start(tile):
        s = lax.rem(tile, depth)
        _row_gather_start(pos_ref, 2 * tm * tile, y_hbm, buf.at[s], sem.at[s])

    @pl.when(i == 0)
    def _():
        for ahead in range(depth - 1):
            @pl.when(ahead < n)
            def _(ahead=ahead):
                start(ahead)

    @pl.when(i + depth - 1 < n)
    def _():
        start(i + depth - 1)

    _row_gather_wait(buf.at[slot], sem.at[slot])
    route = route_ref[...]
    lane = lax.broadcasted_iota(jnp.int32, route.shape, 1)
    w1 = _lane_pick(route, lane, 2)
    w2 = _lane_pick(route, lane, 3)
    cur = buf.at[slot]
    half = cur.shape[2]
    gm = tm // ROW_GROUP
    cw = min(4 * LANE, half)
    for c0 in range(0, half, cw):
        first = _unpack_bf16_pairs(_rows(cur[:gm, :, c0:c0 + cw]))
        second = _unpack_bf16_pairs(_rows(cur[gm:, :, c0:c0 + cw]))
        for part in (0, 1):
            cols = slice(part * half + c0, part * half + c0 + cw)
            o_ref[:, cols] = x_ref[:, cols] + gate_ref[:, cols] * (w1 * first[part] + w2 * second[part])


def moe_combine(y, x, route, gate, pos, row0, tm=128):
    t, d = x.shape
    pos = pos[2 * row0:2 * (row0 + t)].reshape(t // tm, tm, 2).transpose(0, 2, 1).reshape(-1)
    blk0 = row0 // tm
    grid_spec = pltpu.PrefetchScalarGridSpec(
        num_scalar_prefetch=1,
        grid=(t // tm,),
        in_specs=[pl.BlockSpec(memory_space=pl.ANY),
                  pl.BlockSpec((tm, d), lambda i, p: (i, 0)),
                  pl.BlockSpec((tm, LANE), lambda i, p: (i + blk0, 0)),
                  pl.BlockSpec((1, d), lambda i, p: (0, 0))],
        out_specs=pl.BlockSpec((tm, d), lambda i, p: (i, 0)),
        scratch_shapes=[pltpu.VMEM((MOE_GATHER_DEPTH, 2 * tm // ROW_GROUP, ROW_GROUP, d // 2), jnp.uint32),
                        pltpu.SemaphoreType.DMA((MOE_GATHER_DEPTH,))],
    )
    return pl.pallas_call(
        _moe_combine_kernel,
        grid_spec=grid_spec,
        out_shape=jax.ShapeDtypeStruct((t, d), F32),
        compiler_params=_params(("arbitrary",), 4 * tm * d * 4 + 8 * tm * d * 4),
        name="moe_combine",
    )(pos, y, x, route, gate.reshape(1, d))


def _layer(l, x, xc, mod, last, consts, w):
    (g_mix, w_in, rpb, w_fn, w_g2, b_g, g_gla, w_out, g_ffn, w_rg, b_rg, w_re, b_re, w_gu, w_dn) = w
    cos, sin, perm, na_tab = consts
    d = D_MODEL
    s_len, c_len = x.shape[0], xc.shape[0]
    m_x = mod[l, 0].reshape(N_MOD, d)
    m_c = mod[l, 1].reshape(N_MOD, d)

    xn, lr = norm_proj([(x, m_x[0], m_x[1]), (xc, m_c[0], m_c[1])], g_mix[l], w_in, l, OFF_LR, 2 * GLA_GATE_RANK)
    tm = (s_len + c_len) // 8
    u_na = matmul(xn, w_in, (l,), OFF_NA, 3 * NA_WIDTH, 512, tm, BF16, name="in_na")
    u_fn = matmul(xn, w_in, (l,), OFF_FN, FN_WIDTH, 512, tm, F32, name="in_fn")
    u_qk = gla_qk_proj(xn, w_in, l, tm)
    u_vr = matmul(xn, w_in, (l,), OFF_GVR, 2 * GLA_VW, 512, tm, BF16, name="in_gvr")

    r_ = GLA_GATE_RANK
    w2p = jnp.stack([jnp.pad(_pad_heads(w_g2[l, dr]), ((dr * r_, (1 - dr) * r_), (0, 0))) for dr in (0, 1)])
    bgp = _pad_heads(b_g[l]).reshape(2, 1, -1)
    zero = jnp.zeros((2, GLA_HEADS, GLA_DV, GLA_DKP), F32)
    *outs_c, s_c = gla_scan(u_qk, u_vr, lr, cos, sin, w2p, bgp, perm, zero, s_len, c_len)
    *outs_x, _ = gla_scan(u_qk, u_vr, lr, cos, sin, w2p, bgp, perm, s_c, 0, s_len)

    mix = (neighbourhood_attention(u_na, s_len, na_tab, l),
           fourier_latent(u_fn, s_len, w_fn[l]),
           gla_output(outs_x[0], outs_x[1], u_vr, 0, g_gla[l]))
    x = matmul(mix, w_out, (l,), 0, d, 512, 1024, F32, res=x, gate=m_x[2], name="out_proj")

    w_route = jnp.pad(jnp.concatenate([w_re[l], w_rg[l]], axis=1),
                      ((0, 0), (0, LANE - MOE_EXPERTS - MOE_GROUPS)))
    b_route = jnp.pad(jnp.concatenate([b_re[l], b_rg[l]]), (0, LANE - MOE_EXPERTS - MOE_GROUPS)).reshape(1, LANE)
    streams = [(x, m_x)]
    if not last:
        mix_c = (full_attention(u_na, s_len), fourier_context(u_fn, s_len, w_fn[l]),
                 gla_output(outs_c[0], outs_c[1], u_vr, s_len, g_gla[l]))
        xc = matmul(mix_c, w_out, (l,), 0, d, 512, 1024, F32, res=xc, gate=m_c[2], name="out_proj_ctx")
        streams.append((xc, m_c))

    yn, route = norm_route([(y, m[3], m[4]) for y, m in streams], g_ffn[l], w_route, b_route)
    src, tile_e, n_used, pos = moe_plan(route)
    y_exp = moe_ffn(yn, w_gu, w_dn, l, src, tile_e, n_used)
    outs, row0 = [], 0
    for y, m in streams:
        outs.append(moe_combine(y_exp, y, route, m[5], pos, row0))
        row0 += y.shape[0]
    return outs[0], (outs[1] if not last else xc)


def kernel(x, c, ctx, c_ctx, w_ada, b_ada, g_mix, w_in, rpb, w_fn, w_g2, b_g, g_gla, w_out,
           g_ffn, w_rg, b_rg, w_re, b_re, w_gu, w_dn, g_final):
    assert x.shape[0] == 1 and c.shape[0] == 1
    d = D_MODEL
    s_len = x.shape[1]
    c_len = ctx.shape[1]
    cc = jnp.concatenate([c, c_ctx[None], jnp.zeros((6, d), F32)], axis=0)
    mod = ada_mod(cc, w_ada, b_ada)
    cos, sin = rope_tables(s_len)
    cos = jnp.concatenate([cos, jnp.ones((c_len, GLA_DKP), F32)], axis=0)
    sin = jnp.concatenate([sin, jnp.zeros((c_len, GLA_DKP), F32)], axis=0)
    consts = (cos, sin, rope_perm(), na_tables(rpb, s_len // GRID_W))
    w_in = jnp.swapaxes(w_in, 1, 2)
    w = (g_mix, w_in, rpb, w_fn, w_g2, b_g, g_gla, w_out, g_ffn, w_rg, b_rg, w_re, b_re, w_gu, w_dn)
    xs, xc = x[0], ctx[0]
    for l in range(DEPTH):
        xs, xc = _layer(l, xs, xc, mod, l == DEPTH - 1, consts, w)
    return final_norm(xs, g_final)[None]
```
